```python
import jax, jax.numpy as jnp
from jax import lax
import numpy as np

D_MODEL = 1024
BATCH = 8
SEQ = 2048
DEPTH = 2

CONV_GROUPS = 4
CONV_GROUP_DIM = 128
CONV_DIM = CONV_GROUPS * CONV_GROUP_DIM
CONV_WIDTH = 31
SGU_HEADS = 4
SGU_HEAD_DIM = 128
SGU_DIM = SGU_HEADS * SGU_HEAD_DIM
CHUNK = 128
EVEN_IN = 2 * CONV_DIM + 2 * SGU_DIM
EVEN_MIX = CONV_DIM + SGU_DIM

MLA_HEADS = 8
MLA_Q_RANK = 256
MLA_KV_RANK = 128
MLA_NOPE = 64
MLA_ROPE = 32
MLA_V = 64
ROPE_THETA = 10000.0
SWA_HEADS = 8
SWA_KV_HEADS = 2
SWA_HEAD_DIM = 64
WINDOW = 128
BLOCK = 128
ODD_SPLITS = (MLA_Q_RANK, MLA_KV_RANK, MLA_ROPE, SWA_HEADS * SWA_HEAD_DIM,
              SWA_KV_HEADS * SWA_HEAD_DIM, SWA_KV_HEADS * SWA_HEAD_DIM)
ODD_IN = sum(ODD_SPLITS)
ODD_MIX = MLA_HEADS * MLA_V + SWA_HEADS * SWA_HEAD_DIM

N_GROUPS = 4
EXPERTS_PER_GROUP = 8
TOP_K = 2
D_EXPERT = 256

EPS = 1e-6
NEG_INF = -1e30

kernel_name = "hybrid_conv_sgu_mla_swa_hmoe_encoder"


def _rmsnorm(x, g):
    xf = x.astype(jnp.float32)
    y = xf * lax.rsqrt(jnp.mean(xf * xf, axis=-1, keepdims=True) + EPS)
    return (y * g.astype(jnp.float32)).astype(x.dtype)


def _layernorm(x, g, b):
    xf = x.astype(jnp.float32)
    mu = jnp.mean(xf, axis=-1, keepdims=True)
    var = jnp.mean(jnp.square(xf - mu), axis=-1, keepdims=True)
    y = (xf - mu) * lax.rsqrt(var + EPS)
    return (y * g.astype(jnp.float32) + b.astype(jnp.float32)).astype(x.dtype)


def _split_at(p, sizes):
    offs = np.cumsum(sizes)[:-1].tolist()
    return jnp.split(p, offs, axis=-1)


def _conv_sgu_mixer(h, w_in, conv_w, conv_b, cnorm_g, cnorm_b, vnorm_g, vnorm_b, w_sp, b_sp, w_out):
    B, S, _ = h.shape
    p = h @ w_in
    a_val, a_gate, u, v = _split_at(p, (CONV_DIM, CONV_DIM, SGU_DIM, SGU_DIM))
    a = a_val * jax.nn.sigmoid(a_gate)
    a = lax.conv_general_dilated(a, conv_w[:, None, :], window_strides=(1,),
                                 padding=[(CONV_WIDTH // 2, CONV_WIDTH // 2)],
                                 dimension_numbers=('NWC', 'WIO', 'NWC'),
                                 feature_group_count=CONV_DIM) + conv_b
    a = _layernorm(a.reshape(B, S, CONV_GROUPS, CONV_GROUP_DIM),
                   cnorm_g.reshape(CONV_GROUPS, CONV_GROUP_DIM),
                   cnorm_b.reshape(CONV_GROUPS, CONV_GROUP_DIM)).reshape(B, S, CONV_DIM)
    a = jax.nn.silu(a)
    nc = S // CHUNK
    u = jax.nn.gelu(u).reshape(B, nc, CHUNK, SGU_HEADS, SGU_HEAD_DIM)
    v = jax.nn.gelu(v).reshape(B, nc, CHUNK, SGU_HEADS, SGU_HEAD_DIM)
    v = _layernorm(v, vnorm_g.reshape(SGU_HEADS, SGU_HEAD_DIM), vnorm_b.reshape(SGU_HEADS, SGU_HEAD_DIM))
    sv = jnp.einsum('hts,bnshc->bnthc', w_sp, v) + jnp.swapaxes(b_sp, 0, 1)[:, :, None]
    s_out = (u * sv).reshape(B, S, SGU_DIM)
    return jnp.concatenate([a, s_out], axis=-1) @ w_out


def _rope_tables(S, dim):
    pos = jnp.arange(S, dtype=jnp.float32)
    inv = 1.0 / (ROPE_THETA ** (jnp.arange(0, dim, 2, dtype=jnp.float32) / dim))
    ang = pos[:, None] * inv[None, :]
    ang = jnp.concatenate([ang, ang], axis=-1)
    return jnp.cos(ang), jnp.sin(ang)


def _apply_rope(x, cos, sin):
    xf = x.astype(jnp.float32)
    x1, x2 = jnp.split(xf, 2, axis=-1)
    rot = jnp.concatenate([-x2, x1], axis=-1)
    return (xf * cos + rot * sin).astype(x.dtype)


def _alibi_slopes(n_heads):
    return jnp.exp2(-8.0 * jnp.arange(1, n_heads + 1, dtype=jnp.float32) / n_heads)


def _dense_attention(q, k, v):
    B, S, H, dqk = q.shape
    scale = dqk ** -0.5
    nb = S // BLOCK
    qb = q.reshape(B, nb, BLOCK, H, dqk).transpose(1, 0, 2, 3, 4)

    def one_block(qblk):
        s = jnp.einsum('bqhd,bkhd->bhqk', qblk, k).astype(jnp.float32) * scale
        p = jax.nn.softmax(s, axis=-1).astype(v.dtype)
        return jnp.einsum('bhqk,bkhd->bqhd', p, v)

    out = lax.map(one_block, qb)
    return out.transpose(1, 0, 2, 3, 4).reshape(B, S, H, v.shape[-1])


def _window_gqa_attention(q, k, v, sink):
    B, S, Hq, dh = q.shape
    G = k.shape[2]
    R = Hq // G
    nb = S // BLOCK
    span = BLOCK + 2 * WINDOW
    qb = q.reshape(B, nb, BLOCK, G, R, dh)
    idx = jnp.arange(nb)[:, None] * BLOCK + jnp.arange(span)[None, :]
    kp = jnp.pad(k, ((0, 0), (WINDOW, WINDOW), (0, 0), (0, 0)))
    vp = jnp.pad(v, ((0, 0), (WINDOW, WINDOW), (0, 0), (0, 0)))
    kb = kp[:, idx]
    vb = vp[:, idx]
    s = jnp.einsum('bnqgrd,bnkgd->bngrqk', qb, kb).astype(jnp.float32) * (dh ** -0.5)
    qi = jnp.arange(BLOCK)
    kj = jnp.arange(span)
    dist = qi[:, None] + WINDOW - kj[None, :]
    in_win = jnp.abs(dist) <= WINDOW
    kpos = jnp.arange(nb)[:, None] * BLOCK - WINDOW + kj[None, :]
    in_seq = (kpos >= 0) & (kpos < S)
    mask = in_win[None] & in_seq[:, None, :]
    slopes = _alibi_slopes(Hq).reshape(G, R)
    bias = -slopes[:, :, None, None] * jnp.abs(dist).astype(jnp.float32)
    s = jnp.where(mask[None, :, None, None], s + bias, NEG_INF)
    sk = sink.astype(jnp.float32).reshape(G, R)[:, :, None]
    m = jnp.maximum(jnp.max(s, axis=-1), sk)
    e = jnp.exp(s - m[..., None])
    denom = jnp.sum(e, axis=-1) + jnp.exp(sk - m)
    pr = (e / denom[..., None]).astype(v.dtype)
    o = jnp.einsum('bngrqk,bnkgd->bnqgrd', pr, vb)
    return o.reshape(B, S, Hq, dh)


def _mla_swa_mixer(h, w_in, g_cq, w_uq, g_ckv, w_ukv, sink, w_out):
    B, S, _ = h.shape
    p = h @ w_in
    c_q, c_kv, k_pe, q_s, k_s, v_s = _split_at(p, ODD_SPLITS)
    cos, sin = _rope_tables(S, MLA_ROPE)
    q = (_rmsnorm(c_q, g_cq) @ w_uq).reshape(B, S, MLA_HEADS, MLA_NOPE + MLA_ROPE)
    q_nope, q_pe = jnp.split(q, [MLA_NOPE], axis=-1)
    q_pe = _apply_rope(q_pe, cos[:, None, :], sin[:, None, :])
    kv = (_rmsnorm(c_kv, g_ckv) @ w_ukv).reshape(B, S, MLA_HEADS, MLA_NOPE + MLA_V)
    k_nope, v_mla = jnp.split(kv, [MLA_NOPE], axis=-1)
    k_pe = _apply_rope(k_pe, cos, sin)
    q_full = jnp.concatenate([q_nope, q_pe], axis=-1)
    k_full = jnp.concatenate([k_nope, jnp.broadcast_to(k_pe[:, :, None, :], (B, S, MLA_HEADS, MLA_ROPE))], axis=-1)
    mla = _dense_attention(q_full, k_full, v_mla).reshape(B, S, MLA_HEADS * MLA_V)
    swa = _window_gqa_attention(q_s.reshape(B, S, SWA_HEADS, SWA_HEAD_DIM),
                                k_s.reshape(B, S, SWA_KV_HEADS, SWA_HEAD_DIM),
                                v_s.reshape(B, S, SWA_KV_HEADS, SWA_HEAD_DIM),
                                sink).reshape(B, S, SWA_HEADS * SWA_HEAD_DIM)
    return jnp.concatenate([mla, swa], axis=-1) @ w_out


def _hier_moe(h, w_group, b_group, w_router, b_router, w_gate, w_up, w_down):
    B, S, D = h.shape
    f32 = jnp.float32
    t = h.reshape(B * S, D)
    group_prob = jax.nn.softmax((t @ w_group).astype(f32) + b_group.astype(f32), axis=-1)
    g_p, g_idx = lax.top_k(group_prob, 1)
    exp_logits = jnp.einsum('td,gde->tge', t, w_router).astype(f32) + b_router.astype(f32)
    sel = jnp.take_along_axis(exp_logits, g_idx[:, :, None], axis=1)[:, 0]
    top_v, top_i = lax.top_k(sel, TOP_K)
    top_w = jax.nn.softmax(top_v, axis=-1) * g_p
    within = jnp.einsum('tk,tke->te', top_w, jax.nn.one_hot(top_i, EXPERTS_PER_GROUP, dtype=f32))
    gate = jax.nn.one_hot(g_idx[:, 0], N_GROUPS, dtype=f32)[:, :, None] * within[:, None, :]
    y = jnp.zeros((B * S, D), f32)
    for g in range(N_GROUPS):
        hid = jax.nn.silu(jnp.einsum('td,edf->tef', t, w_gate[g])) * jnp.einsum('td,edf->tef', t, w_up[g])
        hid = hid * gate[:, g, :, None].astype(hid.dtype)
        y = y + jnp.einsum('tef,efd->td', hid, w_down[g]).astype(f32)
    return y.astype(h.dtype).reshape(B, S, D)


def setup_inputs(seed: int = 0) -> dict:
    key = jax.random.key(seed)
    ks = list(jax.random.split(key, 32))
    f32 = jnp.float32
    ne = (DEPTH + 1) // 2
    no = DEPTH // 2

    def nrm(shape, scale):
        return jax.random.normal(ks.pop(), shape, f32) * scale

    def gain(shape):
        return 1.0 + nrm(shape, 0.1)

    D = D_MODEL
    ng, ne_g, F = N_GROUPS, EXPERTS_PER_GROUP, D_EXPERT
    return {
        "x": nrm((BATCH, SEQ, D), 1.0),
        "norm_mix": gain((DEPTH, D)),
        "norm_ffn": gain((DEPTH, D)),
        "norm_final": gain((D,)),
        "ev_w_in": nrm((ne, D, EVEN_IN), D ** -0.5),
        "ev_conv_w": nrm((ne, CONV_WIDTH, CONV_DIM), CONV_WIDTH ** -0.5),
        "ev_conv_b": nrm((ne, CONV_DIM), 0.02),
        "ev_cnorm_g": gain((ne, CONV_DIM)),
        "ev_cnorm_b": nrm((ne, CONV_DIM), 0.02),
        "ev_vnorm_g": gain((ne, SGU_DIM)),
        "ev_vnorm_b": nrm((ne, SGU_DIM), 0.02),
        "ev_w_sp": nrm((ne, SGU_HEADS, CHUNK, CHUNK), CHUNK ** -0.5),
        "ev_b_sp": gain((ne, SGU_HEADS, CHUNK)),
        "ev_w_out": nrm((ne, EVEN_MIX, D), EVEN_MIX ** -0.5),
        "od_w_in": nrm((no, D, ODD_IN), D ** -0.5),
        "od_g_cq": gain((no, MLA_Q_RANK)),
        "od_w_uq": nrm((no, MLA_Q_RANK, MLA_HEADS * (MLA_NOPE + MLA_ROPE)), MLA_Q_RANK ** -0.5),
        "od_g_ckv": gain((no, MLA_KV_RANK)),
        "od_w_ukv": nrm((no, MLA_KV_RANK, MLA_HEADS * (MLA_NOPE + MLA_V)), MLA_KV_RANK ** -0.5),
        "od_sink": nrm((no, SWA_HEADS), 1.0),
        "od_w_out": nrm((no, ODD_MIX, D), ODD_MIX ** -0.5),
        "moe_w_group": nrm((DEPTH, D, ng), D ** -0.5),
        "moe_b_group": nrm((DEPTH, ng), 0.01),
        "moe_w_router": nrm((DEPTH, ng, D, ne_g), D ** -0.5),
        "moe_b_router": nrm((DEPTH, ng, ne_g), 0.01),
        "moe_w_gate": nrm((DEPTH, ng, ne_g, D, F), D ** -0.5),
        "moe_w_up": nrm((DEPTH, ng, ne_g, D, F), D ** -0.5),
        "moe_w_down": nrm((DEPTH, ng, ne_g, F, D), F ** -0.5),
    }


def reference(x, norm_mix, norm_ffn, norm_final,
              ev_w_in, ev_conv_w, ev_conv_b, ev_cnorm_g, ev_cnorm_b, ev_vnorm_g, ev_vnorm_b,
              ev_w_sp, ev_b_sp, ev_w_out,
              od_w_in, od_g_cq, od_w_uq, od_g_ckv, od_w_ukv, od_sink, od_w_out,
              moe_w_group, moe_b_group, moe_w_router, moe_b_router, moe_w_gate, moe_w_up, moe_w_down):
    for layer in range(DEPTH):
        i = layer // 2
        h = _rmsnorm(x, norm_mix[layer])
        if layer % 2 == 0:
            x = x + _conv_sgu_mixer(h, ev_w_in[i], ev_conv_w[i], ev_conv_b[i], ev_cnorm_g[i], ev_cnorm_b[i],
                                    ev_vnorm_g[i], ev_vnorm_b[i], ev_w_sp[i], ev_b_sp[i], ev_w_out[i])
        else:
            x = x + _mla_swa_mixer(h, od_w_in[i], od_g_cq[i], od_w_uq[i], od_g_ckv[i], od_w_ukv[i],
                                   od_sink[i], od_w_out[i])
        h = _rmsnorm(x, norm_ffn[layer])
        x = x + _hier_moe(h, moe_w_group[layer], moe_b_group[layer], moe_w_router[layer], moe_b_router[layer],
                          moe_w_gate[layer], moe_w_up[layer], moe_w_down[layer])
    return _rmsnorm(x, norm_final)
```

```python
import functools

import jax
import jax.numpy as jnp
from jax import lax
from jax.experimental import pallas as pl
from jax.experimental.pallas import tpu as pltpu

F32 = jnp.float32
BF16 = jnp.bfloat16
I32 = jnp.int32

EPS = 1e-6
NEG_INF = -1e30
LANE = 128
VMEM_LIMIT = 56 * 1024 * 1024

CONV_DIM = 512
CONV_GROUP = 128
CONV_WIDTH = 31
SGU_DIM = 512
SGU_HEAD = 128
CHUNK = 128
MLA_HEADS = 8
MLA_Q_RANK = 256
MLA_KV_RANK = 128
MLA_NOPE = 64
MLA_ROPE = 32
MLA_V = 64
ROPE_THETA = 10000.0
SWA_HEADS = 8
SWA_KV_HEADS = 2
SWA_DIM = 64
WINDOW = 128
N_GROUPS = 4
N_EXP = 8
N_EXPERTS = N_GROUPS * N_EXP

HALO = 16
CONV_ROWS = 64
MOE_TILE = 256
ROW_TILE = 256


def _dot(a, b):
    return jnp.dot(a, b, preferred_element_type=F32)


def _dot_nt(a, b):
    return lax.dot_general(a, b, (((1,), (1,)), ((), ())), preferred_element_type=F32)


def _rms(x, g):
    return x * lax.rsqrt(jnp.mean(x * x, axis=-1, keepdims=True) + EPS) * g


def _ln(x, g, b):
    mu = jnp.mean(x, axis=-1, keepdims=True)
    d = x - mu
    var = jnp.mean(d * d, axis=-1, keepdims=True)
    return d * lax.rsqrt(var + EPS) * g + b


def _gelu(x):
    return 0.5 * x * (1.0 + jnp.tanh(0.7978845608028654 * (x + 0.044715 * (x * x * x))))


def _sigmoid(x):
    return 1.0 / (1.0 + jnp.exp(-x))


def _router_epilogue(xn, gffn_ref, wrh_ref, wrl_ref, br_ref, h3_ref, lg_ref):
    h = _rms(xn, gffn_ref[...])
    for s in range(h.shape[1] // LANE):
        h3_ref[:, s, :] = h[:, s * LANE:(s + 1) * LANE]
    hi = h.astype(BF16)
    lo = (h - hi.astype(F32)).astype(BF16)
    wh = wrh_ref[...]
    lg_ref[...] = _dot(hi, wh) + _dot(lo, wh) + _dot(hi, wrl_ref[...]) + br_ref[...]


def _even_kernel(xc_ref, xp_ref, gmix_ref, win_ref, cw_ref, cb_ref, cng_ref, cnb_ref, vng_ref,
                 vnb_ref, wsp_ref, bsp_ref, wout_ref, gffn_ref, wrh_ref, wrl_ref, br_ref,
                 xo_ref, h3_ref, lg_ref, a_s, mix_s, *, ns, sub, seq):
    j = pl.program_id(1)

    @pl.when(j == 0)
    def _():
        a_s[0:HALO, :] = jnp.zeros((HALO, CONV_DIM), F32)

    @pl.when(j < ns)
    def _proj():
        r0 = pl.multiple_of(j * sub, sub)
        h = _rms(xc_ref[...], gmix_ref[...]).astype(BF16)
        pa = _dot(h, win_ref[:, 0:2 * CONV_DIM])
        a_s[pl.ds(HALO + r0, sub), :] = pa[:, :CONV_DIM] * _sigmoid(pa[:, CONV_DIM:])
        u = _gelu(_dot(h, win_ref[:, 2 * CONV_DIM:2 * CONV_DIM + SGU_DIM]))
        v = _gelu(_dot(h, win_ref[:, 2 * CONV_DIM + SGU_DIM:]))
        for hd in range(SGU_DIM // SGU_HEAD):
            cs = slice(hd * SGU_HEAD, (hd + 1) * SGU_HEAD)
            vn = _ln(v[:, cs], vng_ref[:, cs], vnb_ref[:, cs]).astype(BF16)
            for c in range(sub // CHUNK):
                rs = slice(c * CHUNK, (c + 1) * CHUNK)
                sv = _dot(wsp_ref[hd], vn[rs, :]) + bsp_ref[hd]
                mix_s[pl.ds(r0 + c * CHUNK, CHUNK),
                      CONV_DIM + hd * SGU_HEAD:CONV_DIM + (hd + 1) * SGU_HEAD] = (u[rs, cs] * sv).astype(BF16)

    @pl.when(j == ns)
    def _():
        a_s[HALO + seq:2 * HALO + seq, :] = jnp.zeros((HALO, CONV_DIM), F32)

    @pl.when(j >= 1)
    def _out():
        r0 = (j - 1) * sub
        win_rows = CONV_ROWS + 2 * HALO

        def chunk(i, carry):
            base = pl.multiple_of(r0 + i * CONV_ROWS, CONV_ROWS)
            win = a_s[pl.ds(base, win_rows), :]
            acc = jnp.zeros((CONV_ROWS, CONV_DIM), F32)
            for s in range(8):
                sh = win if s == 0 else pltpu.roll(win, win_rows - s, 0)
                for m in range(4):
                    k = 8 * m + s - 1
                    if 0 <= k < CONV_WIDTH:
                        acc = acc + sh[8 * m:8 * m + CONV_ROWS, :] * cw_ref[k:k + 1, :]
            acc = acc + cb_ref[...]
            for g in range(CONV_DIM // CONV_GROUP):
                cs = slice(g * CONV_GROUP, (g + 1) * CONV_GROUP)
                y = _ln(acc[:, cs], cng_ref[:, cs], cnb_ref[:, cs])
                mix_s[pl.ds(base, CONV_ROWS), cs] = (y * _sigmoid(y)).astype(BF16)
            return carry

        lax.fori_loop(0, sub // CONV_ROWS, chunk, 0)
        mix = mix_s[pl.ds(pl.multiple_of(r0, sub), sub), :]
        xn = xp_ref[...] + _dot(mix, wout_ref[...])
        xo_ref[...] = xn
        _router_epilogue(xn, gffn_ref, wrh_ref, wrl_ref, br_ref, h3_ref, lg_ref)


def _const_spec(shape):
    nd = len(shape)
    return pl.BlockSpec(shape, lambda *_: (0,) * nd)


def _even_mixer(x, seq, gmix, win, cw, cb, cng, cnb, vng, vnb, wsp, bsp, wout, gffn, wrh, wrl, br):
    t, d = x.shape
    nb = t // seq
    sub = 512
    ns = seq // sub
    row = lambda b, j: (b * ns + jnp.minimum(j, ns - 1), 0)
    prev = lambda b, j: (b * ns + jnp.maximum(j - 1, 0), 0)
    prev3 = lambda b, j: (b * ns + jnp.maximum(j - 1, 0), 0, 0)
    consts = (gmix, win, cw, cb, cng, cnb, vng, vnb, wsp, bsp, wout, gffn, wrh, wrl, br)
    return pl.pallas_call(
        functools.partial(_even_kernel, ns=ns, sub=sub, seq=seq),
        grid=(nb, ns + 1),
        in_specs=[pl.BlockSpec((sub, d), row), pl.BlockSpec((sub, d), prev)]
        + [_const_spec(c.shape) for c in consts],
        out_specs=[pl.BlockSpec((sub, d), prev),
                   pl.BlockSpec((sub, d // LANE, LANE), prev3),
                   pl.BlockSpec((sub, LANE), prev)],
        out_shape=[jax.ShapeDtypeStruct((t, d), F32),
                   jax.ShapeDtypeStruct((t, d // LANE, LANE), F32),
                   jax.ShapeDtypeStruct((t, LANE), F32)],
        scratch_shapes=[pltpu.VMEM((seq + 2 * HALO, CONV_DIM), F32),
                        pltpu.VMEM((seq, CONV_DIM + SGU_DIM), BF16)],
        compiler_params=pltpu.CompilerParams(
            dimension_semantics=("arbitrary", "arbitrary"), vmem_limit_bytes=VMEM_LIMIT),
        name="even_mixer",
    )(x, x, *consts)


def _odd_proj_kernel(x_ref, gmix_ref, win_ref, gcq_ref, wuq_ref, gckv_ref, wuk_ref, wuv_ref,
                     cos_ref, sa_ref, sb_ref, qm_ref, km_ref, vm_ref, qs_ref, ks_ref, vs_ref):
    h = _rms(x_ref[...], gmix_ref[...]).astype(BF16)
    p = _dot(h, win_ref[...])
    o = 0
    cq = p[:, o:o + MLA_Q_RANK]; o += MLA_Q_RANK
    ckv = p[:, o:o + MLA_KV_RANK]; o += MLA_KV_RANK
    kpe = p[:, o:o + LANE]; o += LANE
    qs = p[:, o:o + SWA_HEADS * SWA_DIM]; o += SWA_HEADS * SWA_DIM
    ks = p[:, o:o + LANE]; o += LANE
    vs = p[:, o:o + LANE]

    cos = cos_ref[...]
    sa = sa_ref[...]
    sb = sb_ref[...]

    def rope(z, reps):
        w = z.shape[1]
        c = jnp.concatenate([cos] * reps, axis=1) if reps > 1 else cos
        a = jnp.concatenate([sa] * reps, axis=1) if reps > 1 else sa
        b = jnp.concatenate([sb] * reps, axis=1) if reps > 1 else sb
        return z * c + pltpu.roll(z, w - MLA_ROPE // 2, 1) * a + pltpu.roll(z, MLA_ROPE // 2, 1) * b

    q = _dot(_rms(cq, gcq_ref[...]).astype(BF16), wuq_ref[...])
    qm_ref[...] = (rope(q, MLA_HEADS) * ((MLA_NOPE + MLA_ROPE) ** -0.5)).astype(BF16)
    ckvn = _rms(ckv, gckv_ref[...]).astype(BF16)
    kr = rope(kpe, 1)
    km_ref[...] = (_dot(ckvn, wuk_ref[...]) + jnp.concatenate([kr] * MLA_HEADS, axis=1)).astype(BF16)
    vm_ref[...] = _dot(ckvn, wuv_ref[...]).astype(BF16)
    qs_ref[...] = (qs * (SWA_DIM ** -0.5)).astype(BF16)
    ks_ref[...] = ks.astype(BF16)
    vs_ref[...] = vs.astype(BF16)


def _odd_proj(x, seq, gmix, win, gcq, wuq, gckv, wuk, wuv, cos, sa, sb):
    t, d = x.shape
    tm = 512
    nsq = seq // tm
    row = lambda i: (i, 0)
    pos = lambda i: (i % nsq, 0)
    consts = (gmix, win, gcq, wuq, gckv, wuk, wuv)
    widths = (MLA_HEADS * LANE, MLA_HEADS * LANE, MLA_HEADS * MLA_V, SWA_HEADS * SWA_DIM, LANE, LANE)
    return pl.pallas_call(
        _odd_proj_kernel,
        grid=(t // tm,),
        in_specs=[pl.BlockSpec((tm, d), row)] + [_const_spec(c.shape) for c in consts]
        + [pl.BlockSpec((tm, LANE), pos)] * 3,
        out_specs=[pl.BlockSpec((tm, w), row) for w in widths],
        out_shape=[jax.ShapeDtypeStruct((t, w), BF16) for w in widths],
        compiler_params=pltpu.CompilerParams(
            dimension_semantics=("arbitrary",), vmem_limit_bytes=VMEM_LIMIT),
        name="odd_proj",
    )(x, *consts, cos, sa, sb)


def _attn_kernel(sink_ref, x_ref, qm_ref, km_ref, vm_ref, qs_ref, ks_ref, vs_ref, wout_ref,
                 gffn_ref, wrh_ref, wrl_ref, br_ref, xo_ref, h3_ref, lg_ref, *, tq, seq):
    i = pl.program_id(1)
    qm = qm_ref[...]
    outs = []
    for hd in range(MLA_HEADS):
        s = _dot_nt(qm[:, hd * LANE:(hd + 1) * LANE], km_ref[:, hd * LANE:(hd + 1) * LANE])
        e = jnp.exp(s - jnp.max(s, axis=-1, keepdims=True))
        l = jnp.sum(e, axis=-1, keepdims=True)
        outs.append(_dot(e.astype(BF16), vm_ref[:, hd * MLA_V:(hd + 1) * MLA_V]) / l)

    span = LANE + 2 * WINDOW
    rep = SWA_HEADS // SWA_KV_HEADS
    swa_rows = []
    for blk in range(tq // LANE):
        q0 = i * tq + blk * LANE
        start = pl.multiple_of(jnp.clip(q0 - WINDOW, 0, seq - span), LANE)
        kw = ks_ref[pl.ds(start, span), :]
        vw = vs_ref[pl.ds(start, span), :]
        qpos = q0 + lax.broadcasted_iota(I32, (LANE, span), 0)
        kpos = start + lax.broadcasted_iota(I32, (LANE, span), 1)
        absd = jnp.abs(qpos - kpos).astype(F32)
        in_win = absd <= float(WINDOW)
        heads = []
        for hd in range(SWA_HEADS):
            g = hd // rep
            q = qs_ref[blk * LANE:(blk + 1) * LANE, hd * SWA_DIM:(hd + 1) * SWA_DIM]
            s = _dot_nt(q, kw[:, g * SWA_DIM:(g + 1) * SWA_DIM])
            s = jnp.where(in_win, s - (2.0 ** -(hd + 1)) * absd, NEG_INF)
            sk = sink_ref[hd]
            m = jnp.maximum(jnp.max(s, axis=-1, keepdims=True), sk)
            e = jnp.exp(s - m)
            den = jnp.sum(e, axis=-1, keepdims=True) + jnp.exp(sk - m)
            heads.append(_dot(e.astype(BF16), vw[:, g * SWA_DIM:(g + 1) * SWA_DIM]) / den)
        swa_rows.append(jnp.concatenate(heads, axis=1))
    swa = jnp.concatenate(swa_rows, axis=0) if len(swa_rows) > 1 else swa_rows[0]

    mix = jnp.concatenate(outs + [swa], axis=1).astype(BF16)
    xn = x_ref[...] + _dot(mix, wout_ref[...])
    xo_ref[...] = xn
    _router_epilogue(xn, gffn_ref, wrh_ref, wrl_ref, br_ref, h3_ref, lg_ref)


def _attention(x, seq, sink, qm, km, vm, qs, ks, vs, wout, gffn, wrh, wrl, br):
    t, d = x.shape
    nb = t // seq
    tq = 256
    nq = seq // tq
    row = lambda b, i, *_: (b * nq + i, 0)
    row3 = lambda b, i, *_: (b * nq + i, 0, 0)
    bat = lambda b, i, *_: (b, 0)
    consts = (wout, gffn, wrh, wrl, br)
    grid_spec = pltpu.PrefetchScalarGridSpec(
        num_scalar_prefetch=1,
        grid=(nb, nq),
        in_specs=[pl.BlockSpec((tq, d), row),
                  pl.BlockSpec((tq, qm.shape[1]), row),
                  pl.BlockSpec((seq, km.shape[1]), bat),
                  pl.BlockSpec((seq, vm.shape[1]), bat),
                  pl.BlockSpec((tq, qs.shape[1]), row),
                  pl.BlockSpec((seq, ks.shape[1]), bat),
                  pl.BlockSpec((seq, vs.shape[1]), bat)]
        + [pl.BlockSpec(c.shape, lambda b, i, *_, n=len(c.shape): (0,) * n) for c in consts],
        out_specs=[pl.BlockSpec((tq, d), row),
                   pl.BlockSpec((tq, d // LANE, LANE), row3),
                   pl.BlockSpec((tq, LANE), row)],
    )
    return pl.pallas_call(
        functools.partial(_attn_kernel, tq=tq, seq=seq),
        grid_spec=grid_spec,
        out_shape=[jax.ShapeDtypeStruct((t, d), F32),
                   jax.ShapeDtypeStruct((t, d // LANE, LANE), F32),
                   jax.ShapeDtypeStruct((t, LANE), F32)],
        compiler_params=pltpu.CompilerParams(
            dimension_semantics=("arbitrary", "arbitrary"), vmem_limit_bytes=VMEM_LIMIT),
        name="attention",
    )(sink, x, qm, km, vm, qs, ks, vs, *consts)


def _route_kernel(lg_ref, out_ref, cnt_ref, carry_ref, tot_ref):
    p = pl.program_id(0)
    i = pl.program_id(1)
    rows = lg_ref.shape[0]
    lg = lg_ref[...]
    lane = lax.broadcasted_iota(I32, (rows, LANE), 1)
    ninf = -jnp.inf

    gl = jnp.where(lane < N_GROUPS, lg, ninf)
    gmax = jnp.max(gl, axis=1, keepdims=True)
    gidx = jnp.min(jnp.where(gl == gmax, lane, LANE), axis=1, keepdims=True)
    gp = 1.0 / jnp.sum(jnp.exp(gl - gmax), axis=1, keepdims=True)

    first = N_GROUPS + gidx * N_EXP
    el = jnp.where((lane >= first) & (lane < first + N_EXP), lg, ninf)
    m1 = jnp.max(el, axis=1, keepdims=True)
    i1 = jnp.min(jnp.where(el == m1, lane, LANE), axis=1, keepdims=True)
    el2 = jnp.where(lane == i1, ninf, el)
    m2 = jnp.max(el2, axis=1, keepdims=True)
    i2 = jnp.min(jnp.where(el2 == m2, lane, LANE), axis=1, keepdims=True)
    e2 = jnp.exp(m2 - m1)
    w1 = gp / (1.0 + e2)
    w2 = gp * e2 / (1.0 + e2)
    id1 = i1 - N_GROUPS
    id2 = i2 - N_GROUPS
    onehot = jnp.where((lane == id1) | (lane == id2), 1.0, 0.0)
    colsum = jnp.sum(onehot, axis=0, keepdims=True)

    @pl.when(p == 0)
    def _():
        @pl.when(i == 0)
        def _():
            tot_ref[...] = jnp.zeros_like(tot_ref)
        tot_ref[...] += colsum

    @pl.when(p == 1)
    def _():
        @pl.when(i == 0)
        def _():
            tot = jnp.broadcast_to(tot_ref[...], (8, LANE))
            hi = jnp.floor(tot * (1.0 / 256.0))
            lo = tot - 256.0 * hi
            r = lax.broadcasted_iota(I32, (LANE, LANE), 0)
            c = lax.broadcasted_iota(I32, (LANE, LANE), 1)
            upper = jnp.where(r < c, 1.0, 0.0).astype(BF16)
            off = 256.0 * _dot(hi.astype(BF16), upper) + _dot(lo.astype(BF16), upper)
            carry_ref[...] = off[0:1, :]

        r = lax.broadcasted_iota(I32, (rows, rows), 0)
        c = lax.broadcasted_iota(I32, (rows, rows), 1)
        lower = jnp.where(c < r, 1.0, 0.0).astype(BF16)
        before = _dot(lower, onehot.astype(BF16)) + carry_ref[...]
        pos1 = jnp.sum(jnp.where(lane == id1, before, 0.0), axis=1, keepdims=True)
        pos2 = jnp.sum(jnp.where(lane == id2, before, 0.0), axis=1, keepdims=True)
        carry_ref[...] += colsum
        out = jnp.where(lane == 0, pos1, 0.0)
        out = jnp.where(lane == 1, pos2, out)
        out = jnp.where(lane == 2, w1, out)
        out = jnp.where(lane == 3, w2, out)
        out_ref[...] = out
        cnt_ref[...] = tot_ref[...]


def _route(lg):
    t = lg.shape[0]
    rows = 512
    return pl.pallas_call(
        _route_kernel,
        grid=(2, t // rows),
        in_specs=[pl.BlockSpec((rows, LANE), lambda p, i: (i, 0))],
        out_specs=[pl.BlockSpec((rows, LANE), lambda p, i: (i * p, 0)),
                   pl.BlockSpec((1, LANE), lambda p, i: (0, 0))],
        out_shape=[jax.ShapeDtypeStruct((t, LANE), F32), jax.ShapeDtypeStruct((1, LANE), F32)],
        scratch_shapes=[pltpu.VMEM((1, LANE), F32), pltpu.VMEM((1, LANE), F32)],
        compiler_params=pltpu.CompilerParams(dimension_semantics=("arbitrary", "arbitrary")),
        name="moe_route",
    )(lg)


def _row_copy(src_ref, dst_ref, sem):
    return pltpu.make_async_copy(src_ref, dst_ref, sem)


def _dispatch_kernel(pos_ref, h3_ref, xs_ref, sem):
    i = pl.program_id(0)
    rows = h3_ref.shape[0]

    def issue(r, carry):
        t = i * rows + r
        _row_copy(h3_ref.at[r], xs_ref.at[pos_ref[2 * t]], sem).start()
        _row_copy(h3_ref.at[r], xs_ref.at[pos_ref[2 * t + 1]], sem).start()
        return carry

    lax.fori_loop(0, rows, issue, 0)

    def drain(r, carry):
        _row_copy(h3_ref.at[0], xs_ref.at[0], sem).wait()
        return carry

    lax.fori_loop(0, 2 * rows, drain, 0)


def _dispatch(pos, h3):
    t, s, l = h3.shape
    grid_spec = pltpu.PrefetchScalarGridSpec(
        num_scalar_prefetch=1,
        grid=(t // ROW_TILE,),
        in_specs=[pl.BlockSpec((ROW_TILE, s, l), lambda i, *_: (i, 0, 0))],
        out_specs=pl.BlockSpec(memory_space=pl.ANY),
        scratch_shapes=[pltpu.SemaphoreType.DMA],
    )
    return pl.pallas_call(
        _dispatch_kernel,
        grid_spec=grid_spec,
        out_shape=jax.ShapeDtypeStruct((2 * t, s, l), F32),
        compiler_params=pltpu.CompilerParams(dimension_semantics=("arbitrary",)),
        name="moe_dispatch",
    )(pos, h3)


def _expert_kernel(tile_ref, exp_ref, lo_ref, hi_ref, xs_ref, wg_ref, wu_ref, wd_ref, ys_ref,
                   wg_s, wu_s, wd_s):
    i = pl.program_id(0)
    prev = jnp.maximum(i - 1, 0)
    new_tile = (i == 0) | (tile_ref[i] != tile_ref[prev])
    new_exp = (i == 0) | (exp_ref[i] != exp_ref[prev])
    lo = lo_ref[i]
    hi = hi_ref[i]
    rows = xs_ref.shape[0]
    nslab = xs_ref.shape[1]

    @pl.when(new_exp)
    def _():
        wg_s[...] = wg_ref[0].astype(BF16)
        wu_s[...] = wu_ref[0].astype(BF16)
        wd_s[...] = wd_ref[0].astype(BF16)

    @pl.when(new_tile)
    def _():
        ys_ref[...] = jnp.zeros_like(ys_ref)

    @pl.when(hi > lo)
    def _():
        x = jnp.concatenate([xs_ref[:, s, :] for s in range(nslab)], axis=1).astype(BF16)
        g = _dot(x, wg_s[...])
        u = _dot(x, wu_s[...])
        hid = (g * _sigmoid(g) * u).astype(BF16)
        y = _dot(hid, wd_s[...])
        r = tile_ref[i] * rows + lax.broadcasted_iota(I32, (rows, LANE), 0)
        keep = (r >= lo) & (r < hi)
        for s in range(nslab):
            ys_ref[:, s, :] += jnp.where(keep, y[:, s * LANE:(s + 1) * LANE], 0.0)


def _experts(work, xs, wg, wu, wd):
    n, s, l = xs.shape
    d, f = wg.shape[1], wg.shape[2]
    nwork = work[0].shape[0]
    grid_spec = pltpu.PrefetchScalarGridSpec(
        num_scalar_prefetch=4,
        grid=(nwork,),
        in_specs=[pl.BlockSpec((MOE_TILE, s, l), lambda i, tl, ex, lo, hi: (tl[i], 0, 0)),
                  pl.BlockSpec((1, d, f), lambda i, tl, ex, lo, hi: (ex[i], 0, 0)),
                  pl.BlockSpec((1, d, f), lambda i, tl, ex, lo, hi: (ex[i], 0, 0)),
                  pl.BlockSpec((1, f, d), lambda i, tl, ex, lo, hi: (ex[i], 0, 0))],
        out_specs=pl.BlockSpec((MOE_TILE, s, l), lambda i, tl, ex, lo, hi: (tl[i], 0, 0)),
        scratch_shapes=[pltpu.VMEM((d, f), BF16), pltpu.VMEM((d, f), BF16), pltpu.VMEM((f, d), BF16)],
    )
    return pl.pallas_call(
        _expert_kernel,
        grid_spec=grid_spec,
        out_shape=jax.ShapeDtypeStruct((n, s, l), F32),
        compiler_params=pltpu.CompilerParams(
            dimension_semantics=("arbitrary",), vmem_limit_bytes=VMEM_LIMIT),
        name="moe_experts",
    )(*work, xs, wg, wu, wd)


def _combine_kernel(pos_ref, x_ref, rw_ref, gfin_ref, ys_ref, xo_ref, buf, sems, *, final):
    i = pl.program_id(0)
    n = pl.num_programs(0)
    rows = x_ref.shape[0]
    nslab = buf.shape[2]
    slot = i % 2

    def issue(step, slt):
        def body(r, carry):
            t = step * rows + r
            _row_copy(ys_ref.at[pos_ref[2 * t]], buf.at[slt, r], sems.at[slt]).start()
            _row_copy(ys_ref.at[pos_ref[2 * t + 1]], buf.at[slt, rows + r], sems.at[slt]).start()
            return carry
        lax.fori_loop(0, rows, body, 0)

    @pl.when(i == 0)
    def _():
        issue(0, 0)

    @pl.when(i + 1 < n)
    def _():
        issue(i + 1, 1 - slot)

    def drain(r, carry):
        _row_copy(ys_ref.at[0], buf.at[slot, 0], sems.at[slot]).wait()
        return carry

    lax.fori_loop(0, 2 * rows, drain, 0)

    y0 = jnp.concatenate([buf[slot, 0:rows, s, :] for s in range(nslab)], axis=1)
    y1 = jnp.concatenate([buf[slot, rows:2 * rows, s, :] for s in range(nslab)], axis=1)
    rw = rw_ref[...]
    xn = x_ref[...] + rw[:, 2:3] * y0 + rw[:, 3:4] * y1
    if final:
        xn = _rms(xn, gfin_ref[...])
    xo_ref[...] = xn


def _combine(pos, x, rw, gfin, ys, final):
    t, d = x.shape
    s, l = ys.shape[1], ys.shape[2]
    grid_spec = pltpu.PrefetchScalarGridSpec(
        num_scalar_prefetch=1,
        grid=(t // ROW_TILE,),
        in_specs=[pl.BlockSpec((ROW_TILE, d), lambda i, *_: (i, 0)),
                  pl.BlockSpec((ROW_TILE, LANE), lambda i, *_: (i, 0)),
                  pl.BlockSpec((1, d), lambda i, *_: (0, 0)),
                  pl.BlockSpec(memory_space=pl.ANY)],
        out_specs=pl.BlockSpec((ROW_TILE, d), lambda i, *_: (i, 0)),
        scratch_shapes=[pltpu.VMEM((2, 2 * ROW_TILE, s, l), F32), pltpu.SemaphoreType.DMA((2,))],
    )
    return pl.pallas_call(
        functools.partial(_combine_kernel, final=final),
        grid_spec=grid_spec,
        out_shape=jax.ShapeDtypeStruct((t, d), F32),
        compiler_params=pltpu.CompilerParams(
            dimension_semantics=("arbitrary",), vmem_limit_bytes=VMEM_LIMIT),
        name="moe_combine",
    )(pos, x, rw, gfin, ys)


def _worklist(cnt, n_rows):
    cnt = cnt[:N_EXPERTS].astype(I32)
    ends = jnp.cumsum(cnt)
    starts = ends - cnt
    n_tiles = n_rows // MOE_TILE
    tile_lo = jnp.arange(n_tiles, dtype=I32) * MOE_TILE
    exp_lo = starts[1:]
    rank_t = jnp.arange(n_tiles, dtype=I32) + jnp.sum(exp_lo[None, :] < tile_lo[:, None], axis=1)
    rank_e = jnp.arange(N_EXPERTS - 1, dtype=I32) + jnp.sum(tile_lo[None, :] <= exp_lo[:, None], axis=1)
    vals = jnp.concatenate([tile_lo, exp_lo])
    ranks = jnp.concatenate([rank_t, rank_e])
    slot = jnp.arange(vals.shape[0], dtype=I32)
    lo = jnp.sum(jnp.where(ranks[None, :] == slot[:, None], vals[None, :], 0), axis=1)
    hi = jnp.concatenate([lo[1:], jnp.full((1,), n_rows, I32)])
    tile = jnp.minimum(lo // MOE_TILE, n_tiles - 1)
    expert = jnp.minimum(jnp.sum(ends[None, :] <= lo[:, None], axis=1), N_EXPERTS - 1).astype(I32)
    return tile, expert, lo, hi


def _moe(x, h3, lg, wg, wu, wd, gfin, final):
    t = x.shape[0]
    rw, cnt = _route(lg)
    pos = rw[:, 0:2].astype(I32).reshape(2 * t)
    xs = _dispatch(pos, h3)
    work = _worklist(cnt[0], 2 * t)
    ys = _experts(work, xs, wg, wu, wd)
    return _combine(pos, x, rw, gfin, ys, final)


def _router_weights(w_group, b_group, w_router, b_router):
    d = w_group.shape[0]
    w = jnp.concatenate([w_group] + [w_router[g] for g in range(N_GROUPS)], axis=1)
    w = jnp.pad(w, ((0, 0), (0, LANE - w.shape[1])))
    b = jnp.concatenate([b_group, b_router.reshape(-1)])
    b = jnp.pad(b, (0, LANE - b.shape[0])).reshape(1, LANE)
    hi = w.astype(BF16)
    lo = (w - hi.astype(F32)).astype(BF16)
    return hi, lo, b


def _rope_tables(seq):
    half = MLA_ROPE // 2
    pos = jnp.arange(seq, dtype=F32)
    inv = 1.0 / (ROPE_THETA ** (jnp.arange(0, MLA_ROPE, 2, dtype=F32) / MLA_ROPE))
    ang = pos[:, None] * inv[None, :]
    cos, sin = jnp.cos(ang), jnp.sin(ang)
    z = lambda n: jnp.zeros((seq, n), F32)
    tail = LANE - MLA_NOPE - MLA_ROPE
    ctab = jnp.concatenate([jnp.ones((seq, MLA_NOPE), F32), cos, cos, z(tail)], axis=1)
    atab = jnp.concatenate([z(MLA_NOPE), -sin, z(half), z(tail)], axis=1)
    btab = jnp.concatenate([z(MLA_NOPE), z(half), sin, z(tail)], axis=1)
    return ctab, atab, btab


def _odd_weights(w_in, w_uq, w_ukv):
    d = w_in.shape[0]
    sizes = (MLA_Q_RANK, MLA_KV_RANK, MLA_ROPE, SWA_HEADS * SWA_DIM, SWA_KV_HEADS * SWA_DIM,
             SWA_KV_HEADS * SWA_DIM)
    offs = [0]
    for s in sizes:
        offs.append(offs[-1] + s)
    cq, ckv, kpe, qs, ks, vs = [w_in[:, offs[k]:offs[k + 1]] for k in range(6)]
    kpe = jnp.pad(kpe, ((0, 0), (MLA_NOPE, LANE - MLA_NOPE - MLA_ROPE)))
    win = jnp.concatenate([cq, ckv, kpe, qs, ks, vs], axis=1).astype(BF16)
    hq = MLA_NOPE + MLA_ROPE
    wuq = jnp.pad(w_uq.reshape(MLA_Q_RANK, MLA_HEADS, hq), ((0, 0), (0, 0), (0, LANE - hq)))
    wuq = wuq.reshape(MLA_Q_RANK, MLA_HEADS * LANE).astype(BF16)
    wkv = w_ukv.reshape(MLA_KV_RANK, MLA_HEADS, MLA_NOPE + MLA_V)
    wuk = jnp.pad(wkv[:, :, :MLA_NOPE], ((0, 0), (0, 0), (0, LANE - MLA_NOPE)))
    wuk = wuk.reshape(MLA_KV_RANK, MLA_HEADS * LANE).astype(BF16)
    wuv = wkv[:, :, MLA_NOPE:].reshape(MLA_KV_RANK, MLA_HEADS * MLA_V).astype(BF16)
    return win, wuq, wuk, wuv


def kernel(x, norm_mix, norm_ffn, norm_final, ev_w_in, ev_conv_w, ev_conv_b, ev_cnorm_g, ev_cnorm_b,
           ev_vnorm_g, ev_vnorm_b, ev_w_sp, ev_b_sp, ev_w_out, od_w_in, od_g_cq, od_w_uq, od_g_ckv,
           od_w_ukv, od_sink, od_w_out, moe_w_group, moe_b_group, moe_w_router, moe_b_router,
           moe_w_gate, moe_w_up, moe_w_down):
    b, seq, d = x.shape
    t = b * seq
    depth = norm_mix.shape[0]
    row = lambda v: v.reshape(1, -1)
    xt = x.reshape(t, d)
    gfin = row(norm_final)
    for layer in range(depth):
        k = layer // 2
        wrh, wrl, br = _router_weights(moe_w_group[layer], moe_b_group[layer], moe_w_router[layer],
                                       moe_b_router[layer])
        gffn = row(norm_ffn[layer])
        if layer % 2 == 0:
            bsp = jnp.broadcast_to(ev_b_sp[k][:, :, None], ev_w_sp[k].shape)
            xt, h3, lg = _even_mixer(
                xt, seq, row(norm_mix[layer]), ev_w_in[k].astype(BF16), ev_conv_w[k], row(ev_conv_b[k]),
                row(ev_cnorm_g[k]), row(ev_cnorm_b[k]), row(ev_vnorm_g[k]), row(ev_vnorm_b[k]),
                ev_w_sp[k].astype(BF16), bsp, ev_w_out[k].astype(BF16), gffn, wrh, wrl, br)
        else:
            win, wuq, wuk, wuv = _odd_weights(od_w_in[k], od_w_uq[k], od_w_ukv[k])
            ctab, atab, btab = _rope_tables(seq)
            qm, km, vm, qs, ks, vs = _odd_proj(
                xt, seq, row(norm_mix[layer]), win, row(od_g_cq[k]), wuq, row(od_g_ckv[k]), wuk, wuv,
                ctab, atab, btab)
            xt, h3, lg = _attention(xt, seq, od_sink[k], qm, km, vm, qs, ks, vs,
                                    od_w_out[k].astype(BF16), gffn, wrh, wrl, br)
        xt = _moe(xt, h3, lg, moe_w_gate[layer].reshape(N_EXPERTS, d, -1),
                  moe_w_up[layer].reshape(N_EXPERTS, d, -1),
                  moe_w_down[layer].reshape(N_EXPERTS, -1, d), gfin, layer == depth - 1)
    return xt.reshape(b, seq, d)
```

```python
import functools

import jax
import jax.numpy as jnp
from jax import lax
from jax.experimental import pallas as pl
from jax.experimental.pallas import tpu as pltpu

F32 = jnp.float32
BF16 = jnp.bfloat16
I32 = jnp.int32

EPS = 1e-6
NEG_INF = -1e30
LANE = 128
VMEM_LIMIT = 56 * 1024 * 1024

CONV_DIM = 512
CONV_GROUP = 128
CONV_WIDTH = 31
SGU_DIM = 512
SGU_HEAD = 128
CHUNK = 128
MLA_HEADS = 8
MLA_Q_RANK = 256
MLA_KV_RANK = 128
MLA_NOPE = 64
MLA_ROPE = 32
MLA_V = 64
ROPE_THETA = 10000.0
SWA_HEADS = 8
SWA_KV_HEADS = 2
SWA_DIM = 64
WINDOW = 128
N_GROUPS = 4
N_EXP = 8
N_EXPERTS = N_GROUPS * N_EXP

HALO = 16
CONV_ROWS = 64
MOE_TILE = 256
ROW_TILE = 256
DISPATCH_TILE = 1024
ISSUE_UNROLL = 8


def _dot(a, b):
    return jnp.dot(a, b, preferred_element_type=F32)


def _dot_nt(a, b):
    return lax.dot_general(a, b, (((1,), (1,)), ((), ())), preferred_element_type=F32)


def _rms(x, g):
    return x * lax.rsqrt(jnp.mean(x * x, axis=-1, keepdims=True) + EPS) * g


def _ln(x, g, b):
    mu = jnp.mean(x, axis=-1, keepdims=True)
    d = x - mu
    var = jnp.mean(d * d, axis=-1, keepdims=True)
    return d * lax.rsqrt(var + EPS) * g + b


def _gelu(x):
    return 0.5 * x * (1.0 + jnp.tanh(0.7978845608028654 * (x + 0.044715 * (x * x * x))))


def _sigmoid(x):
    return 1.0 / (1.0 + jnp.exp(-x))


def _router_epilogue(xn, gffn_ref, wrh_ref, wrl_ref, br_ref, h3_ref, lg_ref):
    h = _rms(xn, gffn_ref[...])
    h3_ref[...] = h
    hi = h.astype(BF16)
    lo = (h - hi.astype(F32)).astype(BF16)
    wh = wrh_ref[...]
    lg_ref[...] = _dot(hi, wh) + _dot(lo, wh) + _dot(hi, wrl_ref[...]) + br_ref[...]


def _even_kernel(xc_ref, xp_ref, gmix_ref, win_ref, cw_ref, cb_ref, cng_ref, cnb_ref, vng_ref,
                 vnb_ref, wsp_ref, bsp_ref, wout_ref, gffn_ref, wrh_ref, wrl_ref, br_ref,
                 xo_ref, h3_ref, lg_ref, a_s, mix_s, *, ns, sub, seq):
    j = pl.program_id(1)

    @pl.when(j == 0)
    def _():
        a_s[0:HALO, :] = jnp.zeros((HALO, CONV_DIM), F32)

    @pl.when(j < ns)
    def _proj():
        r0 = pl.multiple_of(j * sub, sub)
        h = _rms(xc_ref[...], gmix_ref[...]).astype(BF16)
        pa = _dot(h, win_ref[:, 0:2 * CONV_DIM])
        a_s[pl.ds(HALO + r0, sub), :] = pa[:, :CONV_DIM] * _sigmoid(pa[:, CONV_DIM:])
        u = _gelu(_dot(h, win_ref[:, 2 * CONV_DIM:2 * CONV_DIM + SGU_DIM]))
        v = _gelu(_dot(h, win_ref[:, 2 * CONV_DIM + SGU_DIM:]))
        for hd in range(SGU_DIM // SGU_HEAD):
            cs = slice(hd * SGU_HEAD, (hd + 1) * SGU_HEAD)
            vn = _ln(v[:, cs], vng_ref[:, cs], vnb_ref[:, cs]).astype(BF16)
            for c in range(sub // CHUNK):
                rs = slice(c * CHUNK, (c + 1) * CHUNK)
                sv = _dot(wsp_ref[hd], vn[rs, :]) + bsp_ref[hd]
                mix_s[pl.ds(r0 + c * CHUNK, CHUNK),
                      CONV_DIM + hd * SGU_HEAD:CONV_DIM + (hd + 1) * SGU_HEAD] = (u[rs, cs] * sv).astype(BF16)

    @pl.when(j == ns)
    def _():
        a_s[HALO + seq:2 * HALO + seq, :] = jnp.zeros((HALO, CONV_DIM), F32)

    @pl.when(j >= 1)
    def _out():
        r0 = (j - 1) * sub
        win_rows = CONV_ROWS + 2 * HALO

        def chunk(i, carry):
            base = pl.multiple_of(r0 + i * CONV_ROWS, CONV_ROWS)
            win = a_s[pl.ds(base, win_rows), :]
            acc = jnp.zeros((CONV_ROWS, CONV_DIM), F32)
            for s in range(8):
                sh = win if s == 0 else pltpu.roll(win, win_rows - s, 0)
                for m in range(4):
                    k = 8 * m + s - 1
                    if 0 <= k < CONV_WIDTH:
                        acc = acc + sh[8 * m:8 * m + CONV_ROWS, :] * cw_ref[k:k + 1, :]
            acc = acc + cb_ref[...]
            for g in range(CONV_DIM // CONV_GROUP):
                cs = slice(g * CONV_GROUP, (g + 1) * CONV_GROUP)
                y = _ln(acc[:, cs], cng_ref[:, cs], cnb_ref[:, cs])
                mix_s[pl.ds(base, CONV_ROWS), cs] = (y * _sigmoid(y)).astype(BF16)
            return carry

        lax.fori_loop(0, sub // CONV_ROWS, chunk, 0)
        mix = mix_s[pl.ds(pl.multiple_of(r0, sub), sub), :]
        xn = xp_ref[...] + _dot(mix, wout_ref[...])
        xo_ref[...] = xn
        _router_epilogue(xn, gffn_ref, wrh_ref, wrl_ref, br_ref, h3_ref, lg_ref)


def _const_spec(shape):
    nd = len(shape)
    return pl.BlockSpec(shape, lambda *_: (0,) * nd)


def _even_mixer(x, seq, gmix, win, cw, cb, cng, cnb, vng, vnb, wsp, bsp, wout, gffn, wrh, wrl, br):
    t, d = x.shape
    nb = t // seq
    sub = 512
    ns = seq // sub
    row = lambda b, j: (b * ns + jnp.minimum(j, ns - 1), 0)
    prev = lambda b, j: (b * ns + jnp.maximum(j - 1, 0), 0)
    consts = (gmix, win, cw, cb, cng, cnb, vng, vnb, wsp, bsp, wout, gffn, wrh, wrl, br)
    return pl.pallas_call(
        functools.partial(_even_kernel, ns=ns, sub=sub, seq=seq),
        grid=(nb, ns + 1),
        in_specs=[pl.BlockSpec((sub, d), row), pl.BlockSpec((sub, d), prev)]
        + [_const_spec(c.shape) for c in consts],
        out_specs=[pl.BlockSpec((sub, d), prev),
                   pl.BlockSpec((sub, d), prev),
                   pl.BlockSpec((sub, LANE), prev)],
        out_shape=[jax.ShapeDtypeStruct((t, d), F32),
                   jax.ShapeDtypeStruct((t, d), F32),
                   jax.ShapeDtypeStruct((t, LANE), F32)],
        scratch_shapes=[pltpu.VMEM((seq + 2 * HALO, CONV_DIM), F32),
                        pltpu.VMEM((seq, CONV_DIM + SGU_DIM), BF16)],
        compiler_params=pltpu.CompilerParams(
            dimension_semantics=("arbitrary", "arbitrary"), vmem_limit_bytes=VMEM_LIMIT),
        name="even_mixer",
    )(x, x, *consts)


def _odd_proj_kernel(x_ref, gmix_ref, win_ref, gcq_ref, wuq_ref, gckv_ref, wuk_ref, wuv_ref,
                     cos_ref, sa_ref, sb_ref, qm_ref, km_ref, vm_ref, qs_ref, ks_ref, vs_ref):
    h = _rms(x_ref[...], gmix_ref[...]).astype(BF16)
    p = _dot(h, win_ref[...])
    o = 0
    cq = p[:, o:o + MLA_Q_RANK]; o += MLA_Q_RANK
    ckv = p[:, o:o + MLA_KV_RANK]; o += MLA_KV_RANK
    kpe = p[:, o:o + LANE]; o += LANE
    qs = p[:, o:o + SWA_HEADS * SWA_DIM]; o += SWA_HEADS * SWA_DIM
    ks = p[:, o:o + LANE]; o += LANE
    vs = p[:, o:o + LANE]

    cos = cos_ref[...]
    sa = sa_ref[...]
    sb = sb_ref[...]

    def rope(z, reps):
        w = z.shape[1]
        c = jnp.concatenate([cos] * reps, axis=1) if reps > 1 else cos
        a = jnp.concatenate([sa] * reps, axis=1) if reps > 1 else sa
        b = jnp.concatenate([sb] * reps, axis=1) if reps > 1 else sb
        return z * c + pltpu.roll(z, w - MLA_ROPE // 2, 1) * a + pltpu.roll(z, MLA_ROPE // 2, 1) * b

    q = _dot(_rms(cq, gcq_ref[...]).astype(BF16), wuq_ref[...])
    qm_ref[...] = (rope(q, MLA_HEADS) * ((MLA_NOPE + MLA_ROPE) ** -0.5)).astype(BF16)
    ckvn = _rms(ckv, gckv_ref[...]).astype(BF16)
    kr = rope(kpe, 1)
    km_ref[...] = (_dot(ckvn, wuk_ref[...]) + jnp.concatenate([kr] * MLA_HEADS, axis=1)).astype(BF16)
    vm_ref[...] = _dot(ckvn, wuv_ref[...]).astype(BF16)
    qs_ref[...] = (qs * (SWA_DIM ** -0.5)).astype(BF16)
    ks_ref[...] = ks.astype(BF16)
    vs_ref[...] = vs.astype(BF16)


def _odd_proj(x, seq, gmix, win, gcq, wuq, gckv, wuk, wuv, cos, sa, sb):
    t, d = x.shape
    tm = 512
    nsq = seq // tm
    row = lambda i: (i, 0)
    pos = lambda i: (i % nsq, 0)
    consts = (gmix, win, gcq, wuq, gckv, wuk, wuv)
    widths = (MLA_HEADS * LANE, MLA_HEADS * LANE, MLA_HEADS * MLA_V, SWA_HEADS * SWA_DIM, LANE, LANE)
    return pl.pallas_call(
        _odd_proj_kernel,
        grid=(t // tm,),
        in_specs=[pl.BlockSpec((tm, d), row)] + [_const_spec(c.shape) for c in consts]
        + [pl.BlockSpec((tm, LANE), pos)] * 3,
        out_specs=[pl.BlockSpec((tm, w), row) for w in widths],
        out_shape=[jax.ShapeDtypeStruct((t, w), BF16) for w in widths],
        compiler_params=pltpu.CompilerParams(
            dimension_semantics=("arbitrary",), vmem_limit_bytes=VMEM_LIMIT),
        name="odd_proj",
    )(x, *consts, cos, sa, sb)


def _attn_kernel(sink_ref, x_ref, qm_ref, km_ref, vm_ref, qs_ref, ks_ref, vs_ref, wout_ref,
                 gffn_ref, wrh_ref, wrl_ref, br_ref, xo_ref, h3_ref, lg_ref, *, tq, seq):
    i = pl.program_id(1)
    qm = qm_ref[...]
    outs = []
    for hd in range(MLA_HEADS):
        s = _dot_nt(qm[:, hd * LANE:(hd + 1) * LANE], km_ref[:, hd * LANE:(hd + 1) * LANE])
        e = jnp.exp(s - jnp.max(s, axis=-1, keepdims=True))
        l = jnp.sum(e, axis=-1, keepdims=True)
        outs.append(_dot(e.astype(BF16), vm_ref[:, hd * MLA_V:(hd + 1) * MLA_V]) / l)

    span = LANE + 2 * WINDOW
    rep = SWA_HEADS // SWA_KV_HEADS
    swa_rows = []
    for blk in range(tq // LANE):
        q0 = i * tq + blk * LANE
        start = pl.multiple_of(jnp.clip(q0 - WINDOW, 0, seq - span), LANE)
        kw = ks_ref[pl.ds(start, span), :]
        vw = vs_ref[pl.ds(start, span), :]
        qpos = q0 + lax.broadcasted_iota(I32, (LANE, span), 0)
        kpos = start + lax.broadcasted_iota(I32, (LANE, span), 1)
        absd = jnp.abs(qpos - kpos).astype(F32)
        in_win = absd <= float(WINDOW)
        heads = []
        for hd in range(SWA_HEADS):
            g = hd // rep
            q = qs_ref[blk * LANE:(blk + 1) * LANE, hd * SWA_DIM:(hd + 1) * SWA_DIM]
            s = _dot_nt(q, kw[:, g * SWA_DIM:(g + 1) * SWA_DIM])
            s = jnp.where(in_win, s - (2.0 ** -(hd + 1)) * absd, NEG_INF)
            sk = sink_ref[hd]
            m = jnp.maximum(jnp.max(s, axis=-1, keepdims=True), sk)
            e = jnp.exp(s - m)
            den = jnp.sum(e, axis=-1, keepdims=True) + jnp.exp(sk - m)
            heads.append(_dot(e.astype(BF16), vw[:, g * SWA_DIM:(g + 1) * SWA_DIM]) / den)
        swa_rows.append(jnp.concatenate(heads, axis=1))
    swa = jnp.concatenate(swa_rows, axis=0) if len(swa_rows) > 1 else swa_rows[0]

    mix = jnp.concatenate(outs + [swa], axis=1).astype(BF16)
    xn = x_ref[...] + _dot(mix, wout_ref[...])
    xo_ref[...] = xn
    _router_epilogue(xn, gffn_ref, wrh_ref, wrl_ref, br_ref, h3_ref, lg_ref)


def _attention(x, seq, sink, qm, km, vm, qs, ks, vs, wout, gffn, wrh, wrl, br):
    t, d = x.shape
    nb = t // seq
    tq = 256
    nq = seq // tq
    row = lambda b, i, *_: (b * nq + i, 0)
    bat = lambda b, i, *_: (b, 0)
    consts = (wout, gffn, wrh, wrl, br)
    grid_spec = pltpu.PrefetchScalarGridSpec(
        num_scalar_prefetch=1,
        grid=(nb, nq),
        in_specs=[pl.BlockSpec((tq, d), row),
                  pl.BlockSpec((tq, qm.shape[1]), row),
                  pl.BlockSpec((seq, km.shape[1]), bat),
                  pl.BlockSpec((seq, vm.shape[1]), bat),
                  pl.BlockSpec((tq, qs.shape[1]), row),
                  pl.BlockSpec((seq, ks.shape[1]), bat),
                  pl.BlockSpec((seq, vs.shape[1]), bat)]
        + [pl.BlockSpec(c.shape, lambda b, i, *_, n=len(c.shape): (0,) * n) for c in consts],
        out_specs=[pl.BlockSpec((tq, d), row),
                   pl.BlockSpec((tq, d), row),
                   pl.BlockSpec((tq, LANE), row)],
    )
    return pl.pallas_call(
        functools.partial(_attn_kernel, tq=tq, seq=seq),
        grid_spec=grid_spec,
        out_shape=[jax.ShapeDtypeStruct((t, d), F32),
                   jax.ShapeDtypeStruct((t, d), F32),
                   jax.ShapeDtypeStruct((t, LANE), F32)],
        compiler_params=pltpu.CompilerParams(
            dimension_semantics=("arbitrary", "arbitrary"), vmem_limit_bytes=VMEM_LIMIT),
        name="attention",
    )(sink, x, qm, km, vm, qs, ks, vs, *consts)


def _route_kernel(lg_ref, out_ref, cnt_ref, carry_ref, tot_ref):
    p = pl.program_id(0)
    i = pl.program_id(1)
    rows = lg_ref.shape[0]
    lg = lg_ref[...]
    lane = lax.broadcasted_iota(I32, (rows, LANE), 1)
    ninf = -jnp.inf

    gl = jnp.where(lane < N_GROUPS, lg, ninf)
    gmax = jnp.max(gl, axis=1, keepdims=True)
    gidx = jnp.min(jnp.where(gl == gmax, lane, LANE), axis=1, keepdims=True)
    gp = 1.0 / jnp.sum(jnp.exp(gl - gmax), axis=1, keepdims=True)

    first = N_GROUPS + gidx * N_EXP
    el = jnp.where((lane >= first) & (lane < first + N_EXP), lg, ninf)
    m1 = jnp.max(el, axis=1, keepdims=True)
    i1 = jnp.min(jnp.where(el == m1, lane, LANE), axis=1, keepdims=True)
    el2 = jnp.where(lane == i1, ninf, el)
    m2 = jnp.max(el2, axis=1, keepdims=True)
    i2 = jnp.min(jnp.where(el2 == m2, lane, LANE), axis=1, keepdims=True)
    e2 = jnp.exp(m2 - m1)
    w1 = gp / (1.0 + e2)
    w2 = gp * e2 / (1.0 + e2)
    id1 = i1 - N_GROUPS
    id2 = i2 - N_GROUPS
    onehot = jnp.where((lane == id1) | (lane == id2), 1.0, 0.0)
    colsum = jnp.sum(onehot, axis=0, keepdims=True)

    @pl.when(p == 0)
    def _():
        @pl.when(i == 0)
        def _():
            tot_ref[...] = jnp.zeros_like(tot_ref)
        tot_ref[...] += colsum

    @pl.when(p == 1)
    def _():
        @pl.when(i == 0)
        def _():
            tot = jnp.broadcast_to(tot_ref[...], (8, LANE))
            hi = jnp.floor(tot * (1.0 / 256.0))
            lo = tot - 256.0 * hi
            r = lax.broadcasted_iota(I32, (LANE, LANE), 0)
            c = lax.broadcasted_iota(I32, (LANE, LANE), 1)
            upper = jnp.where(r < c, 1.0, 0.0).astype(BF16)
            off = 256.0 * _dot(hi.astype(BF16), upper) + _dot(lo.astype(BF16), upper)
            carry_ref[...] = off[0:1, :]

        r = lax.broadcasted_iota(I32, (rows, rows), 0)
        c = lax.broadcasted_iota(I32, (rows, rows), 1)
        lower = jnp.where(c < r, 1.0, 0.0).astype(BF16)
        before = _dot(lower, onehot.astype(BF16)) + carry_ref[...]
        pos1 = jnp.sum(jnp.where(lane == id1, before, 0.0), axis=1, keepdims=True)
        pos2 = jnp.sum(jnp.where(lane == id2, before, 0.0), axis=1, keepdims=True)
        carry_ref[...] += colsum
        out = jnp.where(lane == 0, pos1, 0.0)
        out = jnp.where(lane == 1, pos2, out)
        out = jnp.where(lane == 2, w1, out)
        out = jnp.where(lane == 3, w2, out)
        out_ref[...] = out
        cnt_ref[...] = tot_ref[...]


def _route(lg):
    t = lg.shape[0]
    rows = 512
    return pl.pallas_call(
        _route_kernel,
        grid=(2, t // rows),
        in_specs=[pl.BlockSpec((rows, LANE), lambda p, i: (i, 0))],
        out_specs=[pl.BlockSpec((rows, LANE), lambda p, i: (i * p, 0)),
                   pl.BlockSpec((1, LANE), lambda p, i: (0, 0))],
        out_shape=[jax.ShapeDtypeStruct((t, LANE), F32), jax.ShapeDtypeStruct((1, LANE), F32)],
        scratch_shapes=[pltpu.VMEM((1, LANE), F32), pltpu.VMEM((1, LANE), F32)],
        compiler_params=pltpu.CompilerParams(dimension_semantics=("arbitrary", "arbitrary")),
        name="moe_route",
    )(lg)


def _row(ref, r):
    return ref.at[pl.ds(r, 1), :]


def _dispatch_kernel(pos_ref, h_ref, xs_ref, sem):
    i = pl.program_id(0)
    rows = h_ref.shape[0]

    def issue(blk, carry):
        for u in range(ISSUE_UNROLL):
            r = blk * ISSUE_UNROLL + u
            t = i * rows + r
            pltpu.make_async_copy(_row(h_ref, r), _row(xs_ref, pos_ref[2 * t]), sem).start(priority=0)
            pltpu.make_async_copy(_row(h_ref, r), _row(xs_ref, pos_ref[2 * t + 1]), sem).start(priority=1)
        return carry

    lax.fori_loop(0, rows // ISSUE_UNROLL, issue, 0)
    for _ in range(2):
        pltpu.make_async_copy(h_ref, xs_ref.at[pl.ds(0, rows), :], sem).wait()


def _dispatch(pos, h):
    t, d = h.shape
    grid_spec = pltpu.PrefetchScalarGridSpec(
        num_scalar_prefetch=1,
        grid=(t // DISPATCH_TILE,),
        in_specs=[pl.BlockSpec((DISPATCH_TILE, d), lambda i, *_: (i, 0))],
        out_specs=pl.BlockSpec(memory_space=pl.ANY),
        scratch_shapes=[pltpu.SemaphoreType.DMA],
    )
    return pl.pallas_call(
        _dispatch_kernel,
        grid_spec=grid_spec,
        out_shape=jax.ShapeDtypeStruct((2 * t, d), F32),
        compiler_params=pltpu.CompilerParams(dimension_semantics=("arbitrary",)),
        name="moe_dispatch",
    )(pos, h)


def _expert_kernel(tile_ref, exp_ref, lo_ref, hi_ref, xs_ref, wg_ref, wu_ref, wd_ref, ys_ref,
                   wg_s, wu_s, wd_s):
    i = pl.program_id(0)
    prev = jnp.maximum(i - 1, 0)
    new_tile = (i == 0) | (tile_ref[i] != tile_ref[prev])
    new_exp = (i == 0) | (exp_ref[i] != exp_ref[prev])
    lo = lo_ref[i]
    hi = hi_ref[i]
    rows = xs_ref.shape[0]

    @pl.when(new_exp)
    def _():
        wg_s[...] = wg_ref[0].astype(BF16)
        wu_s[...] = wu_ref[0].astype(BF16)
        wd_s[...] = wd_ref[0].astype(BF16)

    @pl.when(new_tile)
    def _():
        ys_ref[...] = jnp.zeros_like(ys_ref)

    @pl.when(hi > lo)
    def _():
        x = xs_ref[...].astype(BF16)
        g = _dot(x, wg_s[...])
        u = _dot(x, wu_s[...])
        hid = (g * _sigmoid(g) * u).astype(BF16)
        y = _dot(hid, wd_s[...])
        r = tile_ref[i] * rows + lax.broadcasted_iota(I32, (rows, 1), 0)
        ys_ref[...] += jnp.where((r >= lo) & (r < hi), y, 0.0)


def _experts(work, xs, wg, wu, wd):
    n, d = xs.shape
    f = wg.shape[2]
    nwork = work[0].shape[0]
    grid_spec = pltpu.PrefetchScalarGridSpec(
        num_scalar_prefetch=4,
        grid=(nwork,),
        in_specs=[pl.BlockSpec((MOE_TILE, d), lambda i, tl, ex, lo, hi: (tl[i], 0)),
                  pl.BlockSpec((1, d, f), lambda i, tl, ex, lo, hi: (ex[i], 0, 0)),
                  pl.BlockSpec((1, d, f), lambda i, tl, ex, lo, hi: (ex[i], 0, 0)),
                  pl.BlockSpec((1, f, d), lambda i, tl, ex, lo, hi: (ex[i], 0, 0))],
        out_specs=pl.BlockSpec((MOE_TILE, d), lambda i, tl, ex, lo, hi: (tl[i], 0)),
        scratch_shapes=[pltpu.VMEM((d, f), BF16), pltpu.VMEM((d, f), BF16), pltpu.VMEM((f, d), BF16)],
    )
    return pl.pallas_call(
        _expert_kernel,
        grid_spec=grid_spec,
        out_shape=jax.ShapeDtypeStruct((n, d), F32),
        compiler_params=pltpu.CompilerParams(
            dimension_semantics=("arbitrary",), vmem_limit_bytes=VMEM_LIMIT),
        name="moe_experts",
    )(*work, xs, wg, wu, wd)


def _combine_kernel(pos_ref, x_ref, rw_ref, gfin_ref, ys_ref, xo_ref, buf, sems, *, final):
    i = pl.program_id(0)
    n = pl.num_programs(0)
    rows = x_ref.shape[0]
    slot = i % 2

    def issue(step, slt):
        def body(blk, carry):
            for u in range(ISSUE_UNROLL):
                r = blk * ISSUE_UNROLL + u
                t = step * rows + r
                pltpu.make_async_copy(_row(ys_ref, pos_ref[2 * t]), _row(buf.at[slt], r),
                                      sems.at[slt]).start(priority=0)
                pltpu.make_async_copy(_row(ys_ref, pos_ref[2 * t + 1]), _row(buf.at[slt], rows + r),
                                      sems.at[slt]).start(priority=1)
            return carry
        lax.fori_loop(0, rows // ISSUE_UNROLL, body, 0)

    @pl.when(i == 0)
    def _():
        issue(0, 0)

    @pl.when(i + 1 < n)
    def _():
        issue(i + 1, 1 - slot)

    pltpu.make_async_copy(ys_ref.at[pl.ds(0, 2 * rows), :], buf.at[slot], sems.at[slot]).wait()

    rw = rw_ref[...]
    xn = x_ref[...] + rw[:, 2:3] * buf[slot, 0:rows, :] + rw[:, 3:4] * buf[slot, rows:2 * rows, :]
    if final:
        xn = _rms(xn, gfin_ref[...])
    xo_ref[...] = xn


def _combine(pos, x, rw, gfin, ys, final):
    t, d = x.shape
    grid_spec = pltpu.PrefetchScalarGridSpec(
        num_scalar_prefetch=1,
        grid=(t // ROW_TILE,),
        in_specs=[pl.BlockSpec((ROW_TILE, d), lambda i, *_: (i, 0)),
                  pl.BlockSpec((ROW_TILE, LANE), lambda i, *_: (i, 0)),
                  pl.BlockSpec((1, d), lambda i, *_: (0, 0)),
                  pl.BlockSpec(memory_space=pl.ANY)],
        out_specs=pl.BlockSpec((ROW_TILE, d), lambda i, *_: (i, 0)),
        scratch_shapes=[pltpu.VMEM((2, 2 * ROW_TILE, d), F32), pltpu.SemaphoreType.DMA((2,))],
    )
    return pl.pallas_call(
        functools.partial(_combine_kernel, final=final),
        grid_spec=grid_spec,
        out_shape=jax.ShapeDtypeStruct((t, d), F32),
        compiler_params=pltpu.CompilerParams(
            dimension_semantics=("arbitrary",), vmem_limit_bytes=VMEM_LIMIT),
        name="moe_combine",
    )(pos, x, rw, gfin, ys)


def _worklist(cnt, n_rows):
    cnt = cnt[:N_EXPERTS].astype(I32)
    ends = jnp.cumsum(cnt)
    starts = ends - cnt
    n_tiles = n_rows // MOE_TILE
    tile_lo = jnp.arange(n_tiles, dtype=I32) * MOE_TILE
    exp_lo = starts[1:]
    rank_t = jnp.arange(n_tiles, dtype=I32) + jnp.sum(exp_lo[None, :] < tile_lo[:, None], axis=1)
    rank_e = jnp.arange(N_EXPERTS - 1, dtype=I32) + jnp.sum(tile_lo[None, :] <= exp_lo[:, None], axis=1)
    vals = jnp.concatenate([tile_lo, exp_lo])
    ranks = jnp.concatenate([rank_t, rank_e])
    slot = jnp.arange(vals.shape[0], dtype=I32)
    lo = jnp.sum(jnp.where(ranks[None, :] == slot[:, None], vals[None, :], 0), axis=1)
    hi = jnp.concatenate([lo[1:], jnp.full((1,), n_rows, I32)])
    tile = jnp.minimum(lo // MOE_TILE, n_tiles - 1)
    expert = jnp.minimum(jnp.sum(ends[None, :] <= lo[:, None], axis=1), N_EXPERTS - 1).astype(I32)
    return tile, expert, lo, hi


def _moe(x, h3, lg, wg, wu, wd, gfin, final):
    t = x.shape[0]
    rw, cnt = _route(lg)
    pos = rw[:, 0:2].astype(I32).reshape(2 * t)
    xs = _dispatch(pos, h3)
    work = _worklist(cnt[0], 2 * t)
    ys = _experts(work, xs, wg, wu, wd)
    return _combine(pos, x, rw, gfin, ys, final)


def _router_weights(w_group, b_group, w_router, b_router):
    d = w_group.shape[0]
    w = jnp.concatenate([w_group] + [w_router[g] for g in range(N_GROUPS)], axis=1)
    w = jnp.pad(w, ((0, 0), (0, LANE - w.shape[1])))
    b = jnp.concatenate([b_group, b_router.reshape(-1)])
    b = jnp.pad(b, (0, LANE - b.shape[0])).reshape(1, LANE)
    hi = w.astype(BF16)
    lo = (w - hi.astype(F32)).astype(BF16)
    return hi, lo, b


def _rope_tables(seq):
    half = MLA_ROPE // 2
    pos = jnp.arange(seq, dtype=F32)
    inv = 1.0 / (ROPE_THETA ** (jnp.arange(0, MLA_ROPE, 2, dtype=F32) / MLA_ROPE))
    ang = pos[:, None] * inv[None, :]
    cos, sin = jnp.cos(ang), jnp.sin(ang)
    z = lambda n: jnp.zeros((seq, n), F32)
    tail = LANE - MLA_NOPE - MLA_ROPE
    ctab = jnp.concatenate([jnp.ones((seq, MLA_NOPE), F32), cos, cos, z(tail)], axis=1)
    atab = jnp.concatenate([z(MLA_NOPE), -sin, z(half), z(tail)], axis=1)
    btab = jnp.concatenate([z(MLA_NOPE), z(half), sin, z(tail)], axis=1)
    return ctab, atab, btab


def _odd_weights(w_in, w_uq, w_ukv):
    d = w_in.shape[0]
    sizes = (MLA_Q_RANK, MLA_KV_RANK, MLA_ROPE, SWA_HEADS * SWA_DIM, SWA_KV_HEADS * SWA_DIM,
             SWA_KV_HEADS * SWA_DIM)
    offs = [0]
    for s in sizes:
        offs.append(offs[-1] + s)
    cq, ckv, kpe, qs, ks, vs = [w_in[:, offs[k]:offs[k + 1]] for k in range(6)]
    kpe = jnp.pad(kpe, ((0, 0), (MLA_NOPE, LANE - MLA_NOPE - MLA_ROPE)))
    win = jnp.concatenate([cq, ckv, kpe, qs, ks, vs], axis=1).astype(BF16)
    hq = MLA_NOPE + MLA_ROPE
    wuq = jnp.pad(w_uq.reshape(MLA_Q_RANK, MLA_HEADS, hq), ((0, 0), (0, 0), (0, LANE - hq)))
    wuq = wuq.reshape(MLA_Q_RANK, MLA_HEADS * LANE).astype(BF16)
    wkv = w_ukv.reshape(MLA_KV_RANK, MLA_HEADS, MLA_NOPE + MLA_V)
    wuk = jnp.pad(wkv[:, :, :MLA_NOPE], ((0, 0), (0, 0), (0, LANE - MLA_NOPE)))
    wuk = wuk.reshape(MLA_KV_RANK, MLA_HEADS * LANE).astype(BF16)
    wuv = wkv[:, :, MLA_NOPE:].reshape(MLA_KV_RANK, MLA_HEADS * MLA_V).astype(BF16)
    return win, wuq, wuk, wuv


def kernel(x, norm_mix, norm_ffn, norm_final, ev_w_in, ev_conv_w, ev_conv_b, ev_cnorm_g, ev_cnorm_b,
           ev_vnorm_g, ev_vnorm_b, ev_w_sp, ev_b_sp, ev_w_out, od_w_in, od_g_cq, od_w_uq, od_g_ckv,
           od_w_ukv, od_sink, od_w_out, moe_w_group, moe_b_group, moe_w_router, moe_b_router,
           moe_w_gate, moe_w_up, moe_w_down):
    b, seq, d = x.shape
    t = b * seq
    depth = norm_mix.shape[0]
    row = lambda v: v.reshape(1, -1)
    xt = x.reshape(t, d)
    gfin = row(norm_final)
    for layer in range(depth):
        k = layer // 2
        wrh, wrl, br = _router_weights(moe_w_group[layer], moe_b_group[layer], moe_w_router[layer],
                                       moe_b_router[layer])
        gffn = row(norm_ffn[layer])
        if layer % 2 == 0:
            bsp = jnp.broadcast_to(ev_b_sp[k][:, :, None], ev_w_sp[k].shape)
            xt, h3, lg = _even_mixer(
                xt, seq, row(norm_mix[layer]), ev_w_in[k].astype(BF16), ev_conv_w[k], row(ev_conv_b[k]),
                row(ev_cnorm_g[k]), row(ev_cnorm_b[k]), row(ev_vnorm_g[k]), row(ev_vnorm_b[k]),
                ev_w_sp[k].astype(BF16), bsp, ev_w_out[k].astype(BF16), gffn, wrh, wrl, br)
        else:
            win, wuq, wuk, wuv = _odd_weights(od_w_in[k], od_w_uq[k], od_w_ukv[k])
            ctab, atab, btab = _rope_tables(seq)
            qm, km, vm, qs, ks, vs = _odd_proj(
                xt, seq, row(norm_mix[layer]), win, row(od_g_cq[k]), wuq, row(od_g_ckv[k]), wuk, wuv,
                ctab, atab, btab)
            xt, h3, lg = _attention(xt, seq, od_sink[k], qm, km, vm, qs, ks, vs,
                                    od_w_out[k].astype(BF16), gffn, wrh, wrl, br)
        xt = _moe(xt, h3, lg, moe_w_gate[layer].reshape(N_EXPERTS, d, -1),
                  moe_w_up[layer].reshape(N_EXPERTS, d, -1),
                  moe_w_down[layer].reshape(N_EXPERTS, -1, d), gfin, layer == depth - 1)
    return xt.reshape(b, seq, d)
```

```python
import functools

import jax
import jax.numpy as jnp
from jax import lax
from jax.experimental import pallas as pl
from jax.experimental.pallas import tpu as pltpu

F32 = jnp.float32
BF16 = jnp.bfloat16
I32 = jnp.int32

EPS = 1e-6
NEG_INF = -1e30
LOG2E = 1.4426950408889634
LANE = 128
VMEM_LIMIT = 56 * 1024 * 1024

CONV_DIM = 512
CONV_GROUP = 128
CONV_WIDTH = 31
SGU_DIM = 512
SGU_HEAD = 128
CHUNK = 128
MLA_HEADS = 8
MLA_Q_RANK = 256
MLA_KV_RANK = 128
MLA_NOPE = 64
MLA_ROPE = 32
MLA_V = 64
ROPE_THETA = 10000.0
SWA_HEADS = 8
SWA_KV_HEADS = 2
SWA_DIM = 64
WINDOW = 128
N_GROUPS = 4
N_EXP = 8
N_EXPERTS = N_GROUPS * N_EXP

HALO = 16
CONV_ROWS = 64
MOE_TILE = 256
ROW_TILE = 256
DISPATCH_TILE = 1024
ISSUE_UNROLL = 8


def _dot(a, b):
    return jnp.dot(a, b, preferred_element_type=F32)


def _dot_nt(a, b):
    return lax.dot_general(a, b, (((1,), (1,)), ((), ())), preferred_element_type=F32)


def _rms(x, g):
    return x * lax.rsqrt(jnp.mean(x * x, axis=-1, keepdims=True) + EPS) * g


def _ln(x, g, b):
    mu = jnp.mean(x, axis=-1, keepdims=True)
    d = x - mu
    var = jnp.mean(d * d, axis=-1, keepdims=True)
    return d * lax.rsqrt(var + EPS) * g + b


def _gelu(x):
    return 0.5 * x * (1.0 + jnp.tanh(0.7978845608028654 * (x + 0.044715 * (x * x * x))))


def _sigmoid(x):
    return 1.0 / (1.0 + jnp.exp(-x))


def _router_epilogue(xn, gffn_ref, wrh_ref, wrl_ref, br_ref, h3_ref, lg_ref):
    h = _rms(xn, gffn_ref[...])
    h3_ref[...] = h
    hi = h.astype(BF16)
    lo = (h - hi.astype(F32)).astype(BF16)
    wh = wrh_ref[...]
    lg_ref[...] = _dot(hi, wh) + _dot(lo, wh) + _dot(hi, wrl_ref[...]) + br_ref[...]


def _even_kernel(xc_ref, xp_ref, gmix_ref, win_ref, cw_ref, cb_ref, cng_ref, cnb_ref, vng_ref,
                 vnb_ref, wsp_ref, bsp_ref, wout_ref, gffn_ref, wrh_ref, wrl_ref, br_ref,
                 xo_ref, h3_ref, lg_ref, a_s, mix_s, *, ns, sub, seq):
    j = pl.program_id(1)

    @pl.when(j == 0)
    def _():
        a_s[0:HALO, :] = jnp.zeros((HALO, CONV_DIM), F32)

    @pl.when(j < ns)
    def _proj():
        r0 = pl.multiple_of(j * sub, sub)
        h = _rms(xc_ref[...], gmix_ref[...]).astype(BF16)
        pa = _dot(h, win_ref[:, 0:2 * CONV_DIM])
        a_s[pl.ds(HALO + r0, sub), :] = pa[:, :CONV_DIM] * _sigmoid(pa[:, CONV_DIM:])
        u = _gelu(_dot(h, win_ref[:, 2 * CONV_DIM:2 * CONV_DIM + SGU_DIM]))
        v = _gelu(_dot(h, win_ref[:, 2 * CONV_DIM + SGU_DIM:]))
        for hd in range(SGU_DIM // SGU_HEAD):
            cs = slice(hd * SGU_HEAD, (hd + 1) * SGU_HEAD)
            vn = _ln(v[:, cs], vng_ref[:, cs], vnb_ref[:, cs]).astype(BF16)
            for c in range(sub // CHUNK):
                rs = slice(c * CHUNK, (c + 1) * CHUNK)
                sv = _dot(wsp_ref[hd], vn[rs, :]) + bsp_ref[hd]
                mix_s[pl.ds(r0 + c * CHUNK, CHUNK),
                      CONV_DIM + hd * SGU_HEAD:CONV_DIM + (hd + 1) * SGU_HEAD] = (u[rs, cs] * sv).astype(BF16)

    @pl.when(j == ns)
    def _():
        a_s[HALO + seq:2 * HALO + seq, :] = jnp.zeros((HALO, CONV_DIM), F32)

    @pl.when(j >= 1)
    def _out():
        r0 = (j - 1) * sub
        win_rows = CONV_ROWS + 2 * HALO

        def chunk(i, carry):
            base = pl.multiple_of(r0 + i * CONV_ROWS, CONV_ROWS)
            win = a_s[pl.ds(base, win_rows), :]
            acc = jnp.zeros((CONV_ROWS, CONV_DIM), F32)
            for s in range(8):
                sh = win if s == 0 else pltpu.roll(win, win_rows - s, 0)
                for m in range(4):
                    k = 8 * m + s - 1
                    if 0 <= k < CONV_WIDTH:
                        acc = acc + sh[8 * m:8 * m + CONV_ROWS, :] * cw_ref[k:k + 1, :]
            acc = acc + cb_ref[...]
            for g in range(CONV_DIM // CONV_GROUP):
                cs = slice(g * CONV_GROUP, (g + 1) * CONV_GROUP)
                y = _ln(acc[:, cs], cng_ref[:, cs], cnb_ref[:, cs])
                mix_s[pl.ds(base, CONV_ROWS), cs] = (y * _sigmoid(y)).astype(BF16)
            return carry

        lax.fori_loop(0, sub // CONV_ROWS, chunk, 0)
        mix = mix_s[pl.ds(pl.multiple_of(r0, sub), sub), :]
        xn = xp_ref[...] + _dot(mix, wout_ref[...])
        xo_ref[...] = xn
        _router_epilogue(xn, gffn_ref, wrh_ref, wrl_ref, br_ref, h3_ref, lg_ref)


def _const_spec(shape):
    nd = len(shape)
    return pl.BlockSpec(shape, lambda *_: (0,) * nd)


def _even_mixer(x, seq, gmix, win, cw, cb, cng, cnb, vng, vnb, wsp, bsp, wout, gffn, wrh, wrl, br):
    t, d = x.shape
    nb = t // seq
    sub = 512
    ns = seq // sub
    row = lambda b, j: (b * ns + jnp.minimum(j, ns - 1), 0)
    prev = lambda b, j: (b * ns + jnp.maximum(j - 1, 0), 0)
    consts = (gmix, win, cw, cb, cng, cnb, vng, vnb, wsp, bsp, wout, gffn, wrh, wrl, br)
    return pl.pallas_call(
        functools.partial(_even_kernel, ns=ns, sub=sub, seq=seq),
        grid=(nb, ns + 1),
        in_specs=[pl.BlockSpec((sub, d), row), pl.BlockSpec((sub, d), prev)]
        + [_const_spec(c.shape) for c in consts],
        out_specs=[pl.BlockSpec((sub, d), prev),
                   pl.BlockSpec((sub, d), prev),
                   pl.BlockSpec((sub, LANE), prev)],
        out_shape=[jax.ShapeDtypeStruct((t, d), F32),
                   jax.ShapeDtypeStruct((t, d), F32),
                   jax.ShapeDtypeStruct((t, LANE), F32)],
        scratch_shapes=[pltpu.VMEM((seq + 2 * HALO, CONV_DIM), F32),
                        pltpu.VMEM((seq, CONV_DIM + SGU_DIM), BF16)],
        compiler_params=pltpu.CompilerParams(
            dimension_semantics=("arbitrary", "arbitrary"), vmem_limit_bytes=VMEM_LIMIT),
        name="even_mixer",
    )(x, x, *consts)


def _odd_proj_kernel(x_ref, gmix_ref, win_ref, gcq_ref, wuq_ref, gckv_ref, wuk_ref, wuv_ref, vone_ref,
                     cos_ref, sa_ref, sb_ref, qm_ref, km_ref, vm_ref, qs_ref, ks_ref, vs_ref):
    h = _rms(x_ref[...], gmix_ref[...]).astype(BF16)
    p = _dot(h, win_ref[...])
    o = 0
    cq = p[:, o:o + MLA_Q_RANK]; o += MLA_Q_RANK
    ckv = p[:, o:o + MLA_KV_RANK]; o += MLA_KV_RANK
    kpe = p[:, o:o + LANE]; o += LANE
    qs = p[:, o:o + SWA_HEADS * SWA_DIM]; o += SWA_HEADS * SWA_DIM
    ks = p[:, o:o + LANE]; o += LANE
    vs = p[:, o:o + LANE]

    cos = cos_ref[...]
    sa = sa_ref[...]
    sb = sb_ref[...]

    def rope(z, reps):
        w = z.shape[1]
        c = jnp.concatenate([cos] * reps, axis=1) if reps > 1 else cos
        a = jnp.concatenate([sa] * reps, axis=1) if reps > 1 else sa
        b = jnp.concatenate([sb] * reps, axis=1) if reps > 1 else sb
        return z * c + pltpu.roll(z, w - MLA_ROPE // 2, 1) * a + pltpu.roll(z, MLA_ROPE // 2, 1) * b

    q = _dot(_rms(cq, gcq_ref[...]).astype(BF16), wuq_ref[...])
    qm_ref[...] = (rope(q, MLA_HEADS) * ((MLA_NOPE + MLA_ROPE) ** -0.5 * LOG2E)).astype(BF16)
    ckvn = _rms(ckv, gckv_ref[...]).astype(BF16)
    kr = rope(kpe, 1)
    km_ref[...] = (_dot(ckvn, wuk_ref[...]) + jnp.concatenate([kr] * MLA_HEADS, axis=1)).astype(BF16)
    vm_ref[...] = (_dot(ckvn, wuv_ref[...]) + vone_ref[...]).astype(BF16)
    qs_ref[...] = (qs * (SWA_DIM ** -0.5)).astype(BF16)
    ks_ref[...] = ks.astype(BF16)
    vs_ref[...] = vs.astype(BF16)


def _odd_proj(x, seq, gmix, win, gcq, wuq, gckv, wuk, wuv, vone, cos, sa, sb):
    t, d = x.shape
    tm = 512
    nsq = seq // tm
    row = lambda i: (i, 0)
    pos = lambda i: (i % nsq, 0)
    consts = (gmix, win, gcq, wuq, gckv, wuk, wuv, vone)
    widths = (MLA_HEADS * LANE, MLA_HEADS * LANE, MLA_HEADS * LANE, SWA_HEADS * SWA_DIM, LANE, LANE)
    return pl.pallas_call(
        _odd_proj_kernel,
        grid=(t // tm,),
        in_specs=[pl.BlockSpec((tm, d), row)] + [_const_spec(c.shape) for c in consts]
        + [pl.BlockSpec((tm, LANE), pos)] * 3,
        out_specs=[pl.BlockSpec((tm, w), row) for w in widths],
        out_shape=[jax.ShapeDtypeStruct((t, w), BF16) for w in widths],
        compiler_params=pltpu.CompilerParams(
            dimension_semantics=("arbitrary",), vmem_limit_bytes=VMEM_LIMIT),
        name="odd_proj",
    )(x, *consts, cos, sa, sb)


def _attn_kernel(sink_ref, x_ref, qm_ref, km_ref, vm_ref, qs_ref, ks_ref, vs_ref, wout_ref,
                 gffn_ref, wrh_ref, wrl_ref, br_ref, xo_ref, h3_ref, lg_ref, *, tq, seq):
    i = pl.program_id(1)
    qm = qm_ref[...]
    outs = []
    for hd in range(MLA_HEADS):
        cs = slice(hd * LANE, (hd + 1) * LANE)
        s = _dot_nt(qm[:, cs], km_ref[:, cs])
        e = jnp.exp2((s - jnp.max(s, axis=-1, keepdims=True)).astype(BF16))
        pv = _dot(e, vm_ref[:, cs])
        outs.append(pv[:, :MLA_V] / pv[:, MLA_V:MLA_V + 1])

    span = LANE + 2 * WINDOW
    rep = SWA_HEADS // SWA_KV_HEADS
    swa_rows = []
    for blk in range(tq // LANE):
        q0 = i * tq + blk * LANE
        start = pl.multiple_of(jnp.clip(q0 - WINDOW, 0, seq - span), LANE)
        kw = ks_ref[pl.ds(start, span), :]
        vw = vs_ref[pl.ds(start, span), :]
        qpos = q0 + lax.broadcasted_iota(I32, (LANE, span), 0)
        kpos = start + lax.broadcasted_iota(I32, (LANE, span), 1)
        absd = jnp.abs(qpos - kpos).astype(F32)
        in_win = absd <= float(WINDOW)
        heads = []
        for hd in range(SWA_HEADS):
            g = hd // rep
            q = qs_ref[blk * LANE:(blk + 1) * LANE, hd * SWA_DIM:(hd + 1) * SWA_DIM]
            s = _dot_nt(q, kw[:, g * SWA_DIM:(g + 1) * SWA_DIM])
            s = jnp.where(in_win, s - (2.0 ** -(hd + 1)) * absd, NEG_INF)
            sk = sink_ref[hd]
            m = jnp.maximum(jnp.max(s, axis=-1, keepdims=True), sk)
            e = jnp.exp(s - m)
            den = jnp.sum(e, axis=-1, keepdims=True) + jnp.exp(sk - m)
            heads.append(_dot(e.astype(BF16), vw[:, g * SWA_DIM:(g + 1) * SWA_DIM]) / den)
        swa_rows.append(jnp.concatenate(heads, axis=1))
    swa = jnp.concatenate(swa_rows, axis=0) if len(swa_rows) > 1 else swa_rows[0]

    mix = jnp.concatenate(outs + [swa], axis=1).astype(BF16)
    xn = x_ref[...] + _dot(mix, wout_ref[...])
    xo_ref[...] = xn
    _router_epilogue(xn, gffn_ref, wrh_ref, wrl_ref, br_ref, h3_ref, lg_ref)


def _attention(x, seq, sink, qm, km, vm, qs, ks, vs, wout, gffn, wrh, wrl, br):
    t, d = x.shape
    nb = t // seq
    tq = 512
    nq = seq // tq
    row = lambda b, i, *_: (b * nq + i, 0)
    bat = lambda b, i, *_: (b, 0)
    consts = (wout, gffn, wrh, wrl, br)
    grid_spec = pltpu.PrefetchScalarGridSpec(
        num_scalar_prefetch=1,
        grid=(nb, nq),
        in_specs=[pl.BlockSpec((tq, d), row),
                  pl.BlockSpec((tq, qm.shape[1]), row),
                  pl.BlockSpec((seq, km.shape[1]), bat),
                  pl.BlockSpec((seq, vm.shape[1]), bat),
                  pl.BlockSpec((tq, qs.shape[1]), row),
                  pl.BlockSpec((seq, ks.shape[1]), bat),
                  pl.BlockSpec((seq, vs.shape[1]), bat)]
        + [pl.BlockSpec(c.shape, lambda b, i, *_, n=len(c.shape): (0,) * n) for c in consts],
        out_specs=[pl.BlockSpec((tq, d), row),
                   pl.BlockSpec((tq, d), row),
                   pl.BlockSpec((tq, LANE), row)],
    )
    return pl.pallas_call(
        functools.partial(_attn_kernel, tq=tq, seq=seq),
        grid_spec=grid_spec,
        out_shape=[jax.ShapeDtypeStruct((t, d), F32),
                   jax.ShapeDtypeStruct((t, d), F32),
                   jax.ShapeDtypeStruct((t, LANE), F32)],
        compiler_params=pltpu.CompilerParams(
            dimension_semantics=("arbitrary", "arbitrary"), vmem_limit_bytes=VMEM_LIMIT),
        name="attention",
    )(sink, x, qm, km, vm, qs, ks, vs, *consts)


def _route_kernel(lg_ref, lower_ref, out_ref, cnt_ref, carry_ref):
    i = pl.program_id(0)
    rows = lg_ref.shape[0]
    lg = lg_ref[...]
    lane = lax.broadcasted_iota(I32, (rows, LANE), 1)
    ninf = -jnp.inf

    @pl.when(i == 0)
    def _():
        carry_ref[...] = jnp.zeros_like(carry_ref)

    gl = jnp.where(lane < N_GROUPS, lg, ninf)
    gmax = jnp.max(gl, axis=1, keepdims=True)
    gidx = jnp.min(jnp.where(gl == gmax, lane, LANE), axis=1, keepdims=True)
    gp = 1.0 / jnp.sum(jnp.exp(gl - gmax), axis=1, keepdims=True)

    first = N_GROUPS + gidx * N_EXP
    el = jnp.where((lane >= first) & (lane < first + N_EXP), lg, ninf)
    m1 = jnp.max(el, axis=1, keepdims=True)
    i1 = jnp.min(jnp.where(el == m1, lane, LANE), axis=1, keepdims=True)
    el2 = jnp.where(lane == i1, ninf, el)
    m2 = jnp.max(el2, axis=1, keepdims=True)
    i2 = jnp.min(jnp.where(el2 == m2, lane, LANE), axis=1, keepdims=True)
    e2 = jnp.exp(m2 - m1)
    w1 = gp / (1.0 + e2)
    w2 = gp * e2 / (1.0 + e2)
    id1 = i1 - N_GROUPS
    id2 = i2 - N_GROUPS
    onehot = jnp.where((lane == id1) | (lane == id2), 1.0, 0.0)

    before = _dot(lower_ref[...], onehot.astype(BF16)) + carry_ref[...]
    rank1 = jnp.sum(jnp.where(lane == id1, before, 0.0), axis=1, keepdims=True)
    rank2 = jnp.sum(jnp.where(lane == id2, before, 0.0), axis=1, keepdims=True)
    carry_ref[...] += jnp.sum(onehot, axis=0, keepdims=True)
    out = jnp.where(lane == 0, rank1, 0.0)
    out = jnp.where(lane == 1, rank2, out)
    out = jnp.where(lane == 2, w1, out)
    out = jnp.where(lane == 3, w2, out)
    out = jnp.where(lane == 4, id1.astype(F32), out)
    out = jnp.where(lane == 5, id2.astype(F32), out)
    out_ref[...] = out
    cnt_ref[...] = carry_ref[...]


def _route(lg):
    t = lg.shape[0]
    rows = 512
    lower = jnp.tri(rows, k=-1, dtype=BF16)
    return pl.pallas_call(
        _route_kernel,
        grid=(t // rows,),
        in_specs=[pl.BlockSpec((rows, LANE), lambda i: (i, 0)),
                  pl.BlockSpec((rows, rows), lambda i: (0, 0))],
        out_specs=[pl.BlockSpec((rows, LANE), lambda i: (i, 0)),
                   pl.BlockSpec((1, LANE), lambda i: (0, 0))],
        out_shape=[jax.ShapeDtypeStruct((t, LANE), F32), jax.ShapeDtypeStruct((1, LANE), F32)],
        scratch_shapes=[pltpu.VMEM((1, LANE), F32)],
        compiler_params=pltpu.CompilerParams(dimension_semantics=("arbitrary",)),
        name="moe_route",
    )(lg, lower)


def _row(ref, r):
    return ref.at[pl.ds(r, 1), :]


def _dispatch_kernel(pos_ref, h_ref, xs_ref, sem):
    i = pl.program_id(0)
    rows = h_ref.shape[0]

    def issue(blk, carry):
        for u in range(ISSUE_UNROLL):
            r = blk * ISSUE_UNROLL + u
            t = i * rows + r
            pltpu.make_async_copy(_row(h_ref, r), _row(xs_ref, pos_ref[2 * t]), sem).start(priority=0)
            pltpu.make_async_copy(_row(h_ref, r), _row(xs_ref, pos_ref[2 * t + 1]), sem).start(priority=1)
        return carry

    lax.fori_loop(0, rows // ISSUE_UNROLL, issue, 0)
    for _ in range(2):
        pltpu.make_async_copy(h_ref, xs_ref.at[pl.ds(0, rows), :], sem).wait()


def _dispatch(pos, h):
    t, d = h.shape
    grid_spec = pltpu.PrefetchScalarGridSpec(
        num_scalar_prefetch=1,
        grid=(t // DISPATCH_TILE,),
        in_specs=[pl.BlockSpec((DISPATCH_TILE, d), lambda i, *_: (i, 0))],
        out_specs=pl.BlockSpec(memory_space=pl.ANY),
        scratch_shapes=[pltpu.SemaphoreType.DMA],
    )
    return pl.pallas_call(
        _dispatch_kernel,
        grid_spec=grid_spec,
        out_shape=jax.ShapeDtypeStruct((2 * t, d), F32),
        compiler_params=pltpu.CompilerParams(dimension_semantics=("arbitrary",)),
        name="moe_dispatch",
    )(pos, h)


def _expert_kernel(tile_ref, exp_ref, lo_ref, hi_ref, xs_ref, wg_ref, wu_ref, wd_ref, ys_ref,
                   wg_s, wu_s, wd_s):
    i = pl.program_id(0)
    prev = jnp.maximum(i - 1, 0)
    new_tile = (i == 0) | (tile_ref[i] != tile_ref[prev])
    new_exp = (i == 0) | (exp_ref[i] != exp_ref[prev])
    lo = lo_ref[i]
    hi = hi_ref[i]
    rows = xs_ref.shape[0]

    @pl.when(new_exp)
    def _():
        wg_s[...] = wg_ref[0].astype(BF16)
        wu_s[...] = wu_ref[0].astype(BF16)
        wd_s[...] = wd_ref[0].astype(BF16)

    @pl.when(new_tile)
    def _():
        ys_ref[...] = jnp.zeros_like(ys_ref)

    @pl.when(hi > lo)
    def _():
        x = xs_ref[...].astype(BF16)
        g = _dot(x, wg_s[...])
        u = _dot(x, wu_s[...])
        hid = (g * _sigmoid(g) * u).astype(BF16)
        y = _dot(hid, wd_s[...])
        r = tile_ref[i] * rows + lax.broadcasted_iota(I32, (rows, 1), 0)
        ys_ref[...] += jnp.where((r >= lo) & (r < hi), y, 0.0)


def _experts(work, first_expert, xs, wg, wu, wd):
    n, d = xs.shape
    f = wg.shape[2]
    nwork = work[0].shape[0]
    wmap = lambda i, tl, ex, lo, hi: (first_expert + ex[i], 0, 0)
    grid_spec = pltpu.PrefetchScalarGridSpec(
        num_scalar_prefetch=4,
        grid=(nwork,),
        in_specs=[pl.BlockSpec((MOE_TILE, d), lambda i, tl, ex, lo, hi: (tl[i], 0)),
                  pl.BlockSpec((1, d, f), wmap),
                  pl.BlockSpec((1, d, f), wmap),
                  pl.BlockSpec((1, f, d), wmap)],
        out_specs=pl.BlockSpec((MOE_TILE, d), lambda i, tl, ex, lo, hi: (tl[i], 0)),
        scratch_shapes=[pltpu.VMEM((d, f), BF16), pltpu.VMEM((d, f), BF16), pltpu.VMEM((f, d), BF16)],
    )
    return pl.pallas_call(
        _expert_kernel,
        grid_spec=grid_spec,
        out_shape=jax.ShapeDtypeStruct((n, d), F32),
        compiler_params=pltpu.CompilerParams(
            dimension_semantics=("arbitrary",), vmem_limit_bytes=VMEM_LIMIT),
        name="moe_experts",
    )(*work, xs, wg, wu, wd)


def _combine_kernel(pos_ref, x_ref, rw_ref, gfin_ref, ys_ref, xo_ref, buf, sems, *, final):
    i = pl.program_id(0)
    n = pl.num_programs(0)
    rows = x_ref.shape[0]
    slot = i % 2

    def issue(step, slt):
        def body(blk, carry):
            for u in range(ISSUE_UNROLL):
                r = blk * ISSUE_UNROLL + u
                t = step * rows + r
                pltpu.make_async_copy(_row(ys_ref, pos_ref[2 * t]), _row(buf.at[slt], r),
                                      sems.at[slt]).start(priority=0)
                pltpu.make_async_copy(_row(ys_ref, pos_ref[2 * t + 1]), _row(buf.at[slt], rows + r),
                                      sems.at[slt]).start(priority=1)
            return carry
        lax.fori_loop(0, rows // ISSUE_UNROLL, body, 0)

    @pl.when(i == 0)
    def _():
        issue(0, 0)

    @pl.when(i + 1 < n)
    def _():
        issue(i + 1, 1 - slot)

    pltpu.make_async_copy(ys_ref.at[pl.ds(0, 2 * rows), :], buf.at[slot], sems.at[slot]).wait()

    rw = rw_ref[...]
    xn = x_ref[...] + rw[:, 2:3] * buf[slot, 0:rows, :] + rw[:, 3:4] * buf[slot, rows:2 * rows, :]
    if final:
        xn = _rms(xn, gfin_ref[...])
    xo_ref[...] = xn


def _combine(pos, x, rw, gfin, ys, final):
    t, d = x.shape
    grid_spec = pltpu.PrefetchScalarGridSpec(
        num_scalar_prefetch=1,
        grid=(t // ROW_TILE,),
        in_specs=[pl.BlockSpec((ROW_TILE, d), lambda i, *_: (i, 0)),
                  pl.BlockSpec((ROW_TILE, LANE), lambda i, *_: (i, 0)),
                  pl.BlockSpec((1, d), lambda i, *_: (0, 0)),
                  pl.BlockSpec(memory_space=pl.ANY)],
        out_specs=pl.BlockSpec((ROW_TILE, d), lambda i, *_: (i, 0)),
        scratch_shapes=[pltpu.VMEM((2, 2 * ROW_TILE, d), F32), pltpu.SemaphoreType.DMA((2,))],
    )
    return pl.pallas_call(
        functools.partial(_combine_kernel, final=final),
        grid_spec=grid_spec,
        out_shape=jax.ShapeDtypeStruct((t, d), F32),
        compiler_params=pltpu.CompilerParams(
            dimension_semantics=("arbitrary",), vmem_limit_bytes=VMEM_LIMIT),
        name="moe_combine",
    )(pos, x, rw, gfin, ys)


def _worklist(cnt, n_rows):
    ends = jnp.cumsum(cnt)
    starts = ends - cnt
    n_tiles = n_rows // MOE_TILE
    tile_lo = jnp.arange(n_tiles, dtype=I32) * MOE_TILE
    exp_lo = starts[1:]
    rank_t = jnp.arange(n_tiles, dtype=I32) + jnp.sum(exp_lo[None, :] < tile_lo[:, None], axis=1)
    rank_e = jnp.arange(N_EXPERTS - 1, dtype=I32) + jnp.sum(tile_lo[None, :] <= exp_lo[:, None], axis=1)
    vals = jnp.concatenate([tile_lo, exp_lo])
    ranks = jnp.concatenate([rank_t, rank_e])
    slot = jnp.arange(vals.shape[0], dtype=I32)
    lo = jnp.sum(jnp.where(ranks[None, :] == slot[:, None], vals[None, :], 0), axis=1)
    hi = jnp.concatenate([lo[1:], jnp.full((1,), n_rows, I32)])
    tile = jnp.minimum(lo // MOE_TILE, n_tiles - 1)
    expert = jnp.minimum(jnp.sum(ends[None, :] <= lo[:, None], axis=1), N_EXPERTS - 1).astype(I32)
    return tile, expert, lo, hi


def _moe(x, h, lg, wg, wu, wd, layer, gfin, final):
    t = x.shape[0]
    rw, cnt = _route(lg)
    cnt = cnt[0, :N_EXPERTS].astype(I32)
    starts = jnp.cumsum(cnt) - cnt
    ids = rw[:, 4:6].astype(I32)
    offs = jnp.sum(jnp.where(ids[:, :, None] == jnp.arange(N_EXPERTS, dtype=I32), starts, 0), axis=-1)
    pos = (offs + rw[:, 0:2].astype(I32)).reshape(2 * t)
    xs = _dispatch(pos, h)
    work = _worklist(cnt, 2 * t)
    ys = _experts(work, layer * N_EXPERTS, xs, wg, wu, wd)
    return _combine(pos, x, rw, gfin, ys, final)


def _router_weights(w_group, b_group, w_router, b_router):
    d = w_group.shape[0]
    w = jnp.concatenate([w_group] + [w_router[g] for g in range(N_GROUPS)], axis=1)
    w = jnp.pad(w, ((0, 0), (0, LANE - w.shape[1])))
    b = jnp.concatenate([b_group, b_router.reshape(-1)])
    b = jnp.pad(b, (0, LANE - b.shape[0])).reshape(1, LANE)
    hi = w.astype(BF16)
    lo = (w - hi.astype(F32)).astype(BF16)
    return hi, lo, b


def _rope_tables(seq):
    half = MLA_ROPE // 2
    pos = jnp.arange(seq, dtype=F32)
    inv = 1.0 / (ROPE_THETA ** (jnp.arange(0, MLA_ROPE, 2, dtype=F32) / MLA_ROPE))
    ang = pos[:, None] * inv[None, :]
    cos, sin = jnp.cos(ang), jnp.sin(ang)
    z = lambda n: jnp.zeros((seq, n), F32)
    tail = LANE - MLA_NOPE - MLA_ROPE
    ctab = jnp.concatenate([jnp.ones((seq, MLA_NOPE), F32), cos, cos, z(tail)], axis=1)
    atab = jnp.concatenate([z(MLA_NOPE), -sin, z(half), z(tail)], axis=1)
    btab = jnp.concatenate([z(MLA_NOPE), z(half), sin, z(tail)], axis=1)
    return ctab, atab, btab


def _odd_weights(w_in, w_uq, w_ukv):
    d = w_in.shape[0]
    sizes = (MLA_Q_RANK, MLA_KV_RANK, MLA_ROPE, SWA_HEADS * SWA_DIM, SWA_KV_HEADS * SWA_DIM,
             SWA_KV_HEADS * SWA_DIM)
    offs = [0]
    for s in sizes:
        offs.append(offs[-1] + s)
    cq, ckv, kpe, qs, ks, vs = [w_in[:, offs[k]:offs[k + 1]] for k in range(6)]
    kpe = jnp.pad(kpe, ((0, 0), (MLA_NOPE, LANE - MLA_NOPE - MLA_ROPE)))
    win = jnp.concatenate([cq, ckv, kpe, qs, ks, vs], axis=1).astype(BF16)
    hq = MLA_NOPE + MLA_ROPE
    wuq = jnp.pad(w_uq.reshape(MLA_Q_RANK, MLA_HEADS, hq), ((0, 0), (0, 0), (0, LANE - hq)))
    wuq = wuq.reshape(MLA_Q_RANK, MLA_HEADS * LANE).astype(BF16)
    wkv = w_ukv.reshape(MLA_KV_RANK, MLA_HEADS, MLA_NOPE + MLA_V)
    wuk = jnp.pad(wkv[:, :, :MLA_NOPE], ((0, 0), (0, 0), (0, LANE - MLA_NOPE)))
    wuk = wuk.reshape(MLA_KV_RANK, MLA_HEADS * LANE).astype(BF16)
    wuv = jnp.pad(wkv[:, :, MLA_NOPE:], ((0, 0), (0, 0), (0, LANE - MLA_V)))
    wuv = wuv.reshape(MLA_KV_RANK, MLA_HEADS * LANE).astype(BF16)
    vone = jnp.tile(jnp.zeros((LANE,), F32).at[MLA_V].set(1.0), MLA_HEADS).reshape(1, MLA_HEADS * LANE)
    return win, wuq, wuk, wuv, vone


def kernel(x, norm_mix, norm_ffn, norm_final, ev_w_in, ev_conv_w, ev_conv_b, ev_cnorm_g, ev_cnorm_b,
           ev_vnorm_g, ev_vnorm_b, ev_w_sp, ev_b_sp, ev_w_out, od_w_in, od_g_cq, od_w_uq, od_g_ckv,
           od_w_ukv, od_sink, od_w_out, moe_w_group, moe_b_group, moe_w_router, moe_b_router,
           moe_w_gate, moe_w_up, moe_w_down):
    b, seq, d = x.shape
    t = b * seq
    depth = norm_mix.shape[0]
    row = lambda v: v.reshape(1, -1)
    xt = x.reshape(t, d)
    gfin = row(norm_final)
    wg_all = moe_w_gate.reshape(depth * N_EXPERTS, d, -1)
    wu_all = moe_w_up.reshape(depth * N_EXPERTS, d, -1)
    wd_all = moe_w_down.reshape(depth * N_EXPERTS, -1, d)
    for layer in range(depth):
        k = layer // 2
        wrh, wrl, br = _router_weights(moe_w_group[layer], moe_b_group[layer], moe_w_router[layer],
                                       moe_b_router[layer])
        gffn = row(norm_ffn[layer])
        if layer % 2 == 0:
            bsp = jnp.broadcast_to(ev_b_sp[k][:, :, None], ev_w_sp[k].shape)
            xt, h3, lg = _even_mixer(
                xt, seq, row(norm_mix[layer]), ev_w_in[k].astype(BF16), ev_conv_w[k], row(ev_conv_b[k]),
                row(ev_cnorm_g[k]), row(ev_cnorm_b[k]), row(ev_vnorm_g[k]), row(ev_vnorm_b[k]),
                ev_w_sp[k].astype(BF16), bsp, ev_w_out[k].astype(BF16), gffn, wrh, wrl, br)
        else:
            win, wuq, wuk, wuv, vone = _odd_weights(od_w_in[k], od_w_uq[k], od_w_ukv[k])
            ctab, atab, btab = _rope_tables(seq)
            qm, km, vm, qs, ks, vs = _odd_proj(
                xt, seq, row(norm_mix[layer]), win, row(od_g_cq[k]), wuq, row(od_g_ckv[k]), wuk, wuv,
                vone, ctab, atab, btab)
            xt, h3, lg = _attention(xt, seq, od_sink[k], qm, km, vm, qs, ks, vs,
                                    od_w_out[k].astype(BF16), gffn, wrh, wrl, br)
        xt = _moe(xt, h3, lg, wg_all, wu_all, wd_all, layer, gfin, layer == depth - 1)
    return xt.reshape(b, seq, d)
```

```python
import functools

import jax
import jax.numpy as jnp
from jax import lax
from jax.experimental import pallas as pl
from jax.experimental.pallas import tpu as pltpu

F32 = jnp.float32
BF16 = jnp.bfloat16
I32 = jnp.int32

EPS = 1e-6
NEG_INF = -1e30
LOG2E = 1.4426950408889634
LANE = 128
VMEM_LIMIT = 56 * 1024 * 1024

CONV_DIM = 512
CONV_GROUP = 128
CONV_WIDTH = 31
SGU_DIM = 512
SGU_HEAD = 128
CHUNK = 128
MLA_HEADS = 8
MLA_Q_RANK = 256
MLA_KV_RANK = 128
MLA_NOPE = 64
MLA_ROPE = 32
MLA_V = 64
ROPE_THETA = 10000.0
SWA_HEADS = 8
SWA_KV_HEADS = 2
SWA_DIM = 64
WINDOW = 128
N_GROUPS = 4
N_EXP = 8
N_EXPERTS = N_GROUPS * N_EXP

HALO = 16
CONV_ROWS = 64
MOE_TILE = 256
ROW_TILE = 256
DISPATCH_TILE = 512


def _dot(a, b):
    return jnp.dot(a, b, preferred_element_type=F32)


def _dot_nt(a, b):
    return lax.dot_general(a, b, (((1,), (1,)), ((), ())), preferred_element_type=F32)


def _rms(x, g):
    return x * lax.rsqrt(jnp.mean(x * x, axis=-1, keepdims=True) + EPS) * g


def _ln(x, g, b):
    mu = jnp.mean(x, axis=-1, keepdims=True)
    d = x - mu
    var = jnp.mean(d * d, axis=-1, keepdims=True)
    return d * lax.rsqrt(var + EPS) * g + b


def _gelu(x):
    return 0.5 * x * (1.0 + jnp.tanh(0.7978845608028654 * (x + 0.044715 * (x * x * x))))


def _sigmoid(x):
    return 1.0 / (1.0 + jnp.exp(-x))


def _router_epilogue(xn, gffn_ref, wrh_ref, wrl_ref, br_ref, h3_ref, lg_ref):
    h = _rms(xn, gffn_ref[...])
    h3_ref[...] = h
    hi = h.astype(BF16)
    lo = (h - hi.astype(F32)).astype(BF16)
    wh = wrh_ref[...]
    lg_ref[...] = _dot(hi, wh) + _dot(lo, wh) + _dot(hi, wrl_ref[...]) + br_ref[...]


def _even_kernel(xc_ref, xp_ref, gmix_ref, win_ref, cw_ref, cb_ref, cng_ref, cnb_ref, vng_ref,
                 vnb_ref, wsp_ref, bsp_ref, wout_ref, gffn_ref, wrh_ref, wrl_ref, br_ref,
                 xo_ref, h3_ref, lg_ref, a_s, mix_s, *, ns, sub, seq):
    j = pl.program_id(1)
    win_rows = CONV_ROWS + 2 * HALO
    n_chunks = sub // CONV_ROWS

    def proj_conv_input():
        h = _rms(xc_ref[...], gmix_ref[...]).astype(BF16)
        pa = _dot(h, win_ref[:, 0:2 * CONV_DIM])
        return h, pa[:, :CONV_DIM] * _sigmoid(pa[:, CONV_DIM:])

    def proj_gating(h, a):
        r0 = pl.multiple_of(j * sub, sub)
        a_s[pl.ds(HALO + r0, sub), :] = a
        u = _gelu(_dot(h, win_ref[:, 2 * CONV_DIM:2 * CONV_DIM + SGU_DIM]))
        v = _gelu(_dot(h, win_ref[:, 2 * CONV_DIM + SGU_DIM:]))
        for hd in range(SGU_DIM // SGU_HEAD):
            cs = slice(hd * SGU_HEAD, (hd + 1) * SGU_HEAD)
            vn = _ln(v[:, cs], vng_ref[:, cs], vnb_ref[:, cs]).astype(BF16)
            for c in range(sub // CHUNK):
                rs = slice(c * CHUNK, (c + 1) * CHUNK)
                sv = _dot(wsp_ref[hd], vn[rs, :]) + bsp_ref[hd]
                mix_s[pl.ds(r0 + c * CHUNK, CHUNK),
                      CONV_DIM + hd * SGU_HEAD:CONV_DIM + (hd + 1) * SGU_HEAD] = (u[rs, cs] * sv).astype(BF16)

    def conv_and_out(next_halo):
        r0 = (j - 1) * sub
        for i in range(n_chunks):
            base = pl.multiple_of(r0 + i * CONV_ROWS, CONV_ROWS)
            if i < n_chunks - 1:
                win = a_s[pl.ds(base, win_rows), :]
            else:
                win = jnp.concatenate([a_s[pl.ds(base, CONV_ROWS + HALO), :], next_halo], axis=0)
            acc = jnp.zeros((CONV_ROWS, CONV_DIM), F32)
            for s in range(8):
                sh = win if s == 0 else pltpu.roll(win, win_rows - s, 0)
                for m in range(4):
                    k = 8 * m + s - 1
                    if 0 <= k < CONV_WIDTH:
                        acc = acc + sh[8 * m:8 * m + CONV_ROWS, :] * cw_ref[k:k + 1, :]
            acc = acc + cb_ref[...]
            for g in range(CONV_DIM // CONV_GROUP):
                cs = slice(g * CONV_GROUP, (g + 1) * CONV_GROUP)
                y = _ln(acc[:, cs], cng_ref[:, cs], cnb_ref[:, cs])
                mix_s[pl.ds(base, CONV_ROWS), cs] = (y * _sigmoid(y)).astype(BF16)
        mix = mix_s[pl.ds(pl.multiple_of(r0, sub), sub), :]
        xn = xp_ref[...] + _dot(mix, wout_ref[...])
        xo_ref[...] = xn
        _router_epilogue(xn, gffn_ref, wrh_ref, wrl_ref, br_ref, h3_ref, lg_ref)

    @pl.when(j == 0)
    def _():
        a_s[0:HALO, :] = jnp.zeros((HALO, CONV_DIM), F32)
        h, a = proj_conv_input()
        proj_gating(h, a)

    @pl.when((j >= 1) & (j < ns))
    def _():
        h, a = proj_conv_input()
        conv_and_out(a[0:HALO, :])
        proj_gating(h, a)

    @pl.when(j == ns)
    def _():
        conv_and_out(jnp.zeros((HALO, CONV_DIM), F32))


def _const_spec(shape):
    nd = len(shape)
    return pl.BlockSpec(shape, lambda *_: (0,) * nd)


def _even_mixer(x, seq, gmix, win, cw, cb, cng, cnb, vng, vnb, wsp, bsp, wout, gffn, wrh, wrl, br):
    t, d = x.shape
    nb = t // seq
    sub = 512
    ns = seq // sub
    row = lambda b, j: (b * ns + jnp.minimum(j, ns - 1), 0)
    prev = lambda b, j: (b * ns + jnp.maximum(j - 1, 0), 0)
    consts = (gmix, win, cw, cb, cng, cnb, vng, vnb, wsp, bsp, wout, gffn, wrh, wrl, br)
    return pl.pallas_call(
        functools.partial(_even_kernel, ns=ns, sub=sub, seq=seq),
        grid=(nb, ns + 1),
        in_specs=[pl.BlockSpec((sub, d), row), pl.BlockSpec((sub, d), prev)]
        + [_const_spec(c.shape) for c in consts],
        out_specs=[pl.BlockSpec((sub, d), prev),
                   pl.BlockSpec((sub, d), prev),
                   pl.BlockSpec((sub, LANE), prev)],
        out_shape=[jax.ShapeDtypeStruct((t, d), F32),
                   jax.ShapeDtypeStruct((t, d), F32),
                   jax.ShapeDtypeStruct((t, LANE), F32)],
        scratch_shapes=[pltpu.VMEM((seq + HALO, CONV_DIM), F32),
                        pltpu.VMEM((seq, CONV_DIM + SGU_DIM), BF16)],
        compiler_params=pltpu.CompilerParams(
            dimension_semantics=("arbitrary", "arbitrary"), vmem_limit_bytes=VMEM_LIMIT),
        name="even_mixer",
    )(x, x, *consts)


def _odd_proj_kernel(x_ref, gmix_ref, win_ref, gcq_ref, wuq_ref, gckv_ref, wuk_ref, wuv_ref, vone_ref,
                     cos_ref, sa_ref, sb_ref, qm_ref, km_ref, vm_ref, qs_ref, ks_ref, vs_ref):
    h = _rms(x_ref[...], gmix_ref[...]).astype(BF16)
    p = _dot(h, win_ref[...])
    o = 0
    cq = p[:, o:o + MLA_Q_RANK]; o += MLA_Q_RANK
    ckv = p[:, o:o + MLA_KV_RANK]; o += MLA_KV_RANK
    kpe = p[:, o:o + LANE]; o += LANE
    qs = p[:, o:o + SWA_HEADS * SWA_DIM]; o += SWA_HEADS * SWA_DIM
    ks = p[:, o:o + LANE]; o += LANE
    vs = p[:, o:o + LANE]

    cos = cos_ref[...]
    sa = sa_ref[...]
    sb = sb_ref[...]

    def rope(z, reps):
        w = z.shape[1]
        c = jnp.concatenate([cos] * reps, axis=1) if reps > 1 else cos
        a = jnp.concatenate([sa] * reps, axis=1) if reps > 1 else sa
        b = jnp.concatenate([sb] * reps, axis=1) if reps > 1 else sb
        return z * c + pltpu.roll(z, w - MLA_ROPE // 2, 1) * a + pltpu.roll(z, MLA_ROPE // 2, 1) * b

    q = _dot(_rms(cq, gcq_ref[...]).astype(BF16), wuq_ref[...])
    qm_ref[...] = (rope(q, MLA_HEADS) * ((MLA_NOPE + MLA_ROPE) ** -0.5 * LOG2E)).astype(BF16)
    ckvn = _rms(ckv, gckv_ref[...]).astype(BF16)
    kr = rope(kpe, 1)
    km_ref[...] = (_dot(ckvn, wuk_ref[...]) + jnp.concatenate([kr] * MLA_HEADS, axis=1)).astype(BF16)
    vm_ref[...] = (_dot(ckvn, wuv_ref[...]) + vone_ref[...]).astype(BF16)
    qs_ref[...] = (qs * (SWA_DIM ** -0.5)).astype(BF16)
    ks_ref[...] = ks.astype(BF16)
    vs_ref[...] = vs.astype(BF16)


def _odd_proj(x, seq, gmix, win, gcq, wuq, gckv, wuk, wuv, vone, cos, sa, sb):
    t, d = x.shape
    tm = 512
    nsq = seq // tm
    row = lambda i: (i, 0)
    pos = lambda i: (i % nsq, 0)
    consts = (gmix, win, gcq, wuq, gckv, wuk, wuv, vone)
    widths = (MLA_HEADS * LANE, MLA_HEADS * LANE, MLA_HEADS * LANE, SWA_HEADS * SWA_DIM, LANE, LANE)
    return pl.pallas_call(
        _odd_proj_kernel,
        grid=(t // tm,),
        in_specs=[pl.BlockSpec((tm, d), row)] + [_const_spec(c.shape) for c in consts]
        + [pl.BlockSpec((tm, LANE), pos)] * 3,
        out_specs=[pl.BlockSpec((tm, w), row) for w in widths],
        out_shape=[jax.ShapeDtypeStruct((t, w), BF16) for w in widths],
        compiler_params=pltpu.CompilerParams(
            dimension_semantics=("arbitrary",), vmem_limit_bytes=VMEM_LIMIT),
        name="odd_proj",
    )(x, *consts, cos, sa, sb)


def _attn_kernel(sink_ref, x_ref, qm_ref, km_ref, vm_ref, qs_ref, ks_ref, vs_ref, wout_ref,
                 gffn_ref, wrh_ref, wrl_ref, br_ref, xo_ref, h3_ref, lg_ref, *, tq, seq):
    i = pl.program_id(1)
    qm = qm_ref[...]
    outs = []
    for hd in range(MLA_HEADS):
        cs = slice(hd * LANE, (hd + 1) * LANE)
        s = _dot_nt(qm[:, cs], km_ref[:, cs])
        e = jnp.exp2((s - jnp.max(s, axis=-1, keepdims=True)).astype(BF16))
        pv = _dot(e, vm_ref[:, cs])
        outs.append(pv[:, :MLA_V] / pv[:, MLA_V:MLA_V + 1])

    span = LANE + 2 * WINDOW
    rep = SWA_HEADS // SWA_KV_HEADS
    swa_rows = []
    for blk in range(tq // LANE):
        q0 = i * tq + blk * LANE
        start = pl.multiple_of(jnp.clip(q0 - WINDOW, 0, seq - span), LANE)
        kw = ks_ref[pl.ds(start, span), :]
        vw = vs_ref[pl.ds(start, span), :]
        qpos = q0 + lax.broadcasted_iota(I32, (LANE, span), 0)
        kpos = start + lax.broadcasted_iota(I32, (LANE, span), 1)
        absd = jnp.abs(qpos - kpos).astype(F32)
        in_win = absd <= float(WINDOW)
        heads = []
        for hd in range(SWA_HEADS):
            g = hd // rep
            q = qs_ref[blk * LANE:(blk + 1) * LANE, hd * SWA_DIM:(hd + 1) * SWA_DIM]
            s = _dot_nt(q, kw[:, g * SWA_DIM:(g + 1) * SWA_DIM])
            s = jnp.where(in_win, s - (2.0 ** -(hd + 1)) * absd, NEG_INF)
            sk = sink_ref[hd]
            m = jnp.maximum(jnp.max(s, axis=-1, keepdims=True), sk)
            e = jnp.exp(s - m)
            den = jnp.sum(e, axis=-1, keepdims=True) + jnp.exp(sk - m)
            heads.append(_dot(e.astype(BF16), vw[:, g * SWA_DIM:(g + 1) * SWA_DIM]) / den)
        swa_rows.append(jnp.concatenate(heads, axis=1))
    swa = jnp.concatenate(swa_rows, axis=0) if len(swa_rows) > 1 else swa_rows[0]

    mix = jnp.concatenate(outs + [swa], axis=1).astype(BF16)
    xn = x_ref[...] + _dot(mix, wout_ref[...])
    xo_ref[...] = xn
    _router_epilogue(xn, gffn_ref, wrh_ref, wrl_ref, br_ref, h3_ref, lg_ref)


def _attention(x, seq, sink, qm, km, vm, qs, ks, vs, wout, gffn, wrh, wrl, br):
    t, d = x.shape
    nb = t // seq
    tq = 512
    nq = seq // tq
    row = lambda b, i, *_: (b * nq + i, 0)
    bat = lambda b, i, *_: (b, 0)
    consts = (wout, gffn, wrh, wrl, br)
    grid_spec = pltpu.PrefetchScalarGridSpec(
        num_scalar_prefetch=1,
        grid=(nb, nq),
        in_specs=[pl.BlockSpec((tq, d), row),
                  pl.BlockSpec((tq, qm.shape[1]), row),
                  pl.BlockSpec((seq, km.shape[1]), bat),
                  pl.BlockSpec((seq, vm.shape[1]), bat),
                  pl.BlockSpec((tq, qs.shape[1]), row),
                  pl.BlockSpec((seq, ks.shape[1]), bat),
                  pl.BlockSpec((seq, vs.shape[1]), bat)]
        + [pl.BlockSpec(c.shape, lambda b, i, *_, n=len(c.shape): (0,) * n) for c in consts],
        out_specs=[pl.BlockSpec((tq, d), row),
                   pl.BlockSpec((tq, d), row),
                   pl.BlockSpec((tq, LANE), row)],
    )
    return pl.pallas_call(
        functools.partial(_attn_kernel, tq=tq, seq=seq),
        grid_spec=grid_spec,
        out_shape=[jax.ShapeDtypeStruct((t, d), F32),
                   jax.ShapeDtypeStruct((t, d), F32),
                   jax.ShapeDtypeStruct((t, LANE), F32)],
        compiler_params=pltpu.CompilerParams(
            dimension_semantics=("arbitrary", "arbitrary"), vmem_limit_bytes=VMEM_LIMIT),
        name="attention",
    )(sink, x, qm, km, vm, qs, ks, vs, *consts)


def _route_kernel(lg_ref, lower_ref, out_ref, cnt_ref, carry_ref):
    i = pl.program_id(0)
    rows = lg_ref.shape[0]
    lg = lg_ref[...]
    lane = lax.broadcasted_iota(I32, (rows, LANE), 1)
    ninf = -jnp.inf

    @pl.when(i == 0)
    def _():
        carry_ref[...] = jnp.zeros_like(carry_ref)

    gl = jnp.where(lane < N_GROUPS, lg, ninf)
    gmax = jnp.max(gl, axis=1, keepdims=True)
    gidx = jnp.min(jnp.where(gl == gmax, lane, LANE), axis=1, keepdims=True)
    gp = 1.0 / jnp.sum(jnp.exp(gl - gmax), axis=1, keepdims=True)

    first = N_GROUPS + gidx * N_EXP
    el = jnp.where((lane >= first) & (lane < first + N_EXP), lg, ninf)
    m1 = jnp.max(el, axis=1, keepdims=True)
    i1 = jnp.min(jnp.where(el == m1, lane, LANE), axis=1, keepdims=True)
    el2 = jnp.where(lane == i1, ninf, el)
    m2 = jnp.max(el2, axis=1, keepdims=True)
    i2 = jnp.min(jnp.where(el2 == m2, lane, LANE), axis=1, keepdims=True)
    e2 = jnp.exp(m2 - m1)
    w1 = gp / (1.0 + e2)
    w2 = gp * e2 / (1.0 + e2)
    id1 = i1 - N_GROUPS
    id2 = i2 - N_GROUPS
    onehot = jnp.where((lane == id1) | (lane == id2), 1.0, 0.0)

    before = _dot(lower_ref[...], onehot.astype(BF16)) + carry_ref[...]
    rank1 = jnp.sum(jnp.where(lane == id1, before, 0.0), axis=1, keepdims=True)
    rank2 = jnp.sum(jnp.where(lane == id2, before, 0.0), axis=1, keepdims=True)
    carry_ref[...] += jnp.sum(onehot, axis=0, keepdims=True)
    out = jnp.where(lane == 0, rank1, 0.0)
    out = jnp.where(lane == 1, rank2, out)
    out = jnp.where(lane == 2, w1, out)
    out = jnp.where(lane == 3, w2, out)
    out = jnp.where(lane == 4, id1.astype(F32), out)
    out = jnp.where(lane == 5, id2.astype(F32), out)
    out_ref[...] = out
    cnt_ref[...] = carry_ref[...]


def _route(lg):
    t = lg.shape[0]
    rows = 512
    lower = jnp.tri(rows, k=-1, dtype=BF16)
    return pl.pallas_call(
        _route_kernel,
        grid=(t // rows,),
        in_specs=[pl.BlockSpec((rows, LANE), lambda i: (i, 0)),
                  pl.BlockSpec((rows, rows), lambda i: (0, 0))],
        out_specs=[pl.BlockSpec((rows, LANE), lambda i: (i, 0)),
                   pl.BlockSpec((1, LANE), lambda i: (0, 0))],
        out_shape=[jax.ShapeDtypeStruct((t, LANE), F32), jax.ShapeDtypeStruct((1, LANE), F32)],
        scratch_shapes=[pltpu.VMEM((1, LANE), F32)],
        compiler_params=pltpu.CompilerParams(dimension_semantics=("arbitrary",)),
        name="moe_route",
    )(lg, lower)


def _row(ref, r):
    return ref.at[pl.ds(r, 1), :]


def _dispatch_kernel(pos_ref, h_ref, xs_ref, sem):
    i = pl.program_id(0)
    rows = h_ref.shape[0]

    for r in range(rows):
        t = i * rows + r
        pltpu.make_async_copy(_row(h_ref, r), _row(xs_ref, pos_ref[2 * t]), sem).start(priority=0)
        pltpu.make_async_copy(_row(h_ref, r), _row(xs_ref, pos_ref[2 * t + 1]), sem).start(priority=1)
    for _ in range(2):
        pltpu.make_async_copy(h_ref, xs_ref.at[pl.ds(0, rows), :], sem).wait()


def _dispatch(pos, h):
    t, d = h.shape
    grid_spec = pltpu.PrefetchScalarGridSpec(
        num_scalar_prefetch=1,
        grid=(t // DISPATCH_TILE,),
        in_specs=[pl.BlockSpec((DISPATCH_TILE, d), lambda i, *_: (i, 0))],
        out_specs=pl.BlockSpec(memory_space=pl.ANY),
        scratch_shapes=[pltpu.SemaphoreType.DMA],
    )
    return pl.pallas_call(
        _dispatch_kernel,
        grid_spec=grid_spec,
        out_shape=jax.ShapeDtypeStruct((2 * t, d), F32),
        compiler_params=pltpu.CompilerParams(dimension_semantics=("arbitrary",)),
        name="moe_dispatch",
    )(pos, h)


def _expert_kernel(tile_ref, exp_ref, lo_ref, hi_ref, xs_ref, wg_ref, wu_ref, wd_ref, ys_ref,
                   wg_s, wu_s, wd_s):
    i = pl.program_id(0)
    prev = jnp.maximum(i - 1, 0)
    new_tile = (i == 0) | (tile_ref[i] != tile_ref[prev])
    new_exp = (i == 0) | (exp_ref[i] != exp_ref[prev])
    lo = lo_ref[i]
    hi = hi_ref[i]
    rows = xs_ref.shape[0]

    @pl.when(new_exp)
    def _():
        wg_s[...] = wg_ref[0].astype(BF16)
        wu_s[...] = wu_ref[0].astype(BF16)
        wd_s[...] = wd_ref[0].astype(BF16)

    @pl.when(new_tile)
    def _():
        ys_ref[...] = jnp.zeros_like(ys_ref)

    @pl.when(hi > lo)
    def _():
        x = xs_ref[...].astype(BF16)
        g = _dot(x, wg_s[...])
        u = _dot(x, wu_s[...])
        hid = (g * _sigmoid(g) * u).astype(BF16)
        y = _dot(hid, wd_s[...])
        r = tile_ref[i] * rows + lax.broadcasted_iota(I32, (rows, 1), 0)
        ys_ref[...] += jnp.where((r >= lo) & (r < hi), y, 0.0)


def _experts(work, first_expert, xs, wg, wu, wd):
    n, d = xs.shape
    f = wg.shape[2]
    nwork = work[0].shape[0]
    wmap = lambda i, tl, ex, lo, hi: (first_expert + ex[i], 0, 0)
    grid_spec = pltpu.PrefetchScalarGridSpec(
        num_scalar_prefetch=4,
        grid=(nwork,),
        in_specs=[pl.BlockSpec((MOE_TILE, d), lambda i, tl, ex, lo, hi: (tl[i], 0)),
                  pl.BlockSpec((1, d, f), wmap),
                  pl.BlockSpec((1, d, f), wmap),
                  pl.BlockSpec((1, f, d), wmap)],
        out_specs=pl.BlockSpec((MOE_TILE, d), lambda i, tl, ex, lo, hi: (tl[i], 0)),
        scratch_shapes=[pltpu.VMEM((d, f), BF16), pltpu.VMEM((d, f), BF16), pltpu.VMEM((f, d), BF16)],
    )
    return pl.pallas_call(
        _expert_kernel,
        grid_spec=grid_spec,
        out_shape=jax.ShapeDtypeStruct((n, d), F32),
        compiler_params=pltpu.CompilerParams(
            dimension_semantics=("arbitrary",), vmem_limit_bytes=VMEM_LIMIT),
        name="moe_experts",
    )(*work, xs, wg, wu, wd)


def _combine_kernel(pos_ref, x_ref, rw_ref, gfin_ref, ys_ref, xo_ref, buf, sems, *, final):
    i = pl.program_id(0)
    n = pl.num_programs(0)
    rows = x_ref.shape[0]
    slot = i % 2

    def issue(step, slt):
        for r in range(rows):
            t = step * rows + r
            pltpu.make_async_copy(_row(ys_ref, pos_ref[2 * t]), _row(buf.at[slt], r),
                                  sems.at[slt]).start(priority=0)
            pltpu.make_async_copy(_row(ys_ref, pos_ref[2 * t + 1]), _row(buf.at[slt], rows + r),
                                  sems.at[slt]).start(priority=1)

    @pl.when(i == 0)
    def _():
        issue(0, 0)

    for slt in range(2):
        @pl.when((i + 1 < n) & (slot == 1 - slt))
        def _():
            issue(i + 1, slt)

    pltpu.make_async_copy(ys_ref.at[pl.ds(0, 2 * rows), :], buf.at[slot], sems.at[slot]).wait()

    rw = rw_ref[...]
    xn = x_ref[...] + rw[:, 2:3] * buf[slot, 0:rows, :] + rw[:, 3:4] * buf[slot, rows:2 * rows, :]
    if final:
        xn = _rms(xn, gfin_ref[...])
    xo_ref[...] = xn


def _combine(pos, x, rw, gfin, ys, final):
    t, d = x.shape
    grid_spec = pltpu.PrefetchScalarGridSpec(
        num_scalar_prefetch=1,
        grid=(t // ROW_TILE,),
        in_specs=[pl.BlockSpec((ROW_TILE, d), lambda i, *_: (i, 0)),
                  pl.BlockSpec((ROW_TILE, LANE), lambda i, *_: (i, 0)),
                  pl.BlockSpec((1, d), lambda i, *_: (0, 0)),
                  pl.BlockSpec(memory_space=pl.ANY)],
        out_specs=pl.BlockSpec((ROW_TILE, d), lambda i, *_: (i, 0)),
        scratch_shapes=[pltpu.VMEM((2, 2 * ROW_TILE, d), F32), pltpu.SemaphoreType.DMA((2,))],
    )
    return pl.pallas_call(
        functools.partial(_combine_kernel, final=final),
        grid_spec=grid_spec,
        out_shape=jax.ShapeDtypeStruct((t, d), F32),
        compiler_params=pltpu.CompilerParams(
            dimension_semantics=("arbitrary",), vmem_limit_bytes=VMEM_LIMIT),
        name="moe_combine",
    )(pos, x, rw, gfin, ys)


def _worklist(cnt, n_rows):
    ends = jnp.cumsum(cnt)
    starts = ends - cnt
    n_tiles = n_rows // MOE_TILE
    tile_lo = jnp.arange(n_tiles, dtype=I32) * MOE_TILE
    exp_lo = starts[1:]
    rank_t = jnp.arange(n_tiles, dtype=I32) + jnp.sum(exp_lo[None, :] < tile_lo[:, None], axis=1)
    rank_e = jnp.arange(N_EXPERTS - 1, dtype=I32) + jnp.sum(tile_lo[None, :] <= exp_lo[:, None], axis=1)
    vals = jnp.concatenate([tile_lo, exp_lo])
    ranks = jnp.concatenate([rank_t, rank_e])
    slot = jnp.arange(vals.shape[0], dtype=I32)
    lo = jnp.sum(jnp.where(ranks[None, :] == slot[:, None], vals[None, :], 0), axis=1)
    hi = jnp.concatenate([lo[1:], jnp.full((1,), n_rows, I32)])
    tile = jnp.minimum(lo // MOE_TILE, n_tiles - 1)
    expert = jnp.minimum(jnp.sum(ends[None, :] <= lo[:, None], axis=1), N_EXPERTS - 1).astype(I32)
    return tile, expert, lo, hi


def _moe(x, h, lg, wg, wu, wd, layer, gfin, final):
    t = x.shape[0]
    rw, cnt = _route(lg)
    cnt = cnt[0, :N_EXPERTS].astype(I32)
    starts = jnp.cumsum(cnt) - cnt
    ids = rw[:, 4:6].astype(I32)
    offs = jnp.sum(jnp.where(ids[:, :, None] == jnp.arange(N_EXPERTS, dtype=I32), starts, 0), axis=-1)
    pos = (offs + rw[:, 0:2].astype(I32)).reshape(2 * t)
    xs = _dispatch(pos, h)
    work = _worklist(cnt, 2 * t)
    ys = _experts(work, layer * N_EXPERTS, xs, wg, wu, wd)
    return _combine(pos, x, rw, gfin, ys, final)


def _router_weights(w_group, b_group, w_router, b_router):
    d = w_group.shape[0]
    w = jnp.concatenate([w_group] + [w_router[g] for g in range(N_GROUPS)], axis=1)
    w = jnp.pad(w, ((0, 0), (0, LANE - w.shape[1])))
    b = jnp.concatenate([b_group, b_router.reshape(-1)])
    b = jnp.pad(b, (0, LANE - b.shape[0])).reshape(1, LANE)
    hi = w.astype(BF16)
    lo = (w - hi.astype(F32)).astype(BF16)
    return hi, lo, b


def _rope_tables(seq):
    half = MLA_ROPE // 2
    pos = jnp.arange(seq, dtype=F32)
    inv = 1.0 / (ROPE_THETA ** (jnp.arange(0, MLA_ROPE, 2, dtype=F32) / MLA_ROPE))
    ang = pos[:, None] * inv[None, :]
    cos, sin = jnp.cos(ang), jnp.sin(ang)
    z = lambda n: jnp.zeros((seq, n), F32)
    tail = LANE - MLA_NOPE - MLA_ROPE
    ctab = jnp.concatenate([jnp.ones((seq, MLA_NOPE), F32), cos, cos, z(tail)], axis=1)
    atab = jnp.concatenate([z(MLA_NOPE), -sin, z(half), z(tail)], axis=1)
    btab = jnp.concatenate([z(MLA_NOPE), z(half), sin, z(tail)], axis=1)
    return ctab, atab, btab


def _odd_weights(w_in, w_uq, w_ukv):
    d = w_in.shape[0]
    sizes = (MLA_Q_RANK, MLA_KV_RANK, MLA_ROPE, SWA_HEADS * SWA_DIM, SWA_KV_HEADS * SWA_DIM,
             SWA_KV_HEADS * SWA_DIM)
    offs = [0]
    for s in sizes:
        offs.append(offs[-1] + s)
    cq, ckv, kpe, qs, ks, vs = [w_in[:, offs[k]:offs[k + 1]] for k in range(6)]
    kpe = jnp.pad(kpe, ((0, 0), (MLA_NOPE, LANE - MLA_NOPE - MLA_ROPE)))
    win = jnp.concatenate([cq, ckv, kpe, qs, ks, vs], axis=1).astype(BF16)
    hq = MLA_NOPE + MLA_ROPE
    wuq = jnp.pad(w_uq.reshape(MLA_Q_RANK, MLA_HEADS, hq), ((0, 0), (0, 0), (0, LANE - hq)))
    wuq = wuq.reshape(MLA_Q_RANK, MLA_HEADS * LANE).astype(BF16)
    wkv = w_ukv.reshape(MLA_KV_RANK, MLA_HEADS, MLA_NOPE + MLA_V)
    wuk = jnp.pad(wkv[:, :, :MLA_NOPE], ((0, 0), (0, 0), (0, LANE - MLA_NOPE)))
    wuk = wuk.reshape(MLA_KV_RANK, MLA_HEADS * LANE).astype(BF16)
    wuv = jnp.pad(wkv[:, :, MLA_NOPE:], ((0, 0), (0, 0), (0, LANE - MLA_V)))
    wuv = wuv.reshape(MLA_KV_RANK, MLA_HEADS * LANE).astype(BF16)
    vone = jnp.tile(jnp.zeros((LANE,), F32).at[MLA_V].set(1.0), MLA_HEADS).reshape(1, MLA_HEADS * LANE)
    return win, wuq, wuk, wuv, vone


def kernel(x, norm_mix, norm_ffn, norm_final, ev_w_in, ev_conv_w, ev_conv_b, ev_cnorm_g, ev_cnorm_b,
           ev_vnorm_g, ev_vnorm_b, ev_w_sp, ev_b_sp, ev_w_out, od_w_in, od_g_cq, od_w_uq, od_g_ckv,
           od_w_ukv, od_sink, od_w_out, moe_w_group, moe_b_group, moe_w_router, moe_b_router,
           moe_w_gate, moe_w_up, moe_w_down):
    b, seq, d = x.shape
    t = b * seq
    depth = norm_mix.shape[0]
    row = lambda v: v.reshape(1, -1)
    xt = x.reshape(t, d)
    gfin = row(norm_final)
    wg_all = moe_w_gate.reshape(depth * N_EXPERTS, d, -1)
    wu_all = moe_w_up.reshape(depth * N_EXPERTS, d, -1)
    wd_all = moe_w_down.reshape(depth * N_EXPERTS, -1, d)
    for layer in range(depth):
        k = layer // 2
        wrh, wrl, br = _router_weights(moe_w_group[layer], moe_b_group[layer], moe_w_router[layer],
                                       moe_b_router[layer])
        gffn = row(norm_ffn[layer])
        if layer % 2 == 0:
            bsp = jnp.broadcast_to(ev_b_sp[k][:, :, None], ev_w_sp[k].shape)
            xt, h3, lg = _even_mixer(
                xt, seq, row(norm_mix[layer]), ev_w_in[k].astype(BF16), ev_conv_w[k], row(ev_conv_b[k]),
                row(ev_cnorm_g[k]), row(ev_cnorm_b[k]), row(ev_vnorm_g[k]), row(ev_vnorm_b[k]),
                ev_w_sp[k].astype(BF16), bsp, ev_w_out[k].astype(BF16), gffn, wrh, wrl, br)
        else:
            win, wuq, wuk, wuv, vone = _odd_weights(od_w_in[k], od_w_uq[k], od_w_ukv[k])
            ctab, atab, btab = _rope_tables(seq)
            qm, km, vm, qs, ks, vs = _odd_proj(
                xt, seq, row(norm_mix[layer]), win, row(od_g_cq[k]), wuq, row(od_g_ckv[k]), wuk, wuv,
                vone, ctab, atab, btab)
            xt, h3, lg = _attention(xt, seq, od_sink[k], qm, km, vm, qs, ks, vs,
                                    od_w_out[k].astype(BF16), gffn, wrh, wrl, br)
        xt = _moe(xt, h3, lg, wg_all, wu_all, wd_all, layer, gfin, layer == depth - 1)
    return xt.reshape(b, seq, d)
```

```python
import functools

import jax
import jax.numpy as jnp
from jax import lax
from jax.experimental import pallas as pl
from jax.experimental.pallas import tpu as pltpu

F32 = jnp.float32
BF16 = jnp.bfloat16
I32 = jnp.int32

EPS = 1e-6
NEG_INF = -1e30
LOG2E = 1.4426950408889634
LANE = 128
VMEM_LIMIT = 56 * 1024 * 1024

CONV_DIM = 512
CONV_GROUP = 128
CONV_WIDTH = 31
SGU_DIM = 512
SGU_HEAD = 128
CHUNK = 128
MLA_HEADS = 8
MLA_Q_RANK = 256
MLA_KV_RANK = 128
MLA_NOPE = 64
MLA_ROPE = 32
MLA_V = 64
ROPE_THETA = 10000.0
SWA_HEADS = 8
SWA_KV_HEADS = 2
SWA_DIM = 64
WINDOW = 128
N_GROUPS = 4
N_EXP = 8
N_EXPERTS = N_GROUPS * N_EXP

HALO = 16
CONV_ROWS = 64
MOE_TILE = 256
ROW_TILE = 256


def _dot(a, b):
    return jnp.dot(a, b, preferred_element_type=F32)


def _dot_nt(a, b):
    return lax.dot_general(a, b, (((1,), (1,)), ((), ())), preferred_element_type=F32)


def _rms(x, g):
    return x * lax.rsqrt(jnp.mean(x * x, axis=-1, keepdims=True) + EPS) * g


def _ln(x, g, b):
    mu = jnp.mean(x, axis=-1, keepdims=True)
    d = x - mu
    var = jnp.mean(d * d, axis=-1, keepdims=True)
    return d * lax.rsqrt(var + EPS) * g + b


def _gelu(x):
    return 0.5 * x * (1.0 + jnp.tanh(0.7978845608028654 * (x + 0.044715 * (x * x * x))))


def _sigmoid(x):
    return 1.0 / (1.0 + jnp.exp(-x))


def _router_epilogue(xn, gffn_ref, wrh_ref, wrl_ref, br_ref, h3_ref, lg_ref):
    h = _rms(xn, gffn_ref[...])
    h3_ref[...] = h
    hi = h.astype(BF16)
    lo = (h - hi.astype(F32)).astype(BF16)
    wh = wrh_ref[...]
    lg_ref[...] = _dot(hi, wh) + _dot(lo, wh) + _dot(hi, wrl_ref[...]) + br_ref[...]


def _even_kernel(xc_ref, xp_ref, gmix_ref, win_ref, cw_ref, cb_ref, cng_ref, cnb_ref, vng_ref,
                 vnb_ref, wsp_ref, bsp_ref, wout_ref, gffn_ref, wrh_ref, wrl_ref, br_ref,
                 xo_ref, h3_ref, lg_ref, a_s, mix_s, *, ns, sub, seq):
    j = pl.program_id(1)
    win_rows = CONV_ROWS + 2 * HALO
    n_chunks = sub // CONV_ROWS

    def proj_conv_input():
        h = _rms(xc_ref[...], gmix_ref[...]).astype(BF16)
        pa = _dot(h, win_ref[:, 0:2 * CONV_DIM])
        return h, pa[:, :CONV_DIM] * _sigmoid(pa[:, CONV_DIM:])

    def proj_gating(h, a):
        r0 = pl.multiple_of(j * sub, sub)
        a_s[pl.ds(HALO + r0, sub), :] = a
        u = _gelu(_dot(h, win_ref[:, 2 * CONV_DIM:2 * CONV_DIM + SGU_DIM]))
        v = _gelu(_dot(h, win_ref[:, 2 * CONV_DIM + SGU_DIM:]))
        for hd in range(SGU_DIM // SGU_HEAD):
            cs = slice(hd * SGU_HEAD, (hd + 1) * SGU_HEAD)
            vn = _ln(v[:, cs], vng_ref[:, cs], vnb_ref[:, cs]).astype(BF16)
            for c in range(sub // CHUNK):
                rs = slice(c * CHUNK, (c + 1) * CHUNK)
                sv = _dot(wsp_ref[hd], vn[rs, :]) + bsp_ref[hd]
                mix_s[pl.ds(r0 + c * CHUNK, CHUNK),
                      CONV_DIM + hd * SGU_HEAD:CONV_DIM + (hd + 1) * SGU_HEAD] = (u[rs, cs] * sv).astype(BF16)

    def conv_and_out(next_halo):
        r0 = (j - 1) * sub
        for i in range(n_chunks):
            base = pl.multiple_of(r0 + i * CONV_ROWS, CONV_ROWS)
            if i < n_chunks - 1:
                win = a_s[pl.ds(base, win_rows), :]
            else:
                win = jnp.concatenate([a_s[pl.ds(base, CONV_ROWS + HALO), :], next_halo], axis=0)
            acc = jnp.zeros((CONV_ROWS, CONV_DIM), F32)
            for s in range(8):
                sh = win if s == 0 else pltpu.roll(win, win_rows - s, 0)
                for m in range(4):
                    k = 8 * m + s - 1
                    if 0 <= k < CONV_WIDTH:
                        acc = acc + sh[8 * m:8 * m + CONV_ROWS, :] * cw_ref[k:k + 1, :]
            acc = acc + cb_ref[...]
            for g in range(CONV_DIM // CONV_GROUP):
                cs = slice(g * CONV_GROUP, (g + 1) * CONV_GROUP)
                y = _ln(acc[:, cs], cng_ref[:, cs], cnb_ref[:, cs])
                mix_s[pl.ds(base, CONV_ROWS), cs] = (y * _sigmoid(y)).astype(BF16)
        mix = mix_s[pl.ds(pl.multiple_of(r0, sub), sub), :]
        xn = xp_ref[...] + _dot(mix, wout_ref[...])
        xo_ref[...] = xn
        _router_epilogue(xn, gffn_ref, wrh_ref, wrl_ref, br_ref, h3_ref, lg_ref)

    @pl.when(j == 0)
    def _():
        a_s[0:HALO, :] = jnp.zeros((HALO, CONV_DIM), F32)
        h, a = proj_conv_input()
        proj_gating(h, a)

    @pl.when((j >= 1) & (j < ns))
    def _():
        h, a = proj_conv_input()
        conv_and_out(a[0:HALO, :])
        proj_gating(h, a)

    @pl.when(j == ns)
    def _():
        conv_and_out(jnp.zeros((HALO, CONV_DIM), F32))


def _const_spec(shape):
    nd = len(shape)
    return pl.BlockSpec(shape, lambda *_: (0,) * nd)


def _even_mixer(x, seq, gmix, win, cw, cb, cng, cnb, vng, vnb, wsp, bsp, wout, gffn, wrh, wrl, br):
    t, d = x.shape
    nb = t // seq
    sub = 512
    ns = seq // sub
    row = lambda b, j: (b * ns + jnp.minimum(j, ns - 1), 0)
    prev = lambda b, j: (b * ns + jnp.maximum(j - 1, 0), 0)
    consts = (gmix, win, cw, cb, cng, cnb, vng, vnb, wsp, bsp, wout, gffn, wrh, wrl, br)
    return pl.pallas_call(
        functools.partial(_even_kernel, ns=ns, sub=sub, seq=seq),
        grid=(nb, ns + 1),
        in_specs=[pl.BlockSpec((sub, d), row), pl.BlockSpec((sub, d), prev)]
        + [_const_spec(c.shape) for c in consts],
        out_specs=[pl.BlockSpec((sub, d), prev),
                   pl.BlockSpec((sub, d), prev),
                   pl.BlockSpec((sub, LANE), prev)],
        out_shape=[jax.ShapeDtypeStruct((t, d), F32),
                   jax.ShapeDtypeStruct((t, d), F32),
                   jax.ShapeDtypeStruct((t, LANE), F32)],
        scratch_shapes=[pltpu.VMEM((seq + HALO, CONV_DIM), F32),
                        pltpu.VMEM((seq, CONV_DIM + SGU_DIM), BF16)],
        compiler_params=pltpu.CompilerParams(
            dimension_semantics=("arbitrary", "arbitrary"), vmem_limit_bytes=VMEM_LIMIT),
        name="even_mixer",
    )(x, x, *consts)


def _odd_proj_kernel(x_ref, gmix_ref, win_ref, gcq_ref, wuq_ref, gckv_ref, wuk_ref, wuv_ref, vone_ref,
                     cos_ref, sa_ref, sb_ref, qm_ref, km_ref, vm_ref, qs_ref, ks_ref, vs_ref):
    h = _rms(x_ref[...], gmix_ref[...]).astype(BF16)
    p = _dot(h, win_ref[...])
    o = 0
    cq = p[:, o:o + MLA_Q_RANK]; o += MLA_Q_RANK
    ckv = p[:, o:o + MLA_KV_RANK]; o += MLA_KV_RANK
    kpe = p[:, o:o + LANE]; o += LANE
    qs = p[:, o:o + SWA_HEADS * SWA_DIM]; o += SWA_HEADS * SWA_DIM
    ks = p[:, o:o + LANE]; o += LANE
    vs = p[:, o:o + LANE]

    cos = cos_ref[...]
    sa = sa_ref[...]
    sb = sb_ref[...]

    def rope(z, reps):
        w = z.shape[1]
        c = jnp.concatenate([cos] * reps, axis=1) if reps > 1 else cos
        a = jnp.concatenate([sa] * reps, axis=1) if reps > 1 else sa
        b = jnp.concatenate([sb] * reps, axis=1) if reps > 1 else sb
        return z * c + pltpu.roll(z, w - MLA_ROPE // 2, 1) * a + pltpu.roll(z, MLA_ROPE // 2, 1) * b

    q = _dot(_rms(cq, gcq_ref[...]).astype(BF16), wuq_ref[...])
    qm_ref[...] = (rope(q, MLA_HEADS) * ((MLA_NOPE + MLA_ROPE) ** -0.5 * LOG2E)).astype(BF16)
    ckvn = _rms(ckv, gckv_ref[...]).astype(BF16)
    kr = rope(kpe, 1)
    km_ref[...] = (_dot(ckvn, wuk_ref[...]) + jnp.concatenate([kr] * MLA_HEADS, axis=1)).astype(BF16)
    vm_ref[...] = (_dot(ckvn, wuv_ref[...]) + vone_ref[...]).astype(BF16)
    qs_ref[...] = (qs * (SWA_DIM ** -0.5)).astype(BF16)
    ks_ref[...] = ks.astype(BF16)
    vs_ref[...] = vs.astype(BF16)


def _odd_proj(x, seq, gmix, win, gcq, wuq, gckv, wuk, wuv, vone, cos, sa, sb):
    t, d = x.shape
    tm = 512
    nsq = seq // tm
    row = lambda i: (i, 0)
    pos = lambda i: (i % nsq, 0)
    consts = (gmix, win, gcq, wuq, gckv, wuk, wuv, vone)
    widths = (MLA_HEADS * LANE, MLA_HEADS * LANE, MLA_HEADS * LANE, SWA_HEADS * SWA_DIM, LANE, LANE)
    return pl.pallas_call(
        _odd_proj_kernel,
        grid=(t // tm,),
        in_specs=[pl.BlockSpec((tm, d), row)] + [_const_spec(c.shape) for c in consts]
        + [pl.BlockSpec((tm, LANE), pos)] * 3,
        out_specs=[pl.BlockSpec((tm, w), row) for w in widths],
        out_shape=[jax.ShapeDtypeStruct((t, w), BF16) for w in widths],
        compiler_params=pltpu.CompilerParams(
            dimension_semantics=("arbitrary",), vmem_limit_bytes=VMEM_LIMIT),
        name="odd_proj",
    )(x, *consts, cos, sa, sb)


def _attn_kernel(sink_ref, x_ref, qm_ref, km_ref, vm_ref, qs_ref, ks_ref, vs_ref, wout_ref,
                 gffn_ref, wrh_ref, wrl_ref, br_ref, xo_ref, h3_ref, lg_ref, *, tq, seq):
    i = pl.program_id(1)
    qm = qm_ref[...]
    outs = []
    for hd in range(MLA_HEADS):
        cs = slice(hd * LANE, (hd + 1) * LANE)
        s = _dot_nt(qm[:, cs], km_ref[:, cs])
        e = jnp.exp2((s - jnp.max(s, axis=-1, keepdims=True)).astype(BF16))
        pv = _dot(e, vm_ref[:, cs])
        outs.append(pv[:, :MLA_V] / pv[:, MLA_V:MLA_V + 1])

    span = LANE + 2 * WINDOW
    rep = SWA_HEADS // SWA_KV_HEADS
    swa_rows = []
    for blk in range(tq // LANE):
        q0 = i * tq + blk * LANE
        start = pl.multiple_of(jnp.clip(q0 - WINDOW, 0, seq - span), LANE)
        kw = ks_ref[pl.ds(start, span), :]
        vw = vs_ref[pl.ds(start, span), :]
        qpos = q0 + lax.broadcasted_iota(I32, (LANE, span), 0)
        kpos = start + lax.broadcasted_iota(I32, (LANE, span), 1)
        absd = jnp.abs(qpos - kpos).astype(F32)
        in_win = absd <= float(WINDOW)
        heads = []
        for hd in range(SWA_HEADS):
            g = hd // rep
            q = qs_ref[blk * LANE:(blk + 1) * LANE, hd * SWA_DIM:(hd + 1) * SWA_DIM]
            s = _dot_nt(q, kw[:, g * SWA_DIM:(g + 1) * SWA_DIM])
            s = jnp.where(in_win, s - (2.0 ** -(hd + 1)) * absd, NEG_INF)
            sk = sink_ref[hd]
            m = jnp.maximum(jnp.max(s, axis=-1, keepdims=True), sk)
            e = jnp.exp(s - m)
            den = jnp.sum(e, axis=-1, keepdims=True) + jnp.exp(sk - m)
            heads.append(_dot(e.astype(BF16), vw[:, g * SWA_DIM:(g + 1) * SWA_DIM]) / den)
        swa_rows.append(jnp.concatenate(heads, axis=1))
    swa = jnp.concatenate(swa_rows, axis=0) if len(swa_rows) > 1 else swa_rows[0]

    mix = jnp.concatenate(outs + [swa], axis=1).astype(BF16)
    xn = x_ref[...] + _dot(mix, wout_ref[...])
    xo_ref[...] = xn
    _router_epilogue(xn, gffn_ref, wrh_ref, wrl_ref, br_ref, h3_ref, lg_ref)


def _attention(x, seq, sink, qm, km, vm, qs, ks, vs, wout, gffn, wrh, wrl, br):
    t, d = x.shape
    nb = t // seq
    tq = 512
    nq = seq // tq
    row = lambda b, i, *_: (b * nq + i, 0)
    bat = lambda b, i, *_: (b, 0)
    consts = (wout, gffn, wrh, wrl, br)
    grid_spec = pltpu.PrefetchScalarGridSpec(
        num_scalar_prefetch=1,
        grid=(nb, nq),
        in_specs=[pl.BlockSpec((tq, d), row),
                  pl.BlockSpec((tq, qm.shape[1]), row),
                  pl.BlockSpec((seq, km.shape[1]), bat),
                  pl.BlockSpec((seq, vm.shape[1]), bat),
                  pl.BlockSpec((tq, qs.shape[1]), row),
                  pl.BlockSpec((seq, ks.shape[1]), bat),
                  pl.BlockSpec((seq, vs.shape[1]), bat)]
        + [pl.BlockSpec(c.shape, lambda b, i, *_, n=len(c.shape): (0,) * n) for c in consts],
        out_specs=[pl.BlockSpec((tq, d), row),
                   pl.BlockSpec((tq, d), row),
                   pl.BlockSpec((tq, LANE), row)],
    )
    return pl.pallas_call(
        functools.partial(_attn_kernel, tq=tq, seq=seq),
        grid_spec=grid_spec,
        out_shape=[jax.ShapeDtypeStruct((t, d), F32),
                   jax.ShapeDtypeStruct((t, d), F32),
                   jax.ShapeDtypeStruct((t, LANE), F32)],
        compiler_params=pltpu.CompilerParams(
            dimension_semantics=("arbitrary", "arbitrary"), vmem_limit_bytes=VMEM_LIMIT),
        name="attention",
    )(sink, x, qm, km, vm, qs, ks, vs, *consts)


def _route_kernel(lg_ref, lower_ref, out_ref, cnt_ref, carry_ref):
    i = pl.program_id(0)
    rows = lg_ref.shape[0]
    lg = lg_ref[...]
    lane = lax.broadcasted_iota(I32, (rows, LANE), 1)
    ninf = -jnp.inf

    @pl.when(i == 0)
    def _():
        carry_ref[...] = jnp.zeros_like(carry_ref)

    gl = jnp.where(lane < N_GROUPS, lg, ninf)
    gmax = jnp.max(gl, axis=1, keepdims=True)
    gidx = jnp.min(jnp.where(gl == gmax, lane, LANE), axis=1, keepdims=True)
    gp = 1.0 / jnp.sum(jnp.exp(gl - gmax), axis=1, keepdims=True)

    first = N_GROUPS + gidx * N_EXP
    el = jnp.where((lane >= first) & (lane < first + N_EXP), lg, ninf)
    m1 = jnp.max(el, axis=1, keepdims=True)
    i1 = jnp.min(jnp.where(el == m1, lane, LANE), axis=1, keepdims=True)
    el2 = jnp.where(lane == i1, ninf, el)
    m2 = jnp.max(el2, axis=1, keepdims=True)
    i2 = jnp.min(jnp.where(el2 == m2, lane, LANE), axis=1, keepdims=True)
    e2 = jnp.exp(m2 - m1)
    w1 = gp / (1.0 + e2)
    w2 = gp * e2 / (1.0 + e2)
    id1 = i1 - N_GROUPS
    id2 = i2 - N_GROUPS
    onehot = jnp.where((lane == id1) | (lane == id2), 1.0, 0.0)

    before = _dot(lower_ref[...], onehot.astype(BF16)) + carry_ref[...]
    rank1 = jnp.sum(jnp.where(lane == id1, before, 0.0), axis=1, keepdims=True)
    rank2 = jnp.sum(jnp.where(lane == id2, before, 0.0), axis=1, keepdims=True)
    carry_ref[...] += jnp.sum(onehot, axis=0, keepdims=True)
    out = jnp.where(lane == 0, rank1, 0.0)
    out = jnp.where(lane == 1, rank2, out)
    out = jnp.where(lane == 2, w1, out)
    out = jnp.where(lane == 3, w2, out)
    out = jnp.where(lane == 4, id1.astype(F32), out)
    out = jnp.where(lane == 5, id2.astype(F32), out)
    out_ref[...] = out
    cnt_ref[...] = carry_ref[...]


def _route(lg):
    t = lg.shape[0]
    rows = 512
    lower = jnp.tri(rows, k=-1, dtype=BF16)
    return pl.pallas_call(
        _route_kernel,
        grid=(t // rows,),
        in_specs=[pl.BlockSpec((rows, LANE), lambda i: (i, 0)),
                  pl.BlockSpec((rows, rows), lambda i: (0, 0))],
        out_specs=[pl.BlockSpec((rows, LANE), lambda i: (i, 0)),
                   pl.BlockSpec((1, LANE), lambda i: (0, 0))],
        out_shape=[jax.ShapeDtypeStruct((t, LANE), F32), jax.ShapeDtypeStruct((1, LANE), F32)],
        scratch_shapes=[pltpu.VMEM((1, LANE), F32)],
        compiler_params=pltpu.CompilerParams(dimension_semantics=("arbitrary",)),
        name="moe_route",
    )(lg, lower)


def _row(ref, r):
    return ref.at[pl.ds(r, 1), :]


def _expert_kernel(tile_ref, exp_ref, lo_ref, hi_ref, inv_ref, h_ref, wg_ref, wu_ref, wd_ref, ys_ref,
                   xbuf, sems, wg_s, wu_s, wd_s, *, n_tiles):
    i = pl.program_id(0)
    prev = jnp.maximum(i - 1, 0)
    tile = tile_ref[i]
    new_tile = (i == 0) | (tile != tile_ref[prev])
    new_exp = (i == 0) | (exp_ref[i] != exp_ref[prev])
    lo = lo_ref[i]
    hi = hi_ref[i]
    rows = ys_ref.shape[0]
    slot = tile % 2

    def gather(tl, slt):
        for r in range(rows):
            pltpu.make_async_copy(_row(h_ref, inv_ref[tl * rows + r]), _row(xbuf.at[slt], r),
                                  sems.at[slt]).start(priority=r % 2)

    @pl.when(i == 0)
    def _():
        gather(tile, 0)

    for slt in range(2):
        @pl.when(new_tile & (slot == slt))
        def _():
            pltpu.make_async_copy(h_ref.at[pl.ds(0, rows), :], xbuf.at[slt], sems.at[slt]).wait()

            @pl.when(tile + 1 < n_tiles)
            def _():
                gather(tile + 1, 1 - slt)

    @pl.when(new_exp)
    def _():
        wg_s[...] = wg_ref[0].astype(BF16)
        wu_s[...] = wu_ref[0].astype(BF16)
        wd_s[...] = wd_ref[0].astype(BF16)

    @pl.when(new_tile)
    def _():
        ys_ref[...] = jnp.zeros_like(ys_ref)

    @pl.when(hi > lo)
    def _():
        x = xbuf[slot].astype(BF16)
        g = _dot(x, wg_s[...])
        u = _dot(x, wu_s[...])
        hid = (g * _sigmoid(g) * u).astype(BF16)
        y = _dot(hid, wd_s[...])
        r = tile * rows + lax.broadcasted_iota(I32, (rows, 1), 0)
        ys_ref[...] += jnp.where((r >= lo) & (r < hi), y, 0.0)


def _experts(work, inv, first_expert, h, wg, wu, wd):
    t, d = h.shape
    n = inv.shape[0]
    f = wg.shape[2]
    nwork = work[0].shape[0]
    wmap = lambda i, tl, ex, lo, hi, iv: (first_expert + ex[i], 0, 0)
    grid_spec = pltpu.PrefetchScalarGridSpec(
        num_scalar_prefetch=5,
        grid=(nwork,),
        in_specs=[pl.BlockSpec(memory_space=pl.ANY),
                  pl.BlockSpec((1, d, f), wmap),
                  pl.BlockSpec((1, d, f), wmap),
                  pl.BlockSpec((1, f, d), wmap)],
        out_specs=pl.BlockSpec((MOE_TILE, d), lambda i, tl, ex, lo, hi, iv: (tl[i], 0)),
        scratch_shapes=[pltpu.VMEM((2, MOE_TILE, d), F32), pltpu.SemaphoreType.DMA((2,)),
                        pltpu.VMEM((d, f), BF16), pltpu.VMEM((d, f), BF16), pltpu.VMEM((f, d), BF16)],
    )
    return pl.pallas_call(
        functools.partial(_expert_kernel, n_tiles=n // MOE_TILE),
        grid_spec=grid_spec,
        out_shape=jax.ShapeDtypeStruct((n, d), F32),
        compiler_params=pltpu.CompilerParams(
            dimension_semantics=("arbitrary",), vmem_limit_bytes=VMEM_LIMIT),
        name="moe_experts",
    )(*work, inv, h, wg, wu, wd)


def _combine_kernel(pos_ref, x_ref, rw_ref, gfin_ref, ys_ref, xo_ref, buf, sems, *, final):
    i = pl.program_id(0)
    n = pl.num_programs(0)
    rows = x_ref.shape[0]
    slot = i % 2

    def issue(step, slt):
        for r in range(rows):
            t = step * rows + r
            pltpu.make_async_copy(_row(ys_ref, pos_ref[2 * t]), _row(buf.at[slt], r),
                                  sems.at[slt]).start(priority=0)
            pltpu.make_async_copy(_row(ys_ref, pos_ref[2 * t + 1]), _row(buf.at[slt], rows + r),
                                  sems.at[slt]).start(priority=1)

    @pl.when(i == 0)
    def _():
        issue(0, 0)

    for slt in range(2):
        @pl.when((i + 1 < n) & (slot == 1 - slt))
        def _():
            issue(i + 1, slt)

    pltpu.make_async_copy(ys_ref.at[pl.ds(0, 2 * rows), :], buf.at[slot], sems.at[slot]).wait()

    rw = rw_ref[...]
    xn = x_ref[...] + rw[:, 2:3] * buf[slot, 0:rows, :] + rw[:, 3:4] * buf[slot, rows:2 * rows, :]
    if final:
        xn = _rms(xn, gfin_ref[...])
    xo_ref[...] = xn


def _combine(pos, x, rw, gfin, ys, final):
    t, d = x.shape
    grid_spec = pltpu.PrefetchScalarGridSpec(
        num_scalar_prefetch=1,
        grid=(t // ROW_TILE,),
        in_specs=[pl.BlockSpec((ROW_TILE, d), lambda i, *_: (i, 0)),
                  pl.BlockSpec((ROW_TILE, LANE), lambda i, *_: (i, 0)),
                  pl.BlockSpec((1, d), lambda i, *_: (0, 0)),
                  pl.BlockSpec(memory_space=pl.ANY)],
        out_specs=pl.BlockSpec((ROW_TILE, d), lambda i, *_: (i, 0)),
        scratch_shapes=[pltpu.VMEM((2, 2 * ROW_TILE, d), F32), pltpu.SemaphoreType.DMA((2,))],
    )
    return pl.pallas_call(
        functools.partial(_combine_kernel, final=final),
        grid_spec=grid_spec,
        out_shape=jax.ShapeDtypeStruct((t, d), F32),
        compiler_params=pltpu.CompilerParams(
            dimension_semantics=("arbitrary",), vmem_limit_bytes=VMEM_LIMIT),
        name="moe_combine",
    )(pos, x, rw, gfin, ys)


def _worklist(cnt, n_rows):
    ends = jnp.cumsum(cnt)
    starts = ends - cnt
    n_tiles = n_rows // MOE_TILE
    tile_lo = jnp.arange(n_tiles, dtype=I32) * MOE_TILE
    exp_lo = starts[1:]
    rank_t = jnp.arange(n_tiles, dtype=I32) + jnp.sum(exp_lo[None, :] < tile_lo[:, None], axis=1)
    rank_e = jnp.arange(N_EXPERTS - 1, dtype=I32) + jnp.sum(tile_lo[None, :] <= exp_lo[:, None], axis=1)
    vals = jnp.concatenate([tile_lo, exp_lo])
    ranks = jnp.concatenate([rank_t, rank_e])
    slot = jnp.arange(vals.shape[0], dtype=I32)
    lo = jnp.sum(jnp.where(ranks[None, :] == slot[:, None], vals[None, :], 0), axis=1)
    hi = jnp.concatenate([lo[1:], jnp.full((1,), n_rows, I32)])
    tile = jnp.minimum(lo // MOE_TILE, n_tiles - 1)
    expert = jnp.minimum(jnp.sum(ends[None, :] <= lo[:, None], axis=1), N_EXPERTS - 1).astype(I32)
    return tile, expert, lo, hi


def _moe(x, h, lg, wg, wu, wd, layer, gfin, final):
    t = x.shape[0]
    rw, cnt = _route(lg)
    cnt = cnt[0, :N_EXPERTS].astype(I32)
    starts = jnp.cumsum(cnt) - cnt
    ids = rw[:, 4:6].astype(I32)
    offs = jnp.sum(jnp.where(ids[:, :, None] == jnp.arange(N_EXPERTS, dtype=I32), starts, 0), axis=-1)
    pos = (offs + rw[:, 0:2].astype(I32)).reshape(2 * t)
    inv = jnp.zeros((2 * t,), I32).at[pos].set(jnp.arange(2 * t, dtype=I32) // 2, unique_indices=True)
    work = _worklist(cnt, 2 * t)
    ys = _experts(work, inv, layer * N_EXPERTS, h, wg, wu, wd)
    return _combine(pos, x, rw, gfin, ys, final)


def _router_weights(w_group, b_group, w_router, b_router):
    d = w_group.shape[0]
    w = jnp.concatenate([w_group] + [w_router[g] for g in range(N_GROUPS)], axis=1)
    w = jnp.pad(w, ((0, 0), (0, LANE - w.shape[1])))
    b = jnp.concatenate([b_group, b_router.reshape(-1)])
    b = jnp.pad(b, (0, LANE - b.shape[0])).reshape(1, LANE)
    hi = w.astype(BF16)
    lo = (w - hi.astype(F32)).astype(BF16)
    return hi, lo, b


def _rope_tables(seq):
    half = MLA_ROPE // 2
    pos = jnp.arange(seq, dtype=F32)
    inv = 1.0 / (ROPE_THETA ** (jnp.arange(0, MLA_ROPE, 2, dtype=F32) / MLA_ROPE))
    ang = pos[:, None] * inv[None, :]
    cos, sin = jnp.cos(ang), jnp.sin(ang)
    z = lambda n: jnp.zeros((seq, n), F32)
    tail = LANE - MLA_NOPE - MLA_ROPE
    ctab = jnp.concatenate([jnp.ones((seq, MLA_NOPE), F32), cos, cos, z(tail)], axis=1)
    atab = jnp.concatenate([z(MLA_NOPE), -sin, z(half), z(tail)], axis=1)
    btab = jnp.concatenate([z(MLA_NOPE), z(half), sin, z(tail)], axis=1)
    return ctab, atab, btab


def _odd_weights(w_in, w_uq, w_ukv):
    d = w_in.shape[0]
    sizes = (MLA_Q_RANK, MLA_KV_RANK, MLA_ROPE, SWA_HEADS * SWA_DIM, SWA_KV_HEADS * SWA_DIM,
             SWA_KV_HEADS * SWA_DIM)
    offs = [0]
    for s in sizes:
        offs.append(offs[-1] + s)
    cq, ckv, kpe, qs, ks, vs = [w_in[:, offs[k]:offs[k + 1]] for k in range(6)]
    kpe = jnp.pad(kpe, ((0, 0), (MLA_NOPE, LANE - MLA_NOPE - MLA_ROPE)))
    win = jnp.concatenate([cq, ckv, kpe, qs, ks, vs], axis=1).astype(BF16)
    hq = MLA_NOPE + MLA_ROPE
    wuq = jnp.pad(w_uq.reshape(MLA_Q_RANK, MLA_HEADS, hq), ((0, 0), (0, 0), (0, LANE - hq)))
    wuq = wuq.reshape(MLA_Q_RANK, MLA_HEADS * LANE).astype(BF16)
    wkv = w_ukv.reshape(MLA_KV_RANK, MLA_HEADS, MLA_NOPE + MLA_V)
    wuk = jnp.pad(wkv[:, :, :MLA_NOPE], ((0, 0), (0, 0), (0, LANE - MLA_NOPE)))
    wuk = wuk.reshape(MLA_KV_RANK, MLA_HEADS * LANE).astype(BF16)
    wuv = jnp.pad(wkv[:, :, MLA_NOPE:], ((0, 0), (0, 0), (0, LANE - MLA_V)))
    wuv = wuv.reshape(MLA_KV_RANK, MLA_HEADS * LANE).astype(BF16)
    vone = jnp.tile(jnp.zeros((LANE,), F32).at[MLA_V].set(1.0), MLA_HEADS).reshape(1, MLA_HEADS * LANE)
    return win, wuq, wuk, wuv, vone


def kernel(x, norm_mix, norm_ffn, norm_final, ev_w_in, ev_conv_w, ev_conv_b, ev_cnorm_g, ev_cnorm_b,
           ev_vnorm_g, ev_vnorm_b, ev_w_sp, ev_b_sp, ev_w_out, od_w_in, od_g_cq, od_w_uq, od_g_ckv,
           od_w_ukv, od_sink, od_w_out, moe_w_group, moe_b_group, moe_w_router, moe_b_router,
           moe_w_gate, moe_w_up, moe_w_down):
    b, seq, d = x.shape
    t = b * seq
    depth = norm_mix.shape[0]
    row = lambda v: v.reshape(1, -1)
    xt = x.reshape(t, d)
    gfin = row(norm_final)
    wg_all = moe_w_gate.reshape(depth * N_EXPERTS, d, -1)
    wu_all = moe_w_up.reshape(depth * N_EXPERTS, d, -1)
    wd_all = moe_w_down.reshape(depth * N_EXPERTS, -1, d)
    for layer in range(depth):
        k = layer // 2
        wrh, wrl, br = _router_weights(moe_w_group[layer], moe_b_group[layer], moe_w_router[layer],
                                       moe_b_router[layer])
        gffn = row(norm_ffn[layer])
        if layer % 2 == 0:
            bsp = jnp.broadcast_to(ev_b_sp[k][:, :, None], ev_w_sp[k].shape)
            xt, h3, lg = _even_mixer(
                xt, seq, row(norm_mix[layer]), ev_w_in[k].astype(BF16), ev_conv_w[k], row(ev_conv_b[k]),
                row(ev_cnorm_g[k]), row(ev_cnorm_b[k]), row(ev_vnorm_g[k]), row(ev_vnorm_b[k]),
                ev_w_sp[k].astype(BF16), bsp, ev_w_out[k].astype(BF16), gffn, wrh, wrl, br)
        else:
            win, wuq, wuk, wuv, vone = _odd_weights(od_w_in[k], od_w_uq[k], od_w_ukv[k])
            ctab, atab, btab = _rope_tables(seq)
            qm, km, vm, qs, ks, vs = _odd_proj(
                xt, seq, row(norm_mix[layer]), win, row(od_g_cq[k]), wuq, row(od_g_ckv[k]), wuk, wuv,
                vone, ctab, atab, btab)
            xt, h3, lg = _attention(xt, seq, od_sink[k], qm, km, vm, qs, ks, vs,
                                    od_w_out[k].astype(BF16), gffn, wrh, wrl, br)
        xt = _moe(xt, h3, lg, wg_all, wu_all, wd_all, layer, gfin, layer == depth - 1)
    return xt.reshape(b, seq, d)
```

```python
import functools

import jax
import jax.numpy as jnp
from jax import lax
from jax.experimental import pallas as pl
from jax.experimental.pallas import tpu as pltpu

F32 = jnp.float32
BF16 = jnp.bfloat16
I32 = jnp.int32

EPS = 1e-6
NEG_INF = -1e30
LOG2E = 1.4426950408889634
LANE = 128
VMEM_LIMIT = 56 * 1024 * 1024

CONV_DIM = 512
CONV_GROUP = 128
CONV_WIDTH = 31
SGU_DIM = 512
SGU_HEAD = 128
CHUNK = 128
MLA_HEADS = 8
MLA_Q_RANK = 256
MLA_KV_RANK = 128
MLA_NOPE = 64
MLA_ROPE = 32
MLA_V = 64
ROPE_THETA = 10000.0
SWA_HEADS = 8
SWA_KV_HEADS = 2
SWA_DIM = 64
WINDOW = 128
N_GROUPS = 4
N_EXP = 8
N_EXPERTS = N_GROUPS * N_EXP

HALO = 16
CONV_ROWS = 64
MOE_TILE = 512
ROW_TILE = 256
DISPATCH_TILE = 512


def _dot(a, b):
    return jnp.dot(a, b, preferred_element_type=F32)


def _dot_nt(a, b):
    return lax.dot_general(a, b, (((1,), (1,)), ((), ())), preferred_element_type=F32)


def _rms(x, g):
    return x * lax.rsqrt(jnp.mean(x * x, axis=-1, keepdims=True) + EPS) * g


def _ln(x, g, b):
    mu = jnp.mean(x, axis=-1, keepdims=True)
    d = x - mu
    var = jnp.mean(d * d, axis=-1, keepdims=True)
    return d * lax.rsqrt(var + EPS) * g + b


def _gelu(x):
    return 0.5 * x * (1.0 + jnp.tanh(0.7978845608028654 * (x + 0.044715 * (x * x * x))))


def _sigmoid(x):
    return 1.0 / (1.0 + jnp.exp(-x))


def _router_epilogue(xn, gffn_ref, wrh_ref, wrl_ref, br_ref, h3_ref, lg_ref):
    h = _rms(xn, gffn_ref[...])
    h3_ref[...] = h
    hi = h.astype(BF16)
    lo = (h - hi.astype(F32)).astype(BF16)
    wh = wrh_ref[...]
    lg_ref[...] = _dot(hi, wh) + _dot(lo, wh) + _dot(hi, wrl_ref[...]) + br_ref[...]


def _even_kernel(xc_ref, xp_ref, gmix_ref, win_ref, cw_ref, cb_ref, cng_ref, cnb_ref, vng_ref,
                 vnb_ref, wsp_ref, bsp_ref, wout_ref, gffn_ref, wrh_ref, wrl_ref, br_ref,
                 xo_ref, h3_ref, lg_ref, a_s, mix_s, *, ns, sub, seq):
    j = pl.program_id(1)
    win_rows = CONV_ROWS + 2 * HALO
    n_chunks = sub // CONV_ROWS

    def proj_conv_input():
        h = _rms(xc_ref[...], gmix_ref[...]).astype(BF16)
        pa = _dot(h, win_ref[:, 0:2 * CONV_DIM])
        return h, pa[:, :CONV_DIM] * _sigmoid(pa[:, CONV_DIM:])

    def proj_gating(h, a):
        r0 = pl.multiple_of(j * sub, sub)
        a_s[pl.ds(HALO + r0, sub), :] = a
        u = _gelu(_dot(h, win_ref[:, 2 * CONV_DIM:2 * CONV_DIM + SGU_DIM]))
        v = _gelu(_dot(h, win_ref[:, 2 * CONV_DIM + SGU_DIM:]))
        for hd in range(SGU_DIM // SGU_HEAD):
            cs = slice(hd * SGU_HEAD, (hd + 1) * SGU_HEAD)
            vn = _ln(v[:, cs], vng_ref[:, cs], vnb_ref[:, cs]).astype(BF16)
            for c in range(sub // CHUNK):
                rs = slice(c * CHUNK, (c + 1) * CHUNK)
                sv = _dot(wsp_ref[hd], vn[rs, :]) + bsp_ref[hd]
                mix_s[pl.ds(r0 + c * CHUNK, CHUNK),
                      CONV_DIM + hd * SGU_HEAD:CONV_DIM + (hd + 1) * SGU_HEAD] = (u[rs, cs] * sv).astype(BF16)

    def conv_and_out(next_halo):
        r0 = (j - 1) * sub
        for i in range(n_chunks):
            base = pl.multiple_of(r0 + i * CONV_ROWS, CONV_ROWS)
            if i < n_chunks - 1:
                win = a_s[pl.ds(base, win_rows), :]
            else:
                win = jnp.concatenate([a_s[pl.ds(base, CONV_ROWS + HALO), :], next_halo], axis=0)
            acc = jnp.zeros((CONV_ROWS, CONV_DIM), F32)
            for s in range(8):
                sh = win if s == 0 else pltpu.roll(win, win_rows - s, 0)
                for m in range(4):
                    k = 8 * m + s - 1
                    if 0 <= k < CONV_WIDTH:
                        acc = acc + sh[8 * m:8 * m + CONV_ROWS, :] * cw_ref[k:k + 1, :]
            acc = acc + cb_ref[...]
            for g in range(CONV_DIM // CONV_GROUP):
                cs = slice(g * CONV_GROUP, (g + 1) * CONV_GROUP)
                y = _ln(acc[:, cs], cng_ref[:, cs], cnb_ref[:, cs])
                mix_s[pl.ds(base, CONV_ROWS), cs] = (y * _sigmoid(y)).astype(BF16)
        mix = mix_s[pl.ds(pl.multiple_of(r0, sub), sub), :]
        xn = xp_ref[...] + _dot(mix, wout_ref[...])
        xo_ref[...] = xn
        _router_epilogue(xn, gffn_ref, wrh_ref, wrl_ref, br_ref, h3_ref, lg_ref)

    @pl.when(j == 0)
    def _():
        a_s[0:HALO, :] = jnp.zeros((HALO, CONV_DIM), F32)
        h, a = proj_conv_input()
        proj_gating(h, a)

    @pl.when((j >= 1) & (j < ns))
    def _():
        h, a = proj_conv_input()
        conv_and_out(a[0:HALO, :])
        proj_gating(h, a)

    @pl.when(j == ns)
    def _():
        conv_and_out(jnp.zeros((HALO, CONV_DIM), F32))


def _const_spec(shape):
    nd = len(shape)
    return pl.BlockSpec(shape, lambda *_: (0,) * nd)


def _even_mixer(x, seq, gmix, win, cw, cb, cng, cnb, vng, vnb, wsp, bsp, wout, gffn, wrh, wrl, br):
    t, d = x.shape
    nb = t // seq
    sub = 512
    ns = seq // sub
    row = lambda b, j: (b * ns + jnp.minimum(j, ns - 1), 0)
    prev = lambda b, j: (b * ns + jnp.maximum(j - 1, 0), 0)
    consts = (gmix, win, cw, cb, cng, cnb, vng, vnb, wsp, bsp, wout, gffn, wrh, wrl, br)
    return pl.pallas_call(
        functools.partial(_even_kernel, ns=ns, sub=sub, seq=seq),
        grid=(nb, ns + 1),
        in_specs=[pl.BlockSpec((sub, d), row), pl.BlockSpec((sub, d), prev)]
        + [_const_spec(c.shape) for c in consts],
        out_specs=[pl.BlockSpec((sub, d), prev),
                   pl.BlockSpec((sub, d), prev),
                   pl.BlockSpec((sub, LANE), prev)],
        out_shape=[jax.ShapeDtypeStruct((t, d), F32),
                   jax.ShapeDtypeStruct((t, d), F32),
                   jax.ShapeDtypeStruct((t, LANE), F32)],
        scratch_shapes=[pltpu.VMEM((seq + HALO, CONV_DIM), F32),
                        pltpu.VMEM((seq, CONV_DIM + SGU_DIM), BF16)],
        compiler_params=pltpu.CompilerParams(
            dimension_semantics=("arbitrary", "arbitrary"), vmem_limit_bytes=VMEM_LIMIT),
        name="even_mixer",
    )(x, x, *consts)


def _odd_proj_kernel(x_ref, gmix_ref, win_ref, gcq_ref, wuq_ref, gckv_ref, wuk_ref, wuv_ref, vone_ref,
                     cos_ref, sa_ref, sb_ref, qm_ref, km_ref, vm_ref, qs_ref, ks_ref, vs_ref):
    h = _rms(x_ref[...], gmix_ref[...]).astype(BF16)
    p = _dot(h, win_ref[...])
    o = 0
    cq = p[:, o:o + MLA_Q_RANK]; o += MLA_Q_RANK
    ckv = p[:, o:o + MLA_KV_RANK]; o += MLA_KV_RANK
    kpe = p[:, o:o + LANE]; o += LANE
    qs = p[:, o:o + SWA_HEADS * SWA_DIM]; o += SWA_HEADS * SWA_DIM
    ks = p[:, o:o + LANE]; o += LANE
    vs = p[:, o:o + LANE]

    cos = cos_ref[...]
    sa = sa_ref[...]
    sb = sb_ref[...]

    def rope(z, reps):
        w = z.shape[1]
        c = jnp.concatenate([cos] * reps, axis=1) if reps > 1 else cos
        a = jnp.concatenate([sa] * reps, axis=1) if reps > 1 else sa
        b = jnp.concatenate([sb] * reps, axis=1) if reps > 1 else sb
        return z * c + pltpu.roll(z, w - MLA_ROPE // 2, 1) * a + pltpu.roll(z, MLA_ROPE // 2, 1) * b

    q = _dot(_rms(cq, gcq_ref[...]).astype(BF16), wuq_ref[...])
    qm_ref[...] = (rope(q, MLA_HEADS) * ((MLA_NOPE + MLA_ROPE) ** -0.5 * LOG2E)).astype(BF16)
    ckvn = _rms(ckv, gckv_ref[...]).astype(BF16)
    kr = rope(kpe, 1)
    km_ref[...] = (_dot(ckvn, wuk_ref[...]) + jnp.concatenate([kr] * MLA_HEADS, axis=1)).astype(BF16)
    vm_ref[...] = (_dot(ckvn, wuv_ref[...]) + vone_ref[...]).astype(BF16)
    qs_ref[...] = (qs * (SWA_DIM ** -0.5)).astype(BF16)
    ks_ref[...] = ks.astype(BF16)
    vs_ref[...] = vs.astype(BF16)


def _odd_proj(x, seq, gmix, win, gcq, wuq, gckv, wuk, wuv, vone, cos, sa, sb):
    t, d = x.shape
    tm = 512
    nsq = seq // tm
    row = lambda i: (i, 0)
    pos = lambda i: (i % nsq, 0)
    consts = (gmix, win, gcq, wuq, gckv, wuk, wuv, vone)
    widths = (MLA_HEADS * LANE, MLA_HEADS * LANE, MLA_HEADS * LANE, SWA_HEADS * SWA_DIM, LANE, LANE)
    return pl.pallas_call(
        _odd_proj_kernel,
        grid=(t // tm,),
        in_specs=[pl.BlockSpec((tm, d), row)] + [_const_spec(c.shape) for c in consts]
        + [pl.BlockSpec((tm, LANE), pos)] * 3,
        out_specs=[pl.BlockSpec((tm, w), row) for w in widths],
        out_shape=[jax.ShapeDtypeStruct((t, w), BF16) for w in widths],
        compiler_params=pltpu.CompilerParams(
            dimension_semantics=("arbitrary",), vmem_limit_bytes=VMEM_LIMIT),
        name="odd_proj",
    )(x, *consts, cos, sa, sb)


def _attn_kernel(sink_ref, x_ref, qm_ref, km_ref, vm_ref, qs_ref, ks_ref, vs_ref, wout_ref,
                 gffn_ref, wrh_ref, wrl_ref, br_ref, xo_ref, h3_ref, lg_ref, *, tq, seq):
    i = pl.program_id(1)
    qm = qm_ref[...]
    outs = []
    for hd in range(MLA_HEADS):
        cs = slice(hd * LANE, (hd + 1) * LANE)
        s = _dot_nt(qm[:, cs], km_ref[:, cs])
        e = jnp.exp2((s - jnp.max(s, axis=-1, keepdims=True)).astype(BF16))
        pv = _dot(e, vm_ref[:, cs])
        outs.append(pv[:, :MLA_V] / pv[:, MLA_V:MLA_V + 1])

    span = LANE + 2 * WINDOW
    rep = SWA_HEADS // SWA_KV_HEADS
    swa_rows = []
    for blk in range(tq // LANE):
        q0 = i * tq + blk * LANE
        start = pl.multiple_of(jnp.clip(q0 - WINDOW, 0, seq - span), LANE)
        kw = ks_ref[pl.ds(start, span), :]
        vw = vs_ref[pl.ds(start, span), :]
        qpos = q0 + lax.broadcasted_iota(I32, (LANE, span), 0)
        kpos = start + lax.broadcasted_iota(I32, (LANE, span), 1)
        absd = jnp.abs(qpos - kpos).astype(F32)
        in_win = absd <= float(WINDOW)
        heads = []
        for hd in range(SWA_HEADS):
            g = hd // rep
            q = qs_ref[blk * LANE:(blk + 1) * LANE, hd * SWA_DIM:(hd + 1) * SWA_DIM]
            s = _dot_nt(q, kw[:, g * SWA_DIM:(g + 1) * SWA_DIM])
            s = jnp.where(in_win, s - (2.0 ** -(hd + 1)) * absd, NEG_INF)
            sk = sink_ref[hd]
            m = jnp.maximum(jnp.max(s, axis=-1, keepdims=True), sk)
            e = jnp.exp(s - m)
            den = jnp.sum(e, axis=-1, keepdims=True) + jnp.exp(sk - m)
            heads.append(_dot(e.astype(BF16), vw[:, g * SWA_DIM:(g + 1) * SWA_DIM]) / den)
        swa_rows.append(jnp.concatenate(heads, axis=1))
    swa = jnp.concatenate(swa_rows, axis=0) if len(swa_rows) > 1 else swa_rows[0]

    mix = jnp.concatenate(outs + [swa], axis=1).astype(BF16)
    xn = x_ref[...] + _dot(mix, wout_ref[...])
    xo_ref[...] = xn
    _router_epilogue(xn, gffn_ref, wrh_ref, wrl_ref, br_ref, h3_ref, lg_ref)


def _attention(x, seq, sink, qm, km, vm, qs, ks, vs, wout, gffn, wrh, wrl, br):
    t, d = x.shape
    nb = t // seq
    tq = 512
    nq = seq // tq
    row = lambda b, i, *_: (b * nq + i, 0)
    bat = lambda b, i, *_: (b, 0)
    consts = (wout, gffn, wrh, wrl, br)
    grid_spec = pltpu.PrefetchScalarGridSpec(
        num_scalar_prefetch=1,
        grid=(nb, nq),
        in_specs=[pl.BlockSpec((tq, d), row),
                  pl.BlockSpec((tq, qm.shape[1]), row),
                  pl.BlockSpec((seq, km.shape[1]), bat),
                  pl.BlockSpec((seq, vm.shape[1]), bat),
                  pl.BlockSpec((tq, qs.shape[1]), row),
                  pl.BlockSpec((seq, ks.shape[1]), bat),
                  pl.BlockSpec((seq, vs.shape[1]), bat)]
        + [pl.BlockSpec(c.shape, lambda b, i, *_, n=len(c.shape): (0,) * n) for c in consts],
        out_specs=[pl.BlockSpec((tq, d), row),
                   pl.BlockSpec((tq, d), row),
                   pl.BlockSpec((tq, LANE), row)],
    )
    return pl.pallas_call(
        functools.partial(_attn_kernel, tq=tq, seq=seq),
        grid_spec=grid_spec,
        out_shape=[jax.ShapeDtypeStruct((t, d), F32),
                   jax.ShapeDtypeStruct((t, d), F32),
                   jax.ShapeDtypeStruct((t, LANE), F32)],
        compiler_params=pltpu.CompilerParams(
            dimension_semantics=("arbitrary", "arbitrary"), vmem_limit_bytes=VMEM_LIMIT),
        name="attention",
    )(sink, x, qm, km, vm, qs, ks, vs, *consts)


def _route_kernel(lg_ref, lower_ref, out_ref, cnt_ref, carry_ref):
    i = pl.program_id(0)
    rows = lg_ref.shape[0]
    lg = lg_ref[...]
    lane = lax.broadcasted_iota(I32, (rows, LANE), 1)
    ninf = -jnp.inf

    @pl.when(i == 0)
    def _():
        carry_ref[...] = jnp.zeros_like(carry_ref)

    gl = jnp.where(lane < N_GROUPS, lg, ninf)
    gmax = jnp.max(gl, axis=1, keepdims=True)
    gidx = jnp.min(jnp.where(gl == gmax, lane, LANE), axis=1, keepdims=True)
    gp = 1.0 / jnp.sum(jnp.exp(gl - gmax), axis=1, keepdims=True)

    first = N_GROUPS + gidx * N_EXP
    el = jnp.where((lane >= first) & (lane < first + N_EXP), lg, ninf)
    m1 = jnp.max(el, axis=1, keepdims=True)
    i1 = jnp.min(jnp.where(el == m1, lane, LANE), axis=1, keepdims=True)
    el2 = jnp.where(lane == i1, ninf, el)
    m2 = jnp.max(el2, axis=1, keepdims=True)
    i2 = jnp.min(jnp.where(el2 == m2, lane, LANE), axis=1, keepdims=True)
    e2 = jnp.exp(m2 - m1)
    w1 = gp / (1.0 + e2)
    w2 = gp * e2 / (1.0 + e2)
    id1 = i1 - N_GROUPS
    id2 = i2 - N_GROUPS
    onehot = jnp.where((lane == id1) | (lane == id2), 1.0, 0.0)

    before = _dot(lower_ref[...], onehot.astype(BF16)) + carry_ref[...]
    rank1 = jnp.sum(jnp.where(lane == id1, before, 0.0), axis=1, keepdims=True)
    rank2 = jnp.sum(jnp.where(lane == id2, before, 0.0), axis=1, keepdims=True)
    carry_ref[...] += jnp.sum(onehot, axis=0, keepdims=True)
    out = jnp.where(lane == 0, rank1, 0.0)
    out = jnp.where(lane == 1, rank2, out)
    out = jnp.where(lane == 2, w1, out)
    out = jnp.where(lane == 3, w2, out)
    out = jnp.where(lane == 4, id1.astype(F32), out)
    out = jnp.where(lane == 5, id2.astype(F32), out)
    out_ref[...] = out
    cnt_ref[...] = carry_ref[...]


def _route(lg):
    t = lg.shape[0]
    rows = 512
    lower = jnp.tri(rows, k=-1, dtype=BF16)
    return pl.pallas_call(
        _route_kernel,
        grid=(t // rows,),
        in_specs=[pl.BlockSpec((rows, LANE), lambda i: (i, 0)),
                  pl.BlockSpec((rows, rows), lambda i: (0, 0))],
        out_specs=[pl.BlockSpec((rows, LANE), lambda i: (i, 0)),
                   pl.BlockSpec((1, LANE), lambda i: (0, 0))],
        out_shape=[jax.ShapeDtypeStruct((t, LANE), F32), jax.ShapeDtypeStruct((1, LANE), F32)],
        scratch_shapes=[pltpu.VMEM((1, LANE), F32)],
        compiler_params=pltpu.CompilerParams(dimension_semantics=("arbitrary",)),
        name="moe_route",
    )(lg, lower)


def _row(ref, r):
    return ref.at[pl.ds(r, 1), :]


def _dispatch_kernel(pos_ref, h_ref, xs_ref, sem):
    i = pl.program_id(0)
    rows = h_ref.shape[0]

    for r in range(rows):
        t = i * rows + r
        pltpu.make_async_copy(_row(h_ref, r), _row(xs_ref, pos_ref[2 * t]), sem).start(priority=0)
        pltpu.make_async_copy(_row(h_ref, r), _row(xs_ref, pos_ref[2 * t + 1]), sem).start(priority=1)
    for _ in range(2):
        pltpu.make_async_copy(h_ref, xs_ref.at[pl.ds(0, rows), :], sem).wait()


def _dispatch(pos, h):
    t, d = h.shape
    grid_spec = pltpu.PrefetchScalarGridSpec(
        num_scalar_prefetch=1,
        grid=(t // DISPATCH_TILE,),
        in_specs=[pl.BlockSpec((DISPATCH_TILE, d), lambda i, *_: (i, 0))],
        out_specs=pl.BlockSpec(memory_space=pl.ANY),
        scratch_shapes=[pltpu.SemaphoreType.DMA],
    )
    return pl.pallas_call(
        _dispatch_kernel,
        grid_spec=grid_spec,
        out_shape=jax.ShapeDtypeStruct((2 * t, d), F32),
        compiler_params=pltpu.CompilerParams(dimension_semantics=("arbitrary",)),
        name="moe_dispatch",
    )(pos, h)


def _expert_kernel(tile_ref, exp_ref, lo_ref, hi_ref, xs_ref, wg_ref, wu_ref, wd_ref, ys_ref,
                   wg_s, wu_s, wd_s):
    i = pl.program_id(0)
    prev = jnp.maximum(i - 1, 0)
    new_tile = (i == 0) | (tile_ref[i] != tile_ref[prev])
    new_exp = (i == 0) | (exp_ref[i] != exp_ref[prev])
    lo = lo_ref[i]
    hi = hi_ref[i]
    rows = xs_ref.shape[0]

    @pl.when(new_exp)
    def _():
        wg_s[...] = wg_ref[0].astype(BF16)
        wu_s[...] = wu_ref[0].astype(BF16)
        wd_s[...] = wd_ref[0].astype(BF16)

    @pl.when(new_tile)
    def _():
        ys_ref[...] = jnp.zeros_like(ys_ref)

    @pl.when(hi > lo)
    def _():
        x = xs_ref[...].astype(BF16)
        g = _dot(x, wg_s[...])
        u = _dot(x, wu_s[...])
        hid = (g * _sigmoid(g) * u).astype(BF16)
        y = _dot(hid, wd_s[...])
        r = tile_ref[i] * rows + lax.broadcasted_iota(I32, (rows, 1), 0)
        ys_ref[...] += jnp.where((r >= lo) & (r < hi), y, 0.0)


def _experts(work, first_expert, xs, wg, wu, wd):
    n, d = xs.shape
    f = wg.shape[2]
    nwork = work[0].shape[0]
    wmap = lambda i, tl, ex, lo, hi: (first_expert + ex[i], 0, 0)
    grid_spec = pltpu.PrefetchScalarGridSpec(
        num_scalar_prefetch=4,
        grid=(nwork,),
        in_specs=[pl.BlockSpec((MOE_TILE, d), lambda i, tl, ex, lo, hi: (tl[i], 0)),
                  pl.BlockSpec((1, d, f), wmap),
                  pl.BlockSpec((1, d, f), wmap),
                  pl.BlockSpec((1, f, d), wmap)],
        out_specs=pl.BlockSpec((MOE_TILE, d), lambda i, tl, ex, lo, hi: (tl[i], 0)),
        scratch_shapes=[pltpu.VMEM((d, f), BF16), pltpu.VMEM((d, f), BF16), pltpu.VMEM((f, d), BF16)],
    )
    return pl.pallas_call(
        _expert_kernel,
        grid_spec=grid_spec,
        out_shape=jax.ShapeDtypeStruct((n, d), F32),
        compiler_params=pltpu.CompilerParams(
            dimension_semantics=("arbitrary",), vmem_limit_bytes=VMEM_LIMIT),
        name="moe_experts",
    )(*work, xs, wg, wu, wd)


def _combine_kernel(pos_ref, x_ref, rw_ref, gfin_ref, ys_ref, xo_ref, buf, sems, *, final):
    i = pl.program_id(0)
    n = pl.num_programs(0)
    rows = x_ref.shape[0]
    slot = i % 2

    def issue(step, slt):
        for r in range(rows):
            t = step * rows + r
            pltpu.make_async_copy(_row(ys_ref, pos_ref[2 * t]), _row(buf.at[slt], r),
                                  sems.at[slt]).start(priority=0)
            pltpu.make_async_copy(_row(ys_ref, pos_ref[2 * t + 1]), _row(buf.at[slt], rows + r),
                                  sems.at[slt]).start(priority=1)

    @pl.when(i == 0)
    def _():
        issue(0, 0)

    for slt in range(2):
        @pl.when((i + 1 < n) & (slot == 1 - slt))
        def _():
            issue(i + 1, slt)

    pltpu.make_async_copy(ys_ref.at[pl.ds(0, 2 * rows), :], buf.at[slot], sems.at[slot]).wait()

    rw = rw_ref[...]
    xn = x_ref[...] + rw[:, 2:3] * buf[slot, 0:rows, :] + rw[:, 3:4] * buf[slot, rows:2 * rows, :]
    if final:
        xn = _rms(xn, gfin_ref[...])
    xo_ref[...] = xn


def _combine(pos, x, rw, gfin, ys, final):
    t, d = x.shape
    grid_spec = pltpu.PrefetchScalarGridSpec(
        num_scalar_prefetch=1,
        grid=(t // ROW_TILE,),
        in_specs=[pl.BlockSpec((ROW_TILE, d), lambda i, *_: (i, 0)),
                  pl.BlockSpec((ROW_TILE, LANE), lambda i, *_: (i, 0)),
                  pl.BlockSpec((1, d), lambda i, *_: (0, 0)),
                  pl.BlockSpec(memory_space=pl.ANY)],
        out_specs=pl.BlockSpec((ROW_TILE, d), lambda i, *_: (i, 0)),
        scratch_shapes=[pltpu.VMEM((2, 2 * ROW_TILE, d), F32), pltpu.SemaphoreType.DMA((2,))],
    )
    return pl.pallas_call(
        functools.partial(_combine_kernel, final=final),
        grid_spec=grid_spec,
        out_shape=jax.ShapeDtypeStruct((t, d), F32),
        compiler_params=pltpu.CompilerParams(
            dimension_semantics=("arbitrary",), vmem_limit_bytes=VMEM_LIMIT),
        name="moe_combine",
    )(pos, x, rw, gfin, ys)


def _worklist(cnt, n_rows):
    ends = jnp.cumsum(cnt)
    starts = ends - cnt
    n_tiles = n_rows // MOE_TILE
    tile_lo = jnp.arange(n_tiles, dtype=I32) * MOE_TILE
    exp_lo = starts[1:]
    rank_t = jnp.arange(n_tiles, dtype=I32) + jnp.sum(exp_lo[None, :] < tile_lo[:, None], axis=1)
    rank_e = jnp.arange(N_EXPERTS - 1, dtype=I32) + jnp.sum(tile_lo[None, :] <= exp_lo[:, None], axis=1)
    vals = jnp.concatenate([tile_lo, exp_lo])
    ranks = jnp.concatenate([rank_t, rank_e])
    slot = jnp.arange(vals.shape[0], dtype=I32)
    lo = jnp.sum(jnp.where(ranks[None, :] == slot[:, None], vals[None, :], 0), axis=1)
    hi = jnp.concatenate([lo[1:], jnp.full((1,), n_rows, I32)])
    tile = jnp.minimum(lo // MOE_TILE, n_tiles - 1)
    expert = jnp.minimum(jnp.sum(ends[None, :] <= lo[:, None], axis=1), N_EXPERTS - 1).astype(I32)
    return tile, expert, lo, hi


def _moe(x, h, lg, wg, wu, wd, layer, gfin, final):
    t = x.shape[0]
    rw, cnt = _route(lg)
    cnt = cnt[0, :N_EXPERTS].astype(I32)
    starts = jnp.cumsum(cnt) - cnt
    ids = rw[:, 4:6].astype(I32)
    offs = jnp.sum(jnp.where(ids[:, :, None] == jnp.arange(N_EXPERTS, dtype=I32), starts, 0), axis=-1)
    pos = (offs + rw[:, 0:2].astype(I32)).reshape(2 * t)
    xs = _dispatch(pos, h)
    work = _worklist(cnt, 2 * t)
    ys = _experts(work, layer * N_EXPERTS, xs, wg, wu, wd)
    return _combine(pos, x, rw, gfin, ys, final)


def _router_weights(w_group, b_group, w_router, b_router):
    d = w_group.shape[0]
    w = jnp.concatenate([w_group] + [w_router[g] for g in range(N_GROUPS)], axis=1)
    w = jnp.pad(w, ((0, 0), (0, LANE - w.shape[1])))
    b = jnp.concatenate([b_group, b_router.reshape(-1)])
    b = jnp.pad(b, (0, LANE - b.shape[0])).reshape(1, LANE)
    hi = w.astype(BF16)
    lo = (w - hi.astype(F32)).astype(BF16)
    return hi, lo, b


def _rope_tables(seq):
    half = MLA_ROPE // 2
    pos = jnp.arange(seq, dtype=F32)
    inv = 1.0 / (ROPE_THETA ** (jnp.arange(0, MLA_ROPE, 2, dtype=F32) / MLA_ROPE))
    ang = pos[:, None] * inv[None, :]
    cos, sin = jnp.cos(ang), jnp.sin(ang)
    z = lambda n: jnp.zeros((seq, n), F32)
    tail = LANE - MLA_NOPE - MLA_ROPE
    ctab = jnp.concatenate([jnp.ones((seq, MLA_NOPE), F32), cos, cos, z(tail)], axis=1)
    atab = jnp.concatenate([z(MLA_NOPE), -sin, z(half), z(tail)], axis=1)
    btab = jnp.concatenate([z(MLA_NOPE), z(half), sin, z(tail)], axis=1)
    return ctab, atab, btab


def _odd_weights(w_in, w_uq, w_ukv):
    d = w_in.shape[0]
    sizes = (MLA_Q_RANK, MLA_KV_RANK, MLA_ROPE, SWA_HEADS * SWA_DIM, SWA_KV_HEADS * SWA_DIM,
             SWA_KV_HEADS * SWA_DIM)
    offs = [0]
    for s in sizes:
        offs.append(offs[-1] + s)
    cq, ckv, kpe, qs, ks, vs = [w_in[:, offs[k]:offs[k + 1]] for k in range(6)]
    kpe = jnp.pad(kpe, ((0, 0), (MLA_NOPE, LANE - MLA_NOPE - MLA_ROPE)))
    win = jnp.concatenate([cq, ckv, kpe, qs, ks, vs], axis=1).astype(BF16)
    hq = MLA_NOPE + MLA_ROPE
    wuq = jnp.pad(w_uq.reshape(MLA_Q_RANK, MLA_HEADS, hq), ((0, 0), (0, 0), (0, LANE - hq)))
    wuq = wuq.reshape(MLA_Q_RANK, MLA_HEADS * LANE).astype(BF16)
    wkv = w_ukv.reshape(MLA_KV_RANK, MLA_HEADS, MLA_NOPE + MLA_V)
    wuk = jnp.pad(wkv[:, :, :MLA_NOPE], ((0, 0), (0, 0), (0, LANE - MLA_NOPE)))
    wuk = wuk.reshape(MLA_KV_RANK, MLA_HEADS * LANE).astype(BF16)
    wuv = jnp.pad(wkv[:, :, MLA_NOPE:], ((0, 0), (0, 0), (0, LANE - MLA_V)))
    wuv = wuv.reshape(MLA_KV_RANK, MLA_HEADS * LANE).astype(BF16)
    vone = jnp.tile(jnp.zeros((LANE,), F32).at[MLA_V].set(1.0), MLA_HEADS).reshape(1, MLA_HEADS * LANE)
    return win, wuq, wuk, wuv, vone


def kernel(x, norm_mix, norm_ffn, norm_final, ev_w_in, ev_conv_w, ev_conv_b, ev_cnorm_g, ev_cnorm_b,
           ev_vnorm_g, ev_vnorm_b, ev_w_sp, ev_b_sp, ev_w_out, od_w_in, od_g_cq, od_w_uq, od_g_ckv,
           od_w_ukv, od_sink, od_w_out, moe_w_group, moe_b_group, moe_w_router, moe_b_router,
           moe_w_gate, moe_w_up, moe_w_down):
    b, seq, d = x.shape
    t = b * seq
    depth = norm_mix.shape[0]
    row = lambda v: v.reshape(1, -1)
    xt = x.reshape(t, d)
    gfin = row(norm_final)
    wg_all = moe_w_gate.reshape(depth * N_EXPERTS, d, -1)
    wu_all = moe_w_up.reshape(depth * N_EXPERTS, d, -1)
    wd_all = moe_w_down.reshape(depth * N_EXPERTS, -1, d)
    for layer in range(depth):
        k = layer // 2
        wrh, wrl, br = _router_weights(moe_w_group[layer], moe_b_group[layer], moe_w_router[layer],
                                       moe_b_router[layer])
        gffn = row(norm_ffn[layer])
        if layer % 2 == 0:
            bsp = jnp.broadcast_to(ev_b_sp[k][:, :, None], ev_w_sp[k].shape)
            xt, h3, lg = _even_mixer(
                xt, seq, row(norm_mix[layer]), ev_w_in[k].astype(BF16), ev_conv_w[k], row(ev_conv_b[k]),
                row(ev_cnorm_g[k]), row(ev_cnorm_b[k]), row(ev_vnorm_g[k]), row(ev_vnorm_b[k]),
                ev_w_sp[k].astype(BF16), bsp, ev_w_out[k].astype(BF16), gffn, wrh, wrl, br)
        else:
            win, wuq, wuk, wuv, vone = _odd_weights(od_w_in[k], od_w_uq[k], od_w_ukv[k])
            ctab, atab, btab = _rope_tables(seq)
            qm, km, vm, qs, ks, vs = _odd_proj(
                xt, seq, row(norm_mix[layer]), win, row(od_g_cq[k]), wuq, row(od_g_ckv[k]), wuk, wuv,
                vone, ctab, atab, btab)
            xt, h3, lg = _attention(xt, seq, od_sink[k], qm, km, vm, qs, ks, vs,
                                    od_w_out[k].astype(BF16), gffn, wrh, wrl, br)
        xt = _moe(xt, h3, lg, wg_all, wu_all, wd_all, layer, gfin, layer == depth - 1)
    return xt.reshape(b, seq, d)
```

```python
import functools

import jax
import jax.numpy as jnp
from jax import lax
from jax.experimental import pallas as pl
from jax.experimental.pallas import tpu as pltpu

F32 = jnp.float32
BF16 = jnp.bfloat16
I32 = jnp.int32

EPS = 1e-6
NEG_INF = -1e30
LOG2E = 1.4426950408889634
LANE = 128
VMEM_LIMIT = 56 * 1024 * 1024

CONV_DIM = 512
CONV_GROUP = 128
CONV_WIDTH = 31
SGU_DIM = 512
SGU_HEAD = 128
CHUNK = 128
MLA_HEADS = 8
MLA_Q_RANK = 256
MLA_KV_RANK = 128
MLA_NOPE = 64
MLA_ROPE = 32
MLA_V = 64
ROPE_THETA = 10000.0
SWA_HEADS = 8
SWA_KV_HEADS = 2
SWA_DIM = 64
WINDOW = 128
N_GROUPS = 4
N_EXP = 8
N_EXPERTS = N_GROUPS * N_EXP

HALO = 16
CONV_ROWS = 64
MOE_TILE = 512
XBUF_SLOTS = 3
ROW_TILE = 256
DISPATCH_TILE = 512


def _dot(a, b):
    return jnp.dot(a, b, preferred_element_type=F32)


def _dot_nt(a, b):
    return lax.dot_general(a, b, (((1,), (1,)), ((), ())), preferred_element_type=F32)


def _rms(x, g):
    return x * lax.rsqrt(jnp.mean(x * x, axis=-1, keepdims=True) + EPS) * g


def _ln(x, g, b):
    mu = jnp.mean(x, axis=-1, keepdims=True)
    d = x - mu
    var = jnp.mean(d * d, axis=-1, keepdims=True)
    return d * lax.rsqrt(var + EPS) * g + b


def _gelu(x):
    return 0.5 * x * (1.0 + jnp.tanh(0.7978845608028654 * (x + 0.044715 * (x * x * x))))


def _sigmoid(x):
    return 1.0 / (1.0 + jnp.exp(-x))


def _router_epilogue(xn, gffn_ref, wrh_ref, wrl_ref, br_ref, h3_ref, lg_ref):
    h = _rms(xn, gffn_ref[...])
    h3_ref[...] = h
    hi = h.astype(BF16)
    lo = (h - hi.astype(F32)).astype(BF16)
    wh = wrh_ref[...]
    lg_ref[...] = _dot(hi, wh) + _dot(lo, wh) + _dot(hi, wrl_ref[...]) + br_ref[...]


def _even_kernel(xc_ref, xp_ref, gmix_ref, win_ref, cw_ref, cb_ref, cng_ref, cnb_ref, vng_ref,
                 vnb_ref, wsp_ref, bsp_ref, wout_ref, gffn_ref, wrh_ref, wrl_ref, br_ref,
                 xo_ref, h3_ref, lg_ref, a_s, mix_s, *, ns, sub, seq):
    j = pl.program_id(1)
    win_rows = CONV_ROWS + 2 * HALO
    n_chunks = sub // CONV_ROWS

    def proj_conv_input():
        h = _rms(xc_ref[...], gmix_ref[...]).astype(BF16)
        pa = _dot(h, win_ref[:, 0:2 * CONV_DIM])
        return h, pa[:, :CONV_DIM] * _sigmoid(pa[:, CONV_DIM:])

    def proj_gating(h, a):
        r0 = pl.multiple_of(j * sub, sub)
        a_s[pl.ds(HALO + r0, sub), :] = a
        u = _gelu(_dot(h, win_ref[:, 2 * CONV_DIM:2 * CONV_DIM + SGU_DIM]))
        v = _gelu(_dot(h, win_ref[:, 2 * CONV_DIM + SGU_DIM:]))
        for hd in range(SGU_DIM // SGU_HEAD):
            cs = slice(hd * SGU_HEAD, (hd + 1) * SGU_HEAD)
            vn = _ln(v[:, cs], vng_ref[:, cs], vnb_ref[:, cs]).astype(BF16)
            for c in range(sub // CHUNK):
                rs = slice(c * CHUNK, (c + 1) * CHUNK)
                sv = _dot(wsp_ref[hd], vn[rs, :]) + bsp_ref[hd]
                mix_s[pl.ds(r0 + c * CHUNK, CHUNK),
                      CONV_DIM + hd * SGU_HEAD:CONV_DIM + (hd + 1) * SGU_HEAD] = (u[rs, cs] * sv).astype(BF16)

    def conv_and_out(next_halo):
        r0 = (j - 1) * sub
        for i in range(n_chunks):
            base = pl.multiple_of(r0 + i * CONV_ROWS, CONV_ROWS)
            if i < n_chunks - 1:
                win = a_s[pl.ds(base, win_rows), :]
            else:
                win = jnp.concatenate([a_s[pl.ds(base, CONV_ROWS + HALO), :], next_halo], axis=0)
            acc = jnp.zeros((CONV_ROWS, CONV_DIM), F32)
            for s in range(8):
                sh = win if s == 0 else pltpu.roll(win, win_rows - s, 0)
                for m in range(4):
                    k = 8 * m + s - 1
                    if 0 <= k < CONV_WIDTH:
                        acc = acc + sh[8 * m:8 * m + CONV_ROWS, :] * cw_ref[k:k + 1, :]
            acc = acc + cb_ref[...]
            for g in range(CONV_DIM // CONV_GROUP):
                cs = slice(g * CONV_GROUP, (g + 1) * CONV_GROUP)
                y = _ln(acc[:, cs], cng_ref[:, cs], cnb_ref[:, cs])
                mix_s[pl.ds(base, CONV_ROWS), cs] = (y * _sigmoid(y)).astype(BF16)
        mix = mix_s[pl.ds(pl.multiple_of(r0, sub), sub), :]
        xn = xp_ref[...] + _dot(mix, wout_ref[...])
        xo_ref[...] = xn
        _router_epilogue(xn, gffn_ref, wrh_ref, wrl_ref, br_ref, h3_ref, lg_ref)

    @pl.when(j == 0)
    def _():
        a_s[0:HALO, :] = jnp.zeros((HALO, CONV_DIM), F32)
        h, a = proj_conv_input()
        proj_gating(h, a)

    @pl.when((j >= 1) & (j < ns))
    def _():
        h, a = proj_conv_input()
        conv_and_out(a[0:HALO, :])
        proj_gating(h, a)

    @pl.when(j == ns)
    def _():
        conv_and_out(jnp.zeros((HALO, CONV_DIM), F32))


def _const_spec(shape):
    nd = len(shape)
    return pl.BlockSpec(shape, lambda *_: (0,) * nd)


def _even_mixer(x, seq, gmix, win, cw, cb, cng, cnb, vng, vnb, wsp, bsp, wout, gffn, wrh, wrl, br):
    t, d = x.shape
    nb = t // seq
    sub = 512
    ns = seq // sub
    row = lambda b, j: (b * ns + jnp.minimum(j, ns - 1), 0)
    prev = lambda b, j: (b * ns + jnp.maximum(j - 1, 0), 0)
    consts = (gmix, win, cw, cb, cng, cnb, vng, vnb, wsp, bsp, wout, gffn, wrh, wrl, br)
    return pl.pallas_call(
        functools.partial(_even_kernel, ns=ns, sub=sub, seq=seq),
        grid=(nb, ns + 1),
        in_specs=[pl.BlockSpec((sub, d), row), pl.BlockSpec((sub, d), prev)]
        + [_const_spec(c.shape) for c in consts],
        out_specs=[pl.BlockSpec((sub, d), prev),
                   pl.BlockSpec((sub, d), prev),
                   pl.BlockSpec((sub, LANE), prev)],
        out_shape=[jax.ShapeDtypeStruct((t, d), F32),
                   jax.ShapeDtypeStruct((t, d), F32),
                   jax.ShapeDtypeStruct((t, LANE), F32)],
        scratch_shapes=[pltpu.VMEM((seq + HALO, CONV_DIM), F32),
                        pltpu.VMEM((seq, CONV_DIM + SGU_DIM), BF16)],
        compiler_params=pltpu.CompilerParams(
            dimension_semantics=("arbitrary", "arbitrary"), vmem_limit_bytes=VMEM_LIMIT),
        name="even_mixer",
    )(x, x, *consts)


def _odd_proj_kernel(x_ref, gmix_ref, win_ref, gcq_ref, wuq_ref, gckv_ref, wuk_ref, wuv_ref, vone_ref,
                     cos_ref, sa_ref, sb_ref, qm_ref, km_ref, vm_ref, qs_ref, ks_ref, vs_ref):
    h = _rms(x_ref[...], gmix_ref[...]).astype(BF16)
    p = _dot(h, win_ref[...])
    o = 0
    cq = p[:, o:o + MLA_Q_RANK]; o += MLA_Q_RANK
    ckv = p[:, o:o + MLA_KV_RANK]; o += MLA_KV_RANK
    kpe = p[:, o:o + LANE]; o += LANE
    qs = p[:, o:o + SWA_HEADS * SWA_DIM]; o += SWA_HEADS * SWA_DIM
    ks = p[:, o:o + LANE]; o += LANE
    vs = p[:, o:o + LANE]

    cos = cos_ref[...]
    sa = sa_ref[...]
    sb = sb_ref[...]

    def rope(z, reps):
        w = z.shape[1]
        c = jnp.concatenate([cos] * reps, axis=1) if reps > 1 else cos
        a = jnp.concatenate([sa] * reps, axis=1) if reps > 1 else sa
        b = jnp.concatenate([sb] * reps, axis=1) if reps > 1 else sb
        return z * c + pltpu.roll(z, w - MLA_ROPE // 2, 1) * a + pltpu.roll(z, MLA_ROPE // 2, 1) * b

    q = _dot(_rms(cq, gcq_ref[...]).astype(BF16), wuq_ref[...])
    qm_ref[...] = (rope(q, MLA_HEADS) * ((MLA_NOPE + MLA_ROPE) ** -0.5 * LOG2E)).astype(BF16)
    ckvn = _rms(ckv, gckv_ref[...]).astype(BF16)
    kr = rope(kpe, 1)
    km_ref[...] = (_dot(ckvn, wuk_ref[...]) + jnp.concatenate([kr] * MLA_HEADS, axis=1)).astype(BF16)
    vm_ref[...] = (_dot(ckvn, wuv_ref[...]) + vone_ref[...]).astype(BF16)
    qs_ref[...] = (qs * (SWA_DIM ** -0.5 * LOG2E)).astype(BF16)
    ks_ref[...] = ks.astype(BF16)
    lane = lax.broadcasted_iota(I32, (vs.shape[0], SWA_DIM), 1)
    one = jnp.where(lane == 0, 1.0, 0.0)
    vs_ref[...] = jnp.concatenate(
        [piece for g in range(SWA_KV_HEADS) for piece in (vs[:, g * SWA_DIM:(g + 1) * SWA_DIM], one)],
        axis=1).astype(BF16)


def _odd_proj(x, seq, gmix, win, gcq, wuq, gckv, wuk, wuv, vone, cos, sa, sb):
    t, d = x.shape
    tm = 512
    nsq = seq // tm
    row = lambda i: (i, 0)
    pos = lambda i: (i % nsq, 0)
    consts = (gmix, win, gcq, wuq, gckv, wuk, wuv, vone)
    widths = (MLA_HEADS * LANE, MLA_HEADS * LANE, MLA_HEADS * LANE, SWA_HEADS * SWA_DIM, LANE,
              SWA_KV_HEADS * LANE)
    return pl.pallas_call(
        _odd_proj_kernel,
        grid=(t // tm,),
        in_specs=[pl.BlockSpec((tm, d), row)] + [_const_spec(c.shape) for c in consts]
        + [pl.BlockSpec((tm, LANE), pos)] * 3,
        out_specs=[pl.BlockSpec((tm, w), row) for w in widths],
        out_shape=[jax.ShapeDtypeStruct((t, w), BF16) for w in widths],
        compiler_params=pltpu.CompilerParams(
            dimension_semantics=("arbitrary",), vmem_limit_bytes=VMEM_LIMIT),
        name="odd_proj",
    )(x, *consts, cos, sa, sb)


def _attn_kernel(sink_ref, x_ref, qm_ref, km_ref, vm_ref, qs_ref, ks_ref, vs_ref, wout_ref,
                 gffn_ref, wrh_ref, wrl_ref, br_ref, xo_ref, h3_ref, lg_ref, *, tq, seq):
    i = pl.program_id(1)
    qm = qm_ref[...]
    outs = []
    for hd in range(MLA_HEADS):
        cs = slice(hd * LANE, (hd + 1) * LANE)
        s = _dot_nt(qm[:, cs], km_ref[:, cs])
        e = jnp.exp2((s - jnp.max(s, axis=-1, keepdims=True)).astype(BF16))
        pv = _dot(e, vm_ref[:, cs])
        outs.append(pv[:, :MLA_V] / pv[:, MLA_V:MLA_V + 1])

    span = LANE + 2 * WINDOW
    rep = SWA_HEADS // SWA_KV_HEADS
    swa_rows = []
    for blk in range(tq // LANE):
        q0 = i * tq + blk * LANE
        start = pl.multiple_of(jnp.clip(q0 - WINDOW, 0, seq - span), LANE)
        kw = ks_ref[pl.ds(start, span), :]
        vw = vs_ref[pl.ds(start, span), :]
        qpos = q0 + lax.broadcasted_iota(I32, (LANE, span), 0)
        kpos = start + lax.broadcasted_iota(I32, (LANE, span), 1)
        absd = jnp.abs(qpos - kpos).astype(F32)
        in_win = absd <= float(WINDOW)
        heads = []
        for g in range(SWA_KV_HEADS):
            hds = range(g * rep, (g + 1) * rep)
            rows = slice(blk * LANE, (blk + 1) * LANE)
            q = jnp.concatenate([qs_ref[rows, hd * SWA_DIM:(hd + 1) * SWA_DIM] for hd in hds], axis=0)
            bias = jnp.concatenate(
                [jnp.where(in_win, (-LOG2E * 2.0 ** -(hd + 1)) * absd, NEG_INF) for hd in hds], axis=0)
            sk = jnp.concatenate([jnp.full((LANE, 1), sink_ref[hd] * LOG2E, F32) for hd in hds], axis=0)
            s = _dot_nt(q, kw[:, g * SWA_DIM:(g + 1) * SWA_DIM]) + bias
            m = jnp.maximum(jnp.max(s, axis=-1, keepdims=True), sk)
            pv = _dot(jnp.exp2((s - m).astype(BF16)), vw[:, g * LANE:(g + 1) * LANE])
            o = pv[:, :SWA_DIM] / (pv[:, SWA_DIM:SWA_DIM + 1] + jnp.exp2(sk - m))
            heads += [o[r * LANE:(r + 1) * LANE] for r in range(rep)]
        swa_rows.append(jnp.concatenate(heads, axis=1))
    swa = jnp.concatenate(swa_rows, axis=0) if len(swa_rows) > 1 else swa_rows[0]

    mix = jnp.concatenate(outs + [swa], axis=1).astype(BF16)
    xn = x_ref[...] + _dot(mix, wout_ref[...])
    xo_ref[...] = xn
    _router_epilogue(xn, gffn_ref, wrh_ref, wrl_ref, br_ref, h3_ref, lg_ref)


def _attention(x, seq, sink, qm, km, vm, qs, ks, vs, wout, gffn, wrh, wrl, br):
    t, d = x.shape
    nb = t // seq
    tq = 512
    nq = seq // tq
    row = lambda b, i, *_: (b * nq + i, 0)
    bat = lambda b, i, *_: (b, 0)
    consts = (wout, gffn, wrh, wrl, br)
    grid_spec = pltpu.PrefetchScalarGridSpec(
        num_scalar_prefetch=1,
        grid=(nb, nq),
        in_specs=[pl.BlockSpec((tq, d), row),
                  pl.BlockSpec((tq, qm.shape[1]), row),
                  pl.BlockSpec((seq, km.shape[1]), bat),
                  pl.BlockSpec((seq, vm.shape[1]), bat),
                  pl.BlockSpec((tq, qs.shape[1]), row),
                  pl.BlockSpec((seq, ks.shape[1]), bat),
                  pl.BlockSpec((seq, vs.shape[1]), bat)]
        + [pl.BlockSpec(c.shape, lambda b, i, *_, n=len(c.shape): (0,) * n) for c in consts],
        out_specs=[pl.BlockSpec((tq, d), row),
                   pl.BlockSpec((tq, d), row),
                   pl.BlockSpec((tq, LANE), row)],
    )
    return pl.pallas_call(
        functools.partial(_attn_kernel, tq=tq, seq=seq),
        grid_spec=grid_spec,
        out_shape=[jax.ShapeDtypeStruct((t, d), F32),
                   jax.ShapeDtypeStruct((t, d), F32),
                   jax.ShapeDtypeStruct((t, LANE), F32)],
        compiler_params=pltpu.CompilerParams(
            dimension_semantics=("arbitrary", "arbitrary"), vmem_limit_bytes=VMEM_LIMIT),
        name="attention",
    )(sink, x, qm, km, vm, qs, ks, vs, *consts)


def _route_kernel(lg_ref, lower_ref, out_ref, cnt_ref, carry_ref):
    i = pl.program_id(0)
    rows = lg_ref.shape[0]
    lg = lg_ref[...]
    lane = lax.broadcasted_iota(I32, (rows, LANE), 1)
    ninf = -jnp.inf

    @pl.when(i == 0)
    def _():
        carry_ref[...] = jnp.zeros_like(carry_ref)

    gl = jnp.where(lane < N_GROUPS, lg, ninf)
    gmax = jnp.max(gl, axis=1, keepdims=True)
    gidx = jnp.min(jnp.where(gl == gmax, lane, LANE), axis=1, keepdims=True)
    gp = 1.0 / jnp.sum(jnp.exp(gl - gmax), axis=1, keepdims=True)

    first = N_GROUPS + gidx * N_EXP
    el = jnp.where((lane >= first) & (lane < first + N_EXP), lg, ninf)
    m1 = jnp.max(el, axis=1, keepdims=True)
    i1 = jnp.min(jnp.where(el == m1, lane, LANE), axis=1, keepdims=True)
    el2 = jnp.where(lane == i1, ninf, el)
    m2 = jnp.max(el2, axis=1, keepdims=True)
    i2 = jnp.min(jnp.where(el2 == m2, lane, LANE), axis=1, keepdims=True)
    e2 = jnp.exp(m2 - m1)
    w1 = gp / (1.0 + e2)
    w2 = gp * e2 / (1.0 + e2)
    id1 = i1 - N_GROUPS
    id2 = i2 - N_GROUPS
    onehot = jnp.where((lane == id1) | (lane == id2), 1.0, 0.0)

    before = _dot(lower_ref[...], onehot.astype(BF16)) + carry_ref[...]
    rank1 = jnp.sum(jnp.where(lane == id1, before, 0.0), axis=1, keepdims=True)
    rank2 = jnp.sum(jnp.where(lane == id2, before, 0.0), axis=1, keepdims=True)
    carry_ref[...] += jnp.sum(onehot, axis=0, keepdims=True)
    out = jnp.where(lane == 0, rank1, 0.0)
    out = jnp.where(lane == 1, rank2, out)
    out = jnp.where(lane == 2, w1, out)
    out = jnp.where(lane == 3, w2, out)
    out = jnp.where(lane == 4, id1.astype(F32), out)
    out = jnp.where(lane == 5, id2.astype(F32), out)
    out_ref[...] = out
    cnt_ref[...] = carry_ref[...]


def _route(lg):
    t = lg.shape[0]
    rows = 512
    lower = jnp.tri(rows, k=-1, dtype=BF16)
    return pl.pallas_call(
        _route_kernel,
        grid=(t // rows,),
        in_specs=[pl.BlockSpec((rows, LANE), lambda i: (i, 0)),
                  pl.BlockSpec((rows, rows), lambda i: (0, 0))],
        out_specs=[pl.BlockSpec((rows, LANE), lambda i: (i, 0)),
                   pl.BlockSpec((1, LANE), lambda i: (0, 0))],
        out_shape=[jax.ShapeDtypeStruct((t, LANE), F32), jax.ShapeDtypeStruct((1, LANE), F32)],
        scratch_shapes=[pltpu.VMEM((1, LANE), F32)],
        compiler_params=pltpu.CompilerParams(dimension_semantics=("arbitrary",)),
        name="moe_route",
    )(lg, lower)


def _row(ref, r):
    return ref.at[pl.ds(r, 1), :]


def _dispatch_kernel(pos_ref, h_ref, xs_ref, sem):
    i = pl.program_id(0)
    rows = h_ref.shape[0]

    for r in range(rows):
        t = i * rows + r
        pltpu.make_async_copy(_row(h_ref, r), _row(xs_ref, pos_ref[2 * t]), sem).start(priority=0)
        pltpu.make_async_copy(_row(h_ref, r), _row(xs_ref, pos_ref[2 * t + 1]), sem).start(priority=1)
    for _ in range(2):
        pltpu.make_async_copy(h_ref, xs_ref.at[pl.ds(0, rows), :], sem).wait()


def _dispatch(pos, h):
    t, d = h.shape
    grid_spec = pltpu.PrefetchScalarGridSpec(
        num_scalar_prefetch=1,
        grid=(t // DISPATCH_TILE,),
        in_specs=[pl.BlockSpec((DISPATCH_TILE, d), lambda i, *_: (i, 0))],
        out_specs=pl.BlockSpec(memory_space=pl.ANY),
        scratch_shapes=[pltpu.SemaphoreType.DMA],
    )
    return pl.pallas_call(
        _dispatch_kernel,
        grid_spec=grid_spec,
        out_shape=jax.ShapeDtypeStruct((2 * t, d), F32),
        compiler_params=pltpu.CompilerParams(dimension_semantics=("arbitrary",)),
        name="moe_dispatch",
    )(pos, h)


def _expert_kernel(tile_ref, exp_ref, lo_ref, hi_ref, xs_ref, wg_ref, wu_ref, wd_ref, ys_ref,
                   xbuf, sems, wg_s, wu_s, wd_s, *, n_tiles):
    i = pl.program_id(0)
    prev = jnp.maximum(i - 1, 0)
    tile = tile_ref[i]
    new_tile = (i == 0) | (tile != tile_ref[prev])
    new_exp = (i == 0) | (exp_ref[i] != exp_ref[prev])
    lo = lo_ref[i]
    hi = hi_ref[i]
    rows = ys_ref.shape[0]
    slot = tile % XBUF_SLOTS

    def fetch(tl):
        start = tl * rows if isinstance(tl, int) else pl.multiple_of(tl * rows, rows)
        return pltpu.make_async_copy(xs_ref.at[pl.ds(start, rows), :], xbuf.at[tl % XBUF_SLOTS],
                                     sems.at[tl % XBUF_SLOTS])

    @pl.when(i == 0)
    def _():
        for tl in range(min(XBUF_SLOTS - 1, n_tiles)):
            fetch(tl).start()

    @pl.when(new_tile)
    def _():
        fetch(tile).wait()

        @pl.when(tile + XBUF_SLOTS - 1 < n_tiles)
        def _():
            fetch(tile + XBUF_SLOTS - 1).start()

    @pl.when(new_exp)
    def _():
        wg_s[...] = wg_ref[0].astype(BF16)
        wu_s[...] = wu_ref[0].astype(BF16)
        wd_s[...] = wd_ref[0].astype(BF16)

    @pl.when(new_tile)
    def _():
        ys_ref[...] = jnp.zeros_like(ys_ref)

    @pl.when(hi > lo)
    def _():
        x = xbuf[slot].astype(BF16)
        g = _dot(x, wg_s[...])
        u = _dot(x, wu_s[...])
        hid = (g * _sigmoid(g) * u).astype(BF16)
        y = _dot(hid, wd_s[...])
        r = tile * rows + lax.broadcasted_iota(I32, (rows, 1), 0)
        ys_ref[...] += jnp.where((r >= lo) & (r < hi), y, 0.0)


def _experts(work, first_expert, xs, wg, wu, wd):
    n, d = xs.shape
    f = wg.shape[2]
    nwork = work[0].shape[0]
    wmap = lambda i, tl, ex, lo, hi: (first_expert + ex[i], 0, 0)
    grid_spec = pltpu.PrefetchScalarGridSpec(
        num_scalar_prefetch=4,
        grid=(nwork,),
        in_specs=[pl.BlockSpec(memory_space=pl.ANY),
                  pl.BlockSpec((1, d, f), wmap),
                  pl.BlockSpec((1, d, f), wmap),
                  pl.BlockSpec((1, f, d), wmap)],
        out_specs=pl.BlockSpec((MOE_TILE, d), lambda i, tl, ex, lo, hi: (tl[i], 0)),
        scratch_shapes=[pltpu.VMEM((XBUF_SLOTS, MOE_TILE, d), F32), pltpu.SemaphoreType.DMA((XBUF_SLOTS,)),
                        pltpu.VMEM((d, f), BF16), pltpu.VMEM((d, f), BF16), pltpu.VMEM((f, d), BF16)],
    )
    return pl.pallas_call(
        functools.partial(_expert_kernel, n_tiles=n // MOE_TILE),
        grid_spec=grid_spec,
        out_shape=jax.ShapeDtypeStruct((n, d), F32),
        compiler_params=pltpu.CompilerParams(
            dimension_semantics=("arbitrary",), vmem_limit_bytes=VMEM_LIMIT),
        name="moe_experts",
    )(*work, xs, wg, wu, wd)


def _combine_kernel(pos_ref, x_ref, rw_ref, gfin_ref, ys_ref, xo_ref, buf, sems, *, final):
    i = pl.program_id(0)
    n = pl.num_programs(0)
    rows = x_ref.shape[0]
    slot = i % 2

    def issue(step, slt):
        for r in range(rows):
            t = step * rows + r
            pltpu.make_async_copy(_row(ys_ref, pos_ref[2 * t]), _row(buf.at[slt], r),
                                  sems.at[slt]).start(priority=0)
            pltpu.make_async_copy(_row(ys_ref, pos_ref[2 * t + 1]), _row(buf.at[slt], rows + r),
                                  sems.at[slt]).start(priority=1)

    @pl.when(i == 0)
    def _():
        issue(0, 0)

    for slt in range(2):
        @pl.when((i + 1 < n) & (slot == 1 - slt))
        def _():
            issue(i + 1, slt)

    pltpu.make_async_copy(ys_ref.at[pl.ds(0, 2 * rows), :], buf.at[slot], sems.at[slot]).wait()

    rw = rw_ref[...]
    xn = x_ref[...] + rw[:, 2:3] * buf[slot, 0:rows, :] + rw[:, 3:4] * buf[slot, rows:2 * rows, :]
    if final:
        xn = _rms(xn, gfin_ref[...])
    xo_ref[...] = xn


def _combine(pos, x, rw, gfin, ys, final):
    t, d = x.shape
    grid_spec = pltpu.PrefetchScalarGridSpec(
        num_scalar_prefetch=1,
        grid=(t // ROW_TILE,),
        in_specs=[pl.BlockSpec((ROW_TILE, d), lambda i, *_: (i, 0)),
                  pl.BlockSpec((ROW_TILE, LANE), lambda i, *_: (i, 0)),
                  pl.BlockSpec((1, d), lambda i, *_: (0, 0)),
                  pl.BlockSpec(memory_space=pl.ANY)],
        out_specs=pl.BlockSpec((ROW_TILE, d), lambda i, *_: (i, 0)),
        scratch_shapes=[pltpu.VMEM((2, 2 * ROW_TILE, d), F32), pltpu.SemaphoreType.DMA((2,))],
    )
    return pl.pallas_call(
        functools.partial(_combine_kernel, final=final),
        grid_spec=grid_spec,
        out_shape=jax.ShapeDtypeStruct((t, d), F32),
        compiler_params=pltpu.CompilerParams(
            dimension_semantics=("arbitrary",), vmem_limit_bytes=VMEM_LIMIT),
        name="moe_combine",
    )(pos, x, rw, gfin, ys)


def _worklist(cnt, n_rows):
    ends = jnp.cumsum(cnt)
    starts = ends - cnt
    n_tiles = n_rows // MOE_TILE
    tile_lo = jnp.arange(n_tiles, dtype=I32) * MOE_TILE
    exp_lo = starts[1:]
    rank_t = jnp.arange(n_tiles, dtype=I32) + jnp.sum(exp_lo[None, :] < tile_lo[:, None], axis=1)
    rank_e = jnp.arange(N_EXPERTS - 1, dtype=I32) + jnp.sum(tile_lo[None, :] <= exp_lo[:, None], axis=1)
    vals = jnp.concatenate([tile_lo, exp_lo])
    ranks = jnp.concatenate([rank_t, rank_e])
    slot = jnp.arange(vals.shape[0], dtype=I32)
    lo = jnp.sum(jnp.where(ranks[None, :] == slot[:, None], vals[None, :], 0), axis=1)
    hi = jnp.concatenate([lo[1:], jnp.full((1,), n_rows, I32)])
    tile = jnp.minimum(lo // MOE_TILE, n_tiles - 1)
    expert = jnp.minimum(jnp.sum(ends[None, :] <= lo[:, None], axis=1), N_EXPERTS - 1).astype(I32)
    return tile, expert, lo, hi


def _moe(x, h, lg, wg, wu, wd, layer, gfin, final):
    t = x.shape[0]
    rw, cnt = _route(lg)
    cnt = cnt[0, :N_EXPERTS].astype(I32)
    starts = jnp.cumsum(cnt) - cnt
    ids = rw[:, 4:6].astype(I32)
    offs = jnp.sum(jnp.where(ids[:, :, None] == jnp.arange(N_EXPERTS, dtype=I32), starts, 0), axis=-1)
    pos = (offs + rw[:, 0:2].astype(I32)).reshape(2 * t)
    xs = _dispatch(pos, h)
    work = _worklist(cnt, 2 * t)
    ys = _experts(work, layer * N_EXPERTS, xs, wg, wu, wd)
    return _combine(pos, x, rw, gfin, ys, final)


def _router_weights(w_group, b_group, w_router, b_router):
    d = w_group.shape[0]
    w = jnp.concatenate([w_group] + [w_router[g] for g in range(N_GROUPS)], axis=1)
    w = jnp.pad(w, ((0, 0), (0, LANE - w.shape[1])))
    b = jnp.concatenate([b_group, b_router.reshape(-1)])
    b = jnp.pad(b, (0, LANE - b.shape[0])).reshape(1, LANE)
    hi = w.astype(BF16)
    lo = (w - hi.astype(F32)).astype(BF16)
    return hi, lo, b


def _rope_tables(seq):
    half = MLA_ROPE // 2
    pos = jnp.arange(seq, dtype=F32)
    inv = 1.0 / (ROPE_THETA ** (jnp.arange(0, MLA_ROPE, 2, dtype=F32) / MLA_ROPE))
    ang = pos[:, None] * inv[None, :]
    cos, sin = jnp.cos(ang), jnp.sin(ang)
    z = lambda n: jnp.zeros((seq, n), F32)
    tail = LANE - MLA_NOPE - MLA_ROPE
    ctab = jnp.concatenate([jnp.ones((seq, MLA_NOPE), F32), cos, cos, z(tail)], axis=1)
    atab = jnp.concatenate([z(MLA_NOPE), -sin, z(half), z(tail)], axis=1)
    btab = jnp.concatenate([z(MLA_NOPE), z(half), sin, z(tail)], axis=1)
    return ctab, atab, btab


def _odd_weights(w_in, w_uq, w_ukv):
    d = w_in.shape[0]
    sizes = (MLA_Q_RANK, MLA_KV_RANK, MLA_ROPE, SWA_HEADS * SWA_DIM, SWA_KV_HEADS * SWA_DIM,
             SWA_KV_HEADS * SWA_DIM)
    offs = [0]
    for s in sizes:
        offs.append(offs[-1] + s)
    cq, ckv, kpe, qs, ks, vs = [w_in[:, offs[k]:offs[k + 1]] for k in range(6)]
    kpe = jnp.pad(kpe, ((0, 0), (MLA_NOPE, LANE - MLA_NOPE - MLA_ROPE)))
    win = jnp.concatenate([cq, ckv, kpe, qs, ks, vs], axis=1).astype(BF16)
    hq = MLA_NOPE + MLA_ROPE
    wuq = jnp.pad(w_uq.reshape(MLA_Q_RANK, MLA_HEADS, hq), ((0, 0), (0, 0), (0, LANE - hq)))
    wuq = wuq.reshape(MLA_Q_RANK, MLA_HEADS * LANE).astype(BF16)
    wkv = w_ukv.reshape(MLA_KV_RANK, MLA_HEADS, MLA_NOPE + MLA_V)
    wuk = jnp.pad(wkv[:, :, :MLA_NOPE], ((0, 0), (0, 0), (0, LANE - MLA_NOPE)))
    wuk = wuk.reshape(MLA_KV_RANK, MLA_HEADS * LANE).astype(BF16)
    wuv = jnp.pad(wkv[:, :, MLA_NOPE:], ((0, 0), (0, 0), (0, LANE - MLA_V)))
    wuv = wuv.reshape(MLA_KV_RANK, MLA_HEADS * LANE).astype(BF16)
    vone = jnp.tile(jnp.zeros((LANE,), F32).at[MLA_V].set(1.0), MLA_HEADS).reshape(1, MLA_HEADS * LANE)
    return win, wuq, wuk, wuv, vone


def kernel(x, norm_mix, norm_ffn, norm_final, ev_w_in, ev_conv_w, ev_conv_b, ev_cnorm_g, ev_cnorm_b,
           ev_vnorm_g, ev_vnorm_b, ev_w_sp, ev_b_sp, ev_w_out, od_w_in, od_g_cq, od_w_uq, od_g_ckv,
           od_w_ukv, od_sink, od_w_out, moe_w_group, moe_b_group, moe_w_router, moe_b_router,
           moe_w_gate, moe_w_up, moe_w_down):
    b, seq, d = x.shape
    t = b * seq
    depth = norm_mix.shape[0]
    row = lambda v: v.reshape(1, -1)
    xt = x.reshape(t, d)
    gfin = row(norm_final)
    wg_all = moe_w_gate.reshape(depth * N_EXPERTS, d, -1)
    wu_all = moe_w_up.reshape(depth * N_EXPERTS, d, -1)
    wd_all = moe_w_down.reshape(depth * N_EXPERTS, -1, d)
    for layer in range(depth):
        k = layer // 2
        wrh, wrl, br = _router_weights(moe_w_group[layer], moe_b_group[layer], moe_w_router[layer],
                                       moe_b_router[layer])
        gffn = row(norm_ffn[layer])
        if layer % 2 == 0:
            bsp = jnp.broadcast_to(ev_b_sp[k][:, :, None], ev_w_sp[k].shape)
            xt, h3, lg = _even_mixer(
                xt, seq, row(norm_mix[layer]), ev_w_in[k].astype(BF16), ev_conv_w[k], row(ev_conv_b[k]),
                row(ev_cnorm_g[k]), row(ev_cnorm_b[k]), row(ev_vnorm_g[k]), row(ev_vnorm_b[k]),
                ev_w_sp[k].astype(BF16), bsp, ev_w_out[k].astype(BF16), gffn, wrh, wrl, br)
        else:
            win, wuq, wuk, wuv, vone = _odd_weights(od_w_in[k], od_w_uq[k], od_w_ukv[k])
            ctab, atab, btab = _rope_tables(seq)
            qm, km, vm, qs, ks, vs = _odd_proj(
                xt, seq, row(norm_mix[layer]), win, row(od_g_cq[k]), wuq, row(od_g_ckv[k]), wuk, wuv,
                vone, ctab, atab, btab)
            xt, h3, lg = _attention(xt, seq, od_sink[k], qm, km, vm, qs, ks, vs,
                                    od_w_out[k].astype(BF16), gffn, wrh, wrl, br)
        xt = _moe(xt, h3, lg, wg_all, wu_all, wd_all, layer, gfin, layer == depth - 1)
    return xt.reshape(b, seq, d)
```

```python
import functools

import jax
import jax.numpy as jnp
from jax import lax
from jax.experimental import pallas as pl
from jax.experimental.pallas import tpu as pltpu

F32 = jnp.float32
BF16 = jnp.bfloat16
I32 = jnp.int32

EPS = 1e-6
NEG_INF = -1e30
LOG2E = 1.4426950408889634
LANE = 128
VMEM_LIMIT = 56 * 1024 * 1024

CONV_DIM = 512
CONV_GROUP = 128
CONV_WIDTH = 31
SGU_DIM = 512
SGU_HEAD = 128
CHUNK = 128
MLA_HEADS = 8
MLA_Q_RANK = 256
MLA_KV_RANK = 128
MLA_NOPE = 64
MLA_ROPE = 32
MLA_V = 64
ROPE_THETA = 10000.0
SWA_HEADS = 8
SWA_KV_HEADS = 2
SWA_DIM = 64
WINDOW = 128
N_GROUPS = 4
N_EXP = 8
N_EXPERTS = N_GROUPS * N_EXP

HALO = 16
CONV_ROWS = 64
MOE_TILE = 512
MOE_SUBTILE = 256
XBUF_SLOTS = 3
ROW_TILE = 256
TAB_ROWS = 8
DISPATCH_TILE = 512


def _dot(a, b):
    return jnp.dot(a, b, preferred_element_type=F32)


def _dot_nt(a, b):
    return lax.dot_general(a, b, (((1,), (1,)), ((), ())), preferred_element_type=F32)


def _rms(x, g):
    return x * lax.rsqrt(jnp.mean(x * x, axis=-1, keepdims=True) + EPS) * g


def _ln(x, g, b):
    mu = jnp.mean(x, axis=-1, keepdims=True)
    d = x - mu
    var = jnp.mean(d * d, axis=-1, keepdims=True)
    return d * lax.rsqrt(var + EPS) * g + b


def _gelu(x):
    return 0.5 * x * (1.0 + jnp.tanh(0.7978845608028654 * (x + 0.044715 * (x * x * x))))


def _sigmoid(x):
    return 1.0 / (1.0 + jnp.exp(-x))


def _router_epilogue(xn, gffn_ref, wrh_ref, wrl_ref, br_ref, h3_ref, lg_ref):
    h = _rms(xn, gffn_ref[...])
    h3_ref[...] = h
    hi = h.astype(BF16)
    lo = (h - hi.astype(F32)).astype(BF16)
    wh = wrh_ref[...]
    lg_ref[...] = _dot(hi, wh) + _dot(lo, wh) + _dot(hi, wrl_ref[...]) + br_ref[...]


def _even_kernel(xc_ref, xp_ref, gmix_ref, win_ref, cw_ref, cb_ref, cng_ref, cnb_ref, vng_ref,
                 vnb_ref, wsp_ref, bsp_ref, wout_ref, gffn_ref, wrh_ref, wrl_ref, br_ref,
                 xo_ref, h3_ref, lg_ref, a_s, mix_s, *, ns, sub, seq):
    j = pl.program_id(1)
    win_rows = CONV_ROWS + 2 * HALO
    n_chunks = sub // CONV_ROWS

    def proj_conv_input():
        h = _rms(xc_ref[...], gmix_ref[...]).astype(BF16)
        pa = _dot(h, win_ref[:, 0:2 * CONV_DIM])
        return h, pa[:, :CONV_DIM] * _sigmoid(pa[:, CONV_DIM:])

    def proj_gating(h, a):
        r0 = pl.multiple_of(j * sub, sub)
        a_s[pl.ds(HALO + r0, sub), :] = a
        u = _gelu(_dot(h, win_ref[:, 2 * CONV_DIM:2 * CONV_DIM + SGU_DIM]))
        v = _gelu(_dot(h, win_ref[:, 2 * CONV_DIM + SGU_DIM:]))
        for hd in range(SGU_DIM // SGU_HEAD):
            cs = slice(hd * SGU_HEAD, (hd + 1) * SGU_HEAD)
            vn = _ln(v[:, cs], vng_ref[:, cs], vnb_ref[:, cs]).astype(BF16)
            for c in range(sub // CHUNK):
                rs = slice(c * CHUNK, (c + 1) * CHUNK)
                sv = _dot(wsp_ref[hd], vn[rs, :]) + bsp_ref[hd]
                mix_s[pl.ds(r0 + c * CHUNK, CHUNK),
                      CONV_DIM + hd * SGU_HEAD:CONV_DIM + (hd + 1) * SGU_HEAD] = (u[rs, cs] * sv).astype(BF16)

    def conv_and_out(next_halo):
        r0 = (j - 1) * sub
        for i in range(n_chunks):
            base = pl.multiple_of(r0 + i * CONV_ROWS, CONV_ROWS)
            if i < n_chunks - 1:
                win = a_s[pl.ds(base, win_rows), :]
            else:
                win = jnp.concatenate([a_s[pl.ds(base, CONV_ROWS + HALO), :], next_halo], axis=0)
            acc = jnp.zeros((CONV_ROWS, CONV_DIM), F32)
            for s in range(8):
                sh = win if s == 0 else pltpu.roll(win, win_rows - s, 0)
                for m in range(4):
                    k = 8 * m + s - 1
                    if 0 <= k < CONV_WIDTH:
                        acc = acc + sh[8 * m:8 * m + CONV_ROWS, :] * cw_ref[k:k + 1, :]
            acc = acc + cb_ref[...]
            for g in range(CONV_DIM // CONV_GROUP):
                cs = slice(g * CONV_GROUP, (g + 1) * CONV_GROUP)
                y = _ln(acc[:, cs], cng_ref[:, cs], cnb_ref[:, cs])
                mix_s[pl.ds(base, CONV_ROWS), cs] = (y * _sigmoid(y)).astype(BF16)
        mix = mix_s[pl.ds(pl.multiple_of(r0, sub), sub), :]
        xn = xp_ref[...] + _dot(mix, wout_ref[...])
        xo_ref[...] = xn
        _router_epilogue(xn, gffn_ref, wrh_ref, wrl_ref, br_ref, h3_ref, lg_ref)

    @pl.when(j == 0)
    def _():
        a_s[0:HALO, :] = jnp.zeros((HALO, CONV_DIM), F32)
        h, a = proj_conv_input()
        proj_gating(h, a)

    @pl.when((j >= 1) & (j < ns))
    def _():
        h, a = proj_conv_input()
        conv_and_out(a[0:HALO, :])
        proj_gating(h, a)

    @pl.when(j == ns)
    def _():
        conv_and_out(jnp.zeros((HALO, CONV_DIM), F32))


def _const_spec(shape):
    nd = len(shape)
    return pl.BlockSpec(shape, lambda *_: (0,) * nd)


def _even_mixer(x, seq, gmix, win, cw, cb, cng, cnb, vng, vnb, wsp, bsp, wout, gffn, wrh, wrl, br):
    t, d = x.shape
    nb = t // seq
    sub = 512
    ns = seq // sub
    row = lambda b, j: (b * ns + jnp.minimum(j, ns - 1), 0)
    prev = lambda b, j: (b * ns + jnp.maximum(j - 1, 0), 0)
    consts = (gmix, win, cw, cb, cng, cnb, vng, vnb, wsp, bsp, wout, gffn, wrh, wrl, br)
    return pl.pallas_call(
        functools.partial(_even_kernel, ns=ns, sub=sub, seq=seq),
        grid=(nb, ns + 1),
        in_specs=[pl.BlockSpec((sub, d), row), pl.BlockSpec((sub, d), prev)]
        + [_const_spec(c.shape) for c in consts],
        out_specs=[pl.BlockSpec((sub, d), prev),
                   pl.BlockSpec((sub, d), prev),
                   pl.BlockSpec((sub, LANE), prev)],
        out_shape=[jax.ShapeDtypeStruct((t, d), F32),
                   jax.ShapeDtypeStruct((t, d), F32),
                   jax.ShapeDtypeStruct((t, LANE), F32)],
        scratch_shapes=[pltpu.VMEM((seq + HALO, CONV_DIM), F32),
                        pltpu.VMEM((seq, CONV_DIM + SGU_DIM), BF16)],
        compiler_params=pltpu.CompilerParams(
            dimension_semantics=("arbitrary", "arbitrary"), vmem_limit_bytes=VMEM_LIMIT),
        name="even_mixer",
    )(x, x, *consts)


def _odd_proj_kernel(x_ref, gmix_ref, win_ref, gcq_ref, wuq_ref, gckv_ref, wuk_ref, wuv_ref, vone_ref,
                     cos_ref, sa_ref, sb_ref, qm_ref, km_ref, vm_ref, qs_ref, ks_ref, vs_ref):
    h = _rms(x_ref[...], gmix_ref[...]).astype(BF16)
    p = _dot(h, win_ref[...])
    o = 0
    cq = p[:, o:o + MLA_Q_RANK]; o += MLA_Q_RANK
    ckv = p[:, o:o + MLA_KV_RANK]; o += MLA_KV_RANK
    kpe = p[:, o:o + LANE]; o += LANE
    qs = p[:, o:o + SWA_HEADS * SWA_DIM]; o += SWA_HEADS * SWA_DIM
    ks = p[:, o:o + LANE]; o += LANE
    vs = p[:, o:o + LANE]

    cos = cos_ref[...]
    sa = sa_ref[...]
    sb = sb_ref[...]

    def rope(z, reps):
        w = z.shape[1]
        c = jnp.concatenate([cos] * reps, axis=1) if reps > 1 else cos
        a = jnp.concatenate([sa] * reps, axis=1) if reps > 1 else sa
        b = jnp.concatenate([sb] * reps, axis=1) if reps > 1 else sb
        return z * c + pltpu.roll(z, w - MLA_ROPE // 2, 1) * a + pltpu.roll(z, MLA_ROPE // 2, 1) * b

    q = _dot(_rms(cq, gcq_ref[...]).astype(BF16), wuq_ref[...])
    qm_ref[...] = (rope(q, MLA_HEADS) * ((MLA_NOPE + MLA_ROPE) ** -0.5 * LOG2E)).astype(BF16)
    ckvn = _rms(ckv, gckv_ref[...]).astype(BF16)
    kr = rope(kpe, 1)
    km_ref[...] = (_dot(ckvn, wuk_ref[...]) + jnp.concatenate([kr] * MLA_HEADS, axis=1)).astype(BF16)
    vm_ref[...] = (_dot(ckvn, wuv_ref[...]) + vone_ref[...]).astype(BF16)
    qs_ref[...] = (qs * (SWA_DIM ** -0.5 * LOG2E)).astype(BF16)
    ks_ref[...] = ks.astype(BF16)
    lane = lax.broadcasted_iota(I32, (vs.shape[0], SWA_DIM), 1)
    one = jnp.where(lane == 0, 1.0, 0.0)
    vs_ref[...] = jnp.concatenate(
        [piece for g in range(SWA_KV_HEADS) for piece in (vs[:, g * SWA_DIM:(g + 1) * SWA_DIM], one)],
        axis=1).astype(BF16)


def _odd_proj(x, seq, gmix, win, gcq, wuq, gckv, wuk, wuv, vone, cos, sa, sb):
    t, d = x.shape
    tm = 512
    nsq = seq // tm
    row = lambda i: (i, 0)
    pos = lambda i: (i % nsq, 0)
    consts = (gmix, win, gcq, wuq, gckv, wuk, wuv, vone)
    widths = (MLA_HEADS * LANE, MLA_HEADS * LANE, MLA_HEADS * LANE, SWA_HEADS * SWA_DIM, LANE,
              SWA_KV_HEADS * LANE)
    return pl.pallas_call(
        _odd_proj_kernel,
        grid=(t // tm,),
        in_specs=[pl.BlockSpec((tm, d), row)] + [_const_spec(c.shape) for c in consts]
        + [pl.BlockSpec((tm, LANE), pos)] * 3,
        out_specs=[pl.BlockSpec((tm, w), row) for w in widths],
        out_shape=[jax.ShapeDtypeStruct((t, w), BF16) for w in widths],
        compiler_params=pltpu.CompilerParams(
            dimension_semantics=("arbitrary",), vmem_limit_bytes=VMEM_LIMIT),
        name="odd_proj",
    )(x, *consts, cos, sa, sb)


def _attn_kernel(sink_ref, x_ref, qm_ref, km_ref, vm_ref, qs_ref, ks_ref, vs_ref, wout_ref,
                 gffn_ref, wrh_ref, wrl_ref, br_ref, xo_ref, h3_ref, lg_ref, *, tq, seq):
    i = pl.program_id(1)
    qm = qm_ref[...]
    outs = []
    for hd in range(MLA_HEADS):
        cs = slice(hd * LANE, (hd + 1) * LANE)
        s = _dot_nt(qm[:, cs], km_ref[:, cs])
        e = jnp.exp2((s - jnp.max(s, axis=-1, keepdims=True)).astype(BF16))
        pv = _dot(e, vm_ref[:, cs])
        outs.append(pv[:, :MLA_V] / pv[:, MLA_V:MLA_V + 1])

    span = LANE + 2 * WINDOW
    rep = SWA_HEADS // SWA_KV_HEADS
    swa_rows = []
    for blk in range(tq // LANE):
        q0 = i * tq + blk * LANE
        start = pl.multiple_of(jnp.clip(q0 - WINDOW, 0, seq - span), LANE)
        kw = ks_ref[pl.ds(start, span), :]
        vw = vs_ref[pl.ds(start, span), :]
        qpos = q0 + lax.broadcasted_iota(I32, (LANE, span), 0)
        kpos = start + lax.broadcasted_iota(I32, (LANE, span), 1)
        absd = jnp.abs(qpos - kpos).astype(F32)
        in_win = absd <= float(WINDOW)
        heads = []
        for g in range(SWA_KV_HEADS):
            hds = range(g * rep, (g + 1) * rep)
            rows = slice(blk * LANE, (blk + 1) * LANE)
            q = jnp.concatenate([qs_ref[rows, hd * SWA_DIM:(hd + 1) * SWA_DIM] for hd in hds], axis=0)
            bias = jnp.concatenate(
                [jnp.where(in_win, (-LOG2E * 2.0 ** -(hd + 1)) * absd, NEG_INF) for hd in hds], axis=0)
            sk = jnp.concatenate([jnp.full((LANE, 1), sink_ref[hd] * LOG2E, F32) for hd in hds], axis=0)
            s = _dot_nt(q, kw[:, g * SWA_DIM:(g + 1) * SWA_DIM]) + bias
            m = jnp.maximum(jnp.max(s, axis=-1, keepdims=True), sk)
            pv = _dot(jnp.exp2((s - m).astype(BF16)), vw[:, g * LANE:(g + 1) * LANE])
            o = pv[:, :SWA_DIM] / (pv[:, SWA_DIM:SWA_DIM + 1] + jnp.exp2(sk - m))
            heads += [o[r * LANE:(r + 1) * LANE] for r in range(rep)]
        swa_rows.append(jnp.concatenate(heads, axis=1))
    swa = jnp.concatenate(swa_rows, axis=0) if len(swa_rows) > 1 else swa_rows[0]

    mix = jnp.concatenate(outs + [swa], axis=1).astype(BF16)
    xn = x_ref[...] + _dot(mix, wout_ref[...])
    xo_ref[...] = xn
    _router_epilogue(xn, gffn_ref, wrh_ref, wrl_ref, br_ref, h3_ref, lg_ref)


def _attention(x, seq, sink, qm, km, vm, qs, ks, vs, wout, gffn, wrh, wrl, br):
    t, d = x.shape
    nb = t // seq
    tq = 512
    nq = seq // tq
    row = lambda b, i, *_: (b * nq + i, 0)
    bat = lambda b, i, *_: (b, 0)
    consts = (wout, gffn, wrh, wrl, br)
    grid_spec = pltpu.PrefetchScalarGridSpec(
        num_scalar_prefetch=1,
        grid=(nb, nq),
        in_specs=[pl.BlockSpec((tq, d), row),
                  pl.BlockSpec((tq, qm.shape[1]), row),
                  pl.BlockSpec((seq, km.shape[1]), bat),
                  pl.BlockSpec((seq, vm.shape[1]), bat),
                  pl.BlockSpec((tq, qs.shape[1]), row),
                  pl.BlockSpec((seq, ks.shape[1]), bat),
                  pl.BlockSpec((seq, vs.shape[1]), bat)]
        + [pl.BlockSpec(c.shape, lambda b, i, *_, n=len(c.shape): (0,) * n) for c in consts],
        out_specs=[pl.BlockSpec((tq, d), row),
                   pl.BlockSpec((tq, d), row),
                   pl.BlockSpec((tq, LANE), row)],
    )
    return pl.pallas_call(
        functools.partial(_attn_kernel, tq=tq, seq=seq),
        grid_spec=grid_spec,
        out_shape=[jax.ShapeDtypeStruct((t, d), F32),
                   jax.ShapeDtypeStruct((t, d), F32),
                   jax.ShapeDtypeStruct((t, LANE), F32)],
        compiler_params=pltpu.CompilerParams(
            dimension_semantics=("arbitrary", "arbitrary"), vmem_limit_bytes=VMEM_LIMIT),
        name="attention",
    )(sink, x, qm, km, vm, qs, ks, vs, *consts)


def _route_kernel(lg_ref, lower_ref, out_ref, tab_ref, cnt_ref, carry_ref):
    i = pl.program_id(0)
    rows = lg_ref.shape[0]
    lg = lg_ref[...]
    lane = lax.broadcasted_iota(I32, (rows, LANE), 1)
    ninf = -jnp.inf

    @pl.when(i == 0)
    def _():
        carry_ref[...] = jnp.zeros_like(carry_ref)

    gl = jnp.where(lane < N_GROUPS, lg, ninf)
    gmax = jnp.max(gl, axis=1, keepdims=True)
    gidx = jnp.min(jnp.where(gl == gmax, lane, LANE), axis=1, keepdims=True)
    gp = 1.0 / jnp.sum(jnp.exp(gl - gmax), axis=1, keepdims=True)

    first = N_GROUPS + gidx * N_EXP
    el = jnp.where((lane >= first) & (lane < first + N_EXP), lg, ninf)
    m1 = jnp.max(el, axis=1, keepdims=True)
    i1 = jnp.min(jnp.where(el == m1, lane, LANE), axis=1, keepdims=True)
    el2 = jnp.where(lane == i1, ninf, el)
    m2 = jnp.max(el2, axis=1, keepdims=True)
    i2 = jnp.min(jnp.where(el2 == m2, lane, LANE), axis=1, keepdims=True)
    e2 = jnp.exp(m2 - m1)
    w1 = gp / (1.0 + e2)
    w2 = gp * e2 / (1.0 + e2)
    id1 = i1 - N_GROUPS
    id2 = i2 - N_GROUPS
    onehot = jnp.where((lane == id1) | (lane == id2), 1.0, 0.0)

    before = _dot(lower_ref[...], onehot.astype(BF16)) + carry_ref[...]
    rank1 = jnp.sum(jnp.where(lane == id1, before, 0.0), axis=1, keepdims=True)
    rank2 = jnp.sum(jnp.where(lane == id2, before, 0.0), axis=1, keepdims=True)
    carry_ref[...] += jnp.sum(onehot, axis=0, keepdims=True)
    out = jnp.where(lane == 0, rank1, 0.0)
    out = jnp.where(lane == 1, rank2, out)
    out = jnp.where(lane == 2, w1, out)
    out = jnp.where(lane == 3, w2, out)
    out = jnp.where(lane == 4, id1.astype(F32), out)
    out = jnp.where(lane == 5, id2.astype(F32), out)
    out_ref[...] = out
    tab_ref[...] = out.T[0:TAB_ROWS, :]
    cnt_ref[...] = carry_ref[...]


def _route(lg):
    t = lg.shape[0]
    rows = 512
    lower = jnp.tri(rows, k=-1, dtype=BF16)
    return pl.pallas_call(
        _route_kernel,
        grid=(t // rows,),
        in_specs=[pl.BlockSpec((rows, LANE), lambda i: (i, 0)),
                  pl.BlockSpec((rows, rows), lambda i: (0, 0))],
        out_specs=[pl.BlockSpec((rows, LANE), lambda i: (i, 0)),
                   pl.BlockSpec((TAB_ROWS, rows), lambda i: (0, i)),
                   pl.BlockSpec((1, LANE), lambda i: (0, 0))],
        out_shape=[jax.ShapeDtypeStruct((t, LANE), F32), jax.ShapeDtypeStruct((TAB_ROWS, t), F32),
                   jax.ShapeDtypeStruct((1, LANE), F32)],
        scratch_shapes=[pltpu.VMEM((1, LANE), F32)],
        compiler_params=pltpu.CompilerParams(dimension_semantics=("arbitrary",)),
        name="moe_route",
    )(lg, lower)


def _row(ref, r):
    return ref.at[pl.ds(r, 1), :]


def _sorted_row(rank_ref, id_ref, start_ref, j):
    return start_ref[id_ref[j]] + rank_ref[j]


def _dispatch_kernel(rank_ref, id_ref, start_ref, h_ref, xs_ref, sem):
    i = pl.program_id(0)
    rows = h_ref.shape[0]
    n_tok = rank_ref.shape[0] // 2

    for r in range(rows):
        t = i * rows + r
        for k in range(2):
            dst = _row(xs_ref, _sorted_row(rank_ref, id_ref, start_ref, k * n_tok + t))
            pltpu.make_async_copy(_row(h_ref, r), dst, sem).start(priority=k)
    for _ in range(2):
        pltpu.make_async_copy(h_ref, xs_ref.at[pl.ds(0, rows), :], sem).wait()


def _dispatch(routing, h):
    t, d = h.shape
    grid_spec = pltpu.PrefetchScalarGridSpec(
        num_scalar_prefetch=3,
        grid=(t // DISPATCH_TILE,),
        in_specs=[pl.BlockSpec((DISPATCH_TILE, d), lambda i, *_: (i, 0))],
        out_specs=pl.BlockSpec(memory_space=pl.ANY),
        scratch_shapes=[pltpu.SemaphoreType.DMA],
    )
    return pl.pallas_call(
        _dispatch_kernel,
        grid_spec=grid_spec,
        out_shape=jax.ShapeDtypeStruct((2 * t, d), F32),
        compiler_params=pltpu.CompilerParams(dimension_semantics=("arbitrary",)),
        name="moe_dispatch",
    )(*routing, h)


def _expert_kernel(tile_ref, exp_ref, lo_ref, hi_ref, xs_ref, wg_ref, wu_ref, wd_ref, ys_ref,
                   xbuf, sems, wg_s, wu_s, wd_s, *, n_tiles):
    i = pl.program_id(0)
    prev = jnp.maximum(i - 1, 0)
    tile = tile_ref[i]
    new_tile = (i == 0) | (tile != tile_ref[prev])
    new_exp = (i == 0) | (exp_ref[i] != exp_ref[prev])
    lo = lo_ref[i]
    hi = hi_ref[i]
    rows = ys_ref.shape[0]
    slot = tile % XBUF_SLOTS

    def fetch(tl):
        start = tl * rows if isinstance(tl, int) else pl.multiple_of(tl * rows, rows)
        return pltpu.make_async_copy(xs_ref.at[pl.ds(start, rows), :], xbuf.at[tl % XBUF_SLOTS],
                                     sems.at[tl % XBUF_SLOTS])

    @pl.when(i == 0)
    def _():
        for tl in range(min(XBUF_SLOTS - 1, n_tiles)):
            fetch(tl).start()

    @pl.when(new_tile)
    def _():
        fetch(tile).wait()

        @pl.when(tile + XBUF_SLOTS - 1 < n_tiles)
        def _():
            fetch(tile + XBUF_SLOTS - 1).start()

    @pl.when(new_exp)
    def _():
        wg_s[...] = wg_ref[0].astype(BF16)
        wu_s[...] = wu_ref[0].astype(BF16)
        wd_s[...] = wd_ref[0].astype(BF16)

    @pl.when(new_tile)
    def _():
        ys_ref[...] = jnp.zeros_like(ys_ref)

    for part in range(rows // MOE_SUBTILE):
        first = tile * rows + part * MOE_SUBTILE
        rs = slice(part * MOE_SUBTILE, (part + 1) * MOE_SUBTILE)

        @pl.when((hi > first) & (lo < first + MOE_SUBTILE))
        def _():
            x = xbuf[slot, rs, :].astype(BF16)
            g = _dot(x, wg_s[...])
            u = _dot(x, wu_s[...])
            hid = (g * _sigmoid(g) * u).astype(BF16)
            y = _dot(hid, wd_s[...])
            r = first + lax.broadcasted_iota(I32, (MOE_SUBTILE, 1), 0)
            ys_ref[rs, :] += jnp.where((r >= lo) & (r < hi), y, 0.0)


def _experts(work, first_expert, xs, wg, wu, wd):
    n, d = xs.shape
    f = wg.shape[2]
    nwork = work[0].shape[0]
    wmap = lambda i, tl, ex, lo, hi: (first_expert + ex[i], 0, 0)
    grid_spec = pltpu.PrefetchScalarGridSpec(
        num_scalar_prefetch=4,
        grid=(nwork,),
        in_specs=[pl.BlockSpec(memory_space=pl.ANY),
                  pl.BlockSpec((1, d, f), wmap),
                  pl.BlockSpec((1, d, f), wmap),
                  pl.BlockSpec((1, f, d), wmap)],
        out_specs=pl.BlockSpec((MOE_TILE, d), lambda i, tl, ex, lo, hi: (tl[i], 0)),
        scratch_shapes=[pltpu.VMEM((XBUF_SLOTS, MOE_TILE, d), F32), pltpu.SemaphoreType.DMA((XBUF_SLOTS,)),
                        pltpu.VMEM((d, f), BF16), pltpu.VMEM((d, f), BF16), pltpu.VMEM((f, d), BF16)],
    )
    return pl.pallas_call(
        functools.partial(_expert_kernel, n_tiles=n // MOE_TILE),
        grid_spec=grid_spec,
        out_shape=jax.ShapeDtypeStruct((n, d), F32),
        compiler_params=pltpu.CompilerParams(
            dimension_semantics=("arbitrary",), vmem_limit_bytes=VMEM_LIMIT),
        name="moe_experts",
    )(*work, xs, wg, wu, wd)


def _combine_kernel(rank_ref, id_ref, start_ref, x_ref, rw_ref, gfin_ref, ys_ref, xo_ref, buf, sems, *, final):
    i = pl.program_id(0)
    n = pl.num_programs(0)
    rows = x_ref.shape[0]
    n_tok = rank_ref.shape[0] // 2
    slot = i % 2

    def issue(step, slt):
        for r in range(rows):
            t = step * rows + r
            for k in range(2):
                src = _row(ys_ref, _sorted_row(rank_ref, id_ref, start_ref, k * n_tok + t))
                pltpu.make_async_copy(src, _row(buf.at[slt], k * rows + r), sems.at[slt]).start(priority=k)

    @pl.when(i == 0)
    def _():
        issue(0, 0)

    for slt in range(2):
        @pl.when((i + 1 < n) & (slot == 1 - slt))
        def _():
            issue(i + 1, slt)

    pltpu.make_async_copy(ys_ref.at[pl.ds(0, 2 * rows), :], buf.at[slot], sems.at[slot]).wait()

    rw = rw_ref[...]
    xn = x_ref[...] + rw[:, 2:3] * buf[slot, 0:rows, :] + rw[:, 3:4] * buf[slot, rows:2 * rows, :]
    if final:
        xn = _rms(xn, gfin_ref[...])
    xo_ref[...] = xn


def _combine(routing, x, rw, gfin, ys, final):
    t, d = x.shape
    grid_spec = pltpu.PrefetchScalarGridSpec(
        num_scalar_prefetch=3,
        grid=(t // ROW_TILE,),
        in_specs=[pl.BlockSpec((ROW_TILE, d), lambda i, *_: (i, 0)),
                  pl.BlockSpec((ROW_TILE, LANE), lambda i, *_: (i, 0)),
                  pl.BlockSpec((1, d), lambda i, *_: (0, 0)),
                  pl.BlockSpec(memory_space=pl.ANY)],
        out_specs=pl.BlockSpec((ROW_TILE, d), lambda i, *_: (i, 0)),
        scratch_shapes=[pltpu.VMEM((2, 2 * ROW_TILE, d), F32), pltpu.SemaphoreType.DMA((2,))],
    )
    return pl.pallas_call(
        functools.partial(_combine_kernel, final=final),
        grid_spec=grid_spec,
        out_shape=jax.ShapeDtypeStruct((t, d), F32),
        compiler_params=pltpu.CompilerParams(
            dimension_semantics=("arbitrary",), vmem_limit_bytes=VMEM_LIMIT),
        name="moe_combine",
    )(*routing, x, rw, gfin, ys)


def _worklist(cnt, n_rows):
    ends = jnp.cumsum(cnt)
    starts = ends - cnt
    n_tiles = n_rows // MOE_TILE
    tile_lo = jnp.arange(n_tiles, dtype=I32) * MOE_TILE
    exp_lo = starts[1:]
    rank_t = jnp.arange(n_tiles, dtype=I32) + jnp.sum(exp_lo[None, :] < tile_lo[:, None], axis=1)
    rank_e = jnp.arange(N_EXPERTS - 1, dtype=I32) + jnp.sum(tile_lo[None, :] <= exp_lo[:, None], axis=1)
    vals = jnp.concatenate([tile_lo, exp_lo])
    ranks = jnp.concatenate([rank_t, rank_e])
    slot = jnp.arange(vals.shape[0], dtype=I32)
    lo = jnp.sum(jnp.where(ranks[None, :] == slot[:, None], vals[None, :], 0), axis=1)
    hi = jnp.concatenate([lo[1:], jnp.full((1,), n_rows, I32)])
    tile = jnp.minimum(lo // MOE_TILE, n_tiles - 1)
    expert = jnp.minimum(jnp.sum(ends[None, :] <= lo[:, None], axis=1), N_EXPERTS - 1).astype(I32)
    return tile, expert, lo, hi


def _moe(x, h, lg, wg, wu, wd, layer, gfin, final):
    t = x.shape[0]
    rw, tab, cnt = _route(lg)
    cnt = cnt[0, :N_EXPERTS].astype(I32)
    starts = jnp.cumsum(cnt) - cnt
    routing = (tab[0:2].astype(I32).reshape(2 * t), tab[4:6].astype(I32).reshape(2 * t), starts)
    xs = _dispatch(routing, h)
    work = _worklist(cnt, 2 * t)
    ys = _experts(work, layer * N_EXPERTS, xs, wg, wu, wd)
    return _combine(routing, x, rw, gfin, ys, final)


def _router_weights(w_group, b_group, w_router, b_router):
    d = w_group.shape[0]
    w = jnp.concatenate([w_group] + [w_router[g] for g in range(N_GROUPS)], axis=1)
    w = jnp.pad(w, ((0, 0), (0, LANE - w.shape[1])))
    b = jnp.concatenate([b_group, b_router.reshape(-1)])
    b = jnp.pad(b, (0, LANE - b.shape[0])).reshape(1, LANE)
    hi = w.astype(BF16)
    lo = (w - hi.astype(F32)).astype(BF16)
    return hi, lo, b


def _rope_tables(seq):
    half = MLA_ROPE // 2
    pos = jnp.arange(seq, dtype=F32)
    inv = 1.0 / (ROPE_THETA ** (jnp.arange(0, MLA_ROPE, 2, dtype=F32) / MLA_ROPE))
    ang = pos[:, None] * inv[None, :]
    cos, sin = jnp.cos(ang), jnp.sin(ang)
    z = lambda n: jnp.zeros((seq, n), F32)
    tail = LANE - MLA_NOPE - MLA_ROPE
    ctab = jnp.concatenate([jnp.ones((seq, MLA_NOPE), F32), cos, cos, z(tail)], axis=1)
    atab = jnp.concatenate([z(MLA_NOPE), -sin, z(half), z(tail)], axis=1)
    btab = jnp.concatenate([z(MLA_NOPE), z(half), sin, z(tail)], axis=1)
    return ctab, atab, btab


def _odd_weights(w_in, w_uq, w_ukv):
    d = w_in.shape[0]
    sizes = (MLA_Q_RANK, MLA_KV_RANK, MLA_ROPE, SWA_HEADS * SWA_DIM, SWA_KV_HEADS * SWA_DIM,
             SWA_KV_HEADS * SWA_DIM)
    offs = [0]
    for s in sizes:
        offs.append(offs[-1] + s)
    cq, ckv, kpe, qs, ks, vs = [w_in[:, offs[k]:offs[k + 1]] for k in range(6)]
    kpe = jnp.pad(kpe, ((0, 0), (MLA_NOPE, LANE - MLA_NOPE - MLA_ROPE)))
    win = jnp.concatenate([cq, ckv, kpe, qs, ks, vs], axis=1).astype(BF16)
    hq = MLA_NOPE + MLA_ROPE
    wuq = jnp.pad(w_uq.reshape(MLA_Q_RANK, MLA_HEADS, hq), ((0, 0), (0, 0), (0, LANE - hq)))
    wuq = wuq.reshape(MLA_Q_RANK, MLA_HEADS * LANE).astype(BF16)
    wkv = w_ukv.reshape(MLA_KV_RANK, MLA_HEADS, MLA_NOPE + MLA_V)
    wuk = jnp.pad(wkv[:, :, :MLA_NOPE], ((0, 0), (0, 0), (0, LANE - MLA_NOPE)))
    wuk = wuk.reshape(MLA_KV_RANK, MLA_HEADS * LANE).astype(BF16)
    wuv = jnp.pad(wkv[:, :, MLA_NOPE:], ((0, 0), (0, 0), (0, LANE - MLA_V)))
    wuv = wuv.reshape(MLA_KV_RANK, MLA_HEADS * LANE).astype(BF16)
    vone = jnp.tile(jnp.zeros((LANE,), F32).at[MLA_V].set(1.0), MLA_HEADS).reshape(1, MLA_HEADS * LANE)
    return win, wuq, wuk, wuv, vone


def kernel(x, norm_mix, norm_ffn, norm_final, ev_w_in, ev_conv_w, ev_conv_b, ev_cnorm_g, ev_cnorm_b,
           ev_vnorm_g, ev_vnorm_b, ev_w_sp, ev_b_sp, ev_w_out, od_w_in, od_g_cq, od_w_uq, od_g_ckv,
           od_w_ukv, od_sink, od_w_out, moe_w_group, moe_b_group, moe_w_router, moe_b_router,
           moe_w_gate, moe_w_up, moe_w_down):
    b, seq, d = x.shape
    t = b * seq
    depth = norm_mix.shape[0]
    row = lambda v: v.reshape(1, -1)
    xt = x.reshape(t, d)
    gfin = row(norm_final)
    wg_all = moe_w_gate.reshape(depth * N_EXPERTS, d, -1)
    wu_all = moe_w_up.reshape(depth * N_EXPERTS, d, -1)
    wd_all = moe_w_down.reshape(depth * N_EXPERTS, -1, d)
    for layer in range(depth):
        k = layer // 2
        wrh, wrl, br = _router_weights(moe_w_group[layer], moe_b_group[layer], moe_w_router[layer],
                                       moe_b_router[layer])
        gffn = row(norm_ffn[layer])
        if layer % 2 == 0:
            bsp = jnp.broadcast_to(ev_b_sp[k][:, :, None], ev_w_sp[k].shape)
            xt, h3, lg = _even_mixer(
                xt, seq, row(norm_mix[layer]), ev_w_in[k].astype(BF16), ev_conv_w[k], row(ev_conv_b[k]),
                row(ev_cnorm_g[k]), row(ev_cnorm_b[k]), row(ev_vnorm_g[k]), row(ev_vnorm_b[k]),
                ev_w_sp[k].astype(BF16), bsp, ev_w_out[k].astype(BF16), gffn, wrh, wrl, br)
        else:
            win, wuq, wuk, wuv, vone = _odd_weights(od_w_in[k], od_w_uq[k], od_w_ukv[k])
            ctab, atab, btab = _rope_tables(seq)
            qm, km, vm, qs, ks, vs = _odd_proj(
                xt, seq, row(norm_mix[layer]), win, row(od_g_cq[k]), wuq, row(od_g_ckv[k]), wuk, wuv,
                vone, ctab, atab, btab)
            xt, h3, lg = _attention(xt, seq, od_sink[k], qm, km, vm, qs, ks, vs,
                                    od_w_out[k].astype(BF16), gffn, wrh, wrl, br)
        xt = _moe(xt, h3, lg, wg_all, wu_all, wd_all, layer, gfin, layer == depth - 1)
    return xt.reshape(b, seq, d)
```

```python
import functools

import jax
import jax.numpy as jnp
from jax import lax
from jax.experimental import pallas as pl
from jax.experimental.pallas import tpu as pltpu

F32 = jnp.float32
BF16 = jnp.bfloat16
I32 = jnp.int32

EPS = 1e-6
NEG_INF = -1e30
LOG2E = 1.4426950408889634
LANE = 128
VMEM_LIMIT = 56 * 1024 * 1024

CONV_DIM = 512
CONV_GROUP = 128
CONV_WIDTH = 31
SGU_DIM = 512
SGU_HEAD = 128
CHUNK = 128
MLA_HEADS = 8
MLA_Q_RANK = 256
MLA_KV_RANK = 128
MLA_NOPE = 64
MLA_ROPE = 32
MLA_V = 64
ROPE_THETA = 10000.0
SWA_HEADS = 8
SWA_KV_HEADS = 2
SWA_DIM = 64
WINDOW = 128
N_GROUPS = 4
N_EXP = 8
N_EXPERTS = N_GROUPS * N_EXP

HALO = 16
CONV_ROWS = 64
MOE_TILE = 512
XBUF_SLOTS = 3
ROW_TILE = 256
TAB_ROWS = 8
DISPATCH_TILE = 512


def _dot(a, b):
    return jnp.dot(a, b, preferred_element_type=F32)


def _dot_nt(a, b):
    return lax.dot_general(a, b, (((1,), (1,)), ((), ())), preferred_element_type=F32)


def _rms(x, g):
    return x * lax.rsqrt(jnp.mean(x * x, axis=-1, keepdims=True) + EPS) * g


def _ln(x, g, b):
    mu = jnp.mean(x, axis=-1, keepdims=True)
    d = x - mu
    var = jnp.mean(d * d, axis=-1, keepdims=True)
    return d * lax.rsqrt(var + EPS) * g + b


def _gelu(x):
    return 0.5 * x * (1.0 + jnp.tanh(0.7978845608028654 * (x + 0.044715 * (x * x * x))))


def _sigmoid(x):
    return 1.0 / (1.0 + jnp.exp(-x))


def _router_epilogue(xn, gffn_ref, wrh_ref, wrl_ref, br_ref, h3_ref, lg_ref):
    h = _rms(xn, gffn_ref[...])
    h3_ref[...] = h
    hi = h.astype(BF16)
    lo = (h - hi.astype(F32)).astype(BF16)
    wh = wrh_ref[...]
    lg_ref[...] = _dot(hi, wh) + _dot(lo, wh) + _dot(hi, wrl_ref[...]) + br_ref[...]


def _even_kernel(xc_ref, xp_ref, gmix_ref, win_ref, cw_ref, cb_ref, cng_ref, cnb_ref, vng_ref,
                 vnb_ref, wsp_ref, bsp_ref, wout_ref, gffn_ref, wrh_ref, wrl_ref, br_ref,
                 xo_ref, h3_ref, lg_ref, a_s, mix_s, *, ns, sub, seq):
    j = pl.program_id(1)
    win_rows = CONV_ROWS + 2 * HALO
    n_chunks = sub // CONV_ROWS

    def proj_conv_input():
        h = _rms(xc_ref[...], gmix_ref[...]).astype(BF16)
        pa = _dot(h, win_ref[:, 0:2 * CONV_DIM])
        return h, pa[:, :CONV_DIM] * _sigmoid(pa[:, CONV_DIM:])

    def proj_gating(h, a):
        r0 = pl.multiple_of(j * sub, sub)
        a_s[pl.ds(HALO + r0, sub), :] = a
        u = _gelu(_dot(h, win_ref[:, 2 * CONV_DIM:2 * CONV_DIM + SGU_DIM]))
        v = _gelu(_dot(h, win_ref[:, 2 * CONV_DIM + SGU_DIM:]))
        for hd in range(SGU_DIM // SGU_HEAD):
            cs = slice(hd * SGU_HEAD, (hd + 1) * SGU_HEAD)
            vn = _ln(v[:, cs], vng_ref[:, cs], vnb_ref[:, cs]).astype(BF16)
            for c in range(sub // CHUNK):
                rs = slice(c * CHUNK, (c + 1) * CHUNK)
                sv = _dot(wsp_ref[hd], vn[rs, :]) + bsp_ref[hd]
                mix_s[pl.ds(r0 + c * CHUNK, CHUNK),
                      CONV_DIM + hd * SGU_HEAD:CONV_DIM + (hd + 1) * SGU_HEAD] = (u[rs, cs] * sv).astype(BF16)

    def conv_and_out(next_halo):
        r0 = (j - 1) * sub
        for i in range(n_chunks):
            base = pl.multiple_of(r0 + i * CONV_ROWS, CONV_ROWS)
            if i < n_chunks - 1:
                win = a_s[pl.ds(base, win_rows), :]
            else:
                win = jnp.concatenate([a_s[pl.ds(base, CONV_ROWS + HALO), :], next_halo], axis=0)
            acc = jnp.zeros((CONV_ROWS, CONV_DIM), F32)
            for s in range(8):
                sh = win if s == 0 else pltpu.roll(win, win_rows - s, 0)
                for m in range(4):
                    k = 8 * m + s - 1
                    if 0 <= k < CONV_WIDTH:
                        acc = acc + sh[8 * m:8 * m + CONV_ROWS, :] * cw_ref[k:k + 1, :]
            acc = acc + cb_ref[...]
            for g in range(CONV_DIM // CONV_GROUP):
                cs = slice(g * CONV_GROUP, (g + 1) * CONV_GROUP)
                y = _ln(acc[:, cs], cng_ref[:, cs], cnb_ref[:, cs])
                mix_s[pl.ds(base, CONV_ROWS), cs] = (y * _sigmoid(y)).astype(BF16)
        mix = mix_s[pl.ds(pl.multiple_of(r0, sub), sub), :]
        xn = xp_ref[...] + _dot(mix, wout_ref[...])
        xo_ref[...] = xn
        _router_epilogue(xn, gffn_ref, wrh_ref, wrl_ref, br_ref, h3_ref, lg_ref)

    @pl.when(j == 0)
    def _():
        a_s[0:HALO, :] = jnp.zeros((HALO, CONV_DIM), F32)
        h, a = proj_conv_input()
        proj_gating(h, a)

    @pl.when((j >= 1) & (j < ns))
    def _():
        h, a = proj_conv_input()
        conv_and_out(a[0:HALO, :])
        proj_gating(h, a)

    @pl.when(j == ns)
    def _():
        conv_and_out(jnp.zeros((HALO, CONV_DIM), F32))


def _const_spec(shape):
    nd = len(shape)
    return pl.BlockSpec(shape, lambda *_: (0,) * nd)


def _even_mixer(x, seq, gmix, win, cw, cb, cng, cnb, vng, vnb, wsp, bsp, wout, gffn, wrh, wrl, br):
    t, d = x.shape
    nb = t // seq
    sub = 512
    ns = seq // sub
    row = lambda b, j: (b * ns + jnp.minimum(j, ns - 1), 0)
    prev = lambda b, j: (b * ns + jnp.maximum(j - 1, 0), 0)
    consts = (gmix, win, cw, cb, cng, cnb, vng, vnb, wsp, bsp, wout, gffn, wrh, wrl, br)
    return pl.pallas_call(
        functools.partial(_even_kernel, ns=ns, sub=sub, seq=seq),
        grid=(nb, ns + 1),
        in_specs=[pl.BlockSpec((sub, d), row), pl.BlockSpec((sub, d), prev)]
        + [_const_spec(c.shape) for c in consts],
        out_specs=[pl.BlockSpec((sub, d), prev),
                   pl.BlockSpec((sub, d), prev),
                   pl.BlockSpec((sub, LANE), prev)],
        out_shape=[jax.ShapeDtypeStruct((t, d), F32),
                   jax.ShapeDtypeStruct((t, d), F32),
                   jax.ShapeDtypeStruct((t, LANE), F32)],
        scratch_shapes=[pltpu.VMEM((seq + HALO, CONV_DIM), F32),
                        pltpu.VMEM((seq, CONV_DIM + SGU_DIM), BF16)],
        compiler_params=pltpu.CompilerParams(
            dimension_semantics=("arbitrary", "arbitrary"), vmem_limit_bytes=VMEM_LIMIT),
        name="even_mixer",
    )(x, x, *consts)


def _odd_proj_kernel(x_ref, gmix_ref, win_ref, gcq_ref, wuq_ref, gckv_ref, wuk_ref, wuv_ref, vone_ref,
                     cos_ref, sa_ref, sb_ref, qm_ref, km_ref, vm_ref, qs_ref, ks_ref, vs_ref):
    h = _rms(x_ref[...], gmix_ref[...]).astype(BF16)
    p = _dot(h, win_ref[...])
    o = 0
    cq = p[:, o:o + MLA_Q_RANK]; o += MLA_Q_RANK
    ckv = p[:, o:o + MLA_KV_RANK]; o += MLA_KV_RANK
    kpe = p[:, o:o + LANE]; o += LANE
    qs = p[:, o:o + SWA_HEADS * SWA_DIM]; o += SWA_HEADS * SWA_DIM
    ks = p[:, o:o + LANE]; o += LANE
    vs = p[:, o:o + LANE]

    cos = cos_ref[...]
    sa = sa_ref[...]
    sb = sb_ref[...]

    def rope(z, reps):
        w = z.shape[1]
        c = jnp.concatenate([cos] * reps, axis=1) if reps > 1 else cos
        a = jnp.concatenate([sa] * reps, axis=1) if reps > 1 else sa
        b = jnp.concatenate([sb] * reps, axis=1) if reps > 1 else sb
        return z * c + pltpu.roll(z, w - MLA_ROPE // 2, 1) * a + pltpu.roll(z, MLA_ROPE // 2, 1) * b

    q = _dot(_rms(cq, gcq_ref[...]).astype(BF16), wuq_ref[...])
    qm_ref[...] = (rope(q, MLA_HEADS) * ((MLA_NOPE + MLA_ROPE) ** -0.5 * LOG2E)).astype(BF16)
    ckvn = _rms(ckv, gckv_ref[...]).astype(BF16)
    kr = rope(kpe, 1)
    km_ref[...] = (_dot(ckvn, wuk_ref[...]) + jnp.concatenate([kr] * MLA_HEADS, axis=1)).astype(BF16)
    vm_ref[...] = (_dot(ckvn, wuv_ref[...]) + vone_ref[...]).astype(BF16)
    qs_ref[...] = (qs * (SWA_DIM ** -0.5 * LOG2E)).astype(BF16)
    ks_ref[...] = ks.astype(BF16)
    lane = lax.broadcasted_iota(I32, (vs.shape[0], SWA_DIM), 1)
    one = jnp.where(lane == 0, 1.0, 0.0)
    vs_ref[...] = jnp.concatenate(
        [piece for g in range(SWA_KV_HEADS) for piece in (vs[:, g * SWA_DIM:(g + 1) * SWA_DIM], one)],
        axis=1).astype(BF16)


def _odd_proj(x, seq, gmix, win, gcq, wuq, gckv, wuk, wuv, vone, cos, sa, sb):
    t, d = x.shape
    tm = 512
    nsq = seq // tm
    row = lambda i: (i, 0)
    pos = lambda i: (i % nsq, 0)
    consts = (gmix, win, gcq, wuq, gckv, wuk, wuv, vone)
    widths = (MLA_HEADS * LANE, MLA_HEADS * LANE, MLA_HEADS * LANE, SWA_HEADS * SWA_DIM, LANE,
              SWA_KV_HEADS * LANE)
    return pl.pallas_call(
        _odd_proj_kernel,
        grid=(t // tm,),
        in_specs=[pl.BlockSpec((tm, d), row)] + [_const_spec(c.shape) for c in consts]
        + [pl.BlockSpec((tm, LANE), pos)] * 3,
        out_specs=[pl.BlockSpec((tm, w), row) for w in widths],
        out_shape=[jax.ShapeDtypeStruct((t, w), BF16) for w in widths],
        compiler_params=pltpu.CompilerParams(
            dimension_semantics=("arbitrary",), vmem_limit_bytes=VMEM_LIMIT),
        name="odd_proj",
    )(x, *consts, cos, sa, sb)


def _attn_kernel(sink_ref, x_ref, qm_ref, km_ref, vm_ref, qs_ref, ks_ref, vs_ref, wout_ref,
                 gffn_ref, wrh_ref, wrl_ref, br_ref, xo_ref, h3_ref, lg_ref, *, tq, seq):
    i = pl.program_id(1)
    qm = qm_ref[...]
    outs = []
    for hd in range(MLA_HEADS):
        cs = slice(hd * LANE, (hd + 1) * LANE)
        s = _dot_nt(qm[:, cs], km_ref[:, cs])
        e = jnp.exp2((s - jnp.max(s, axis=-1, keepdims=True)).astype(BF16))
        pv = _dot(e, vm_ref[:, cs])
        outs.append(pv[:, :MLA_V] / pv[:, MLA_V:MLA_V + 1])

    span = LANE + 2 * WINDOW
    rep = SWA_HEADS // SWA_KV_HEADS
    swa_rows = []
    for blk in range(tq // LANE):
        q0 = i * tq + blk * LANE
        start = pl.multiple_of(jnp.clip(q0 - WINDOW, 0, seq - span), LANE)
        kw = ks_ref[pl.ds(start, span), :]
        vw = vs_ref[pl.ds(start, span), :]
        qpos = q0 + lax.broadcasted_iota(I32, (LANE, span), 0)
        kpos = start + lax.broadcasted_iota(I32, (LANE, span), 1)
        absd = jnp.abs(qpos - kpos).astype(F32)
        in_win = absd <= float(WINDOW)
        heads = []
        for g in range(SWA_KV_HEADS):
            hds = range(g * rep, (g + 1) * rep)
            rows = slice(blk * LANE, (blk + 1) * LANE)
            q = jnp.concatenate([qs_ref[rows, hd * SWA_DIM:(hd + 1) * SWA_DIM] for hd in hds], axis=0)
            bias = jnp.concatenate(
                [jnp.where(in_win, (-LOG2E * 2.0 ** -(hd + 1)) * absd, NEG_INF) for hd in hds], axis=0)
            sk = jnp.concatenate([jnp.full((LANE, 1), sink_ref[hd] * LOG2E, F32) for hd in hds], axis=0)
            s = _dot_nt(q, kw[:, g * SWA_DIM:(g + 1) * SWA_DIM]) + bias
            m = jnp.maximum(jnp.max(s, axis=-1, keepdims=True), sk)
            pv = _dot(jnp.exp2((s - m).astype(BF16)), vw[:, g * LANE:(g + 1) * LANE])
            o = pv[:, :SWA_DIM] / (pv[:, SWA_DIM:SWA_DIM + 1] + jnp.exp2(sk - m))
            heads += [o[r * LANE:(r + 1) * LANE] for r in range(rep)]
        swa_rows.append(jnp.concatenate(heads, axis=1))
    swa = jnp.concatenate(swa_rows, axis=0) if len(swa_rows) > 1 else swa_rows[0]

    mix = jnp.concatenate(outs + [swa], axis=1).astype(BF16)
    xn = x_ref[...] + _dot(mix, wout_ref[...])
    xo_ref[...] = xn
    _router_epilogue(xn, gffn_ref, wrh_ref, wrl_ref, br_ref, h3_ref, lg_ref)


def _attention(x, seq, sink, qm, km, vm, qs, ks, vs, wout, gffn, wrh, wrl, br):
    t, d = x.shape
    nb = t // seq
    tq = 512
    nq = seq // tq
    row = lambda b, i, *_: (b * nq + i, 0)
    bat = lambda b, i, *_: (b, 0)
    consts = (wout, gffn, wrh, wrl, br)
    grid_spec = pltpu.PrefetchScalarGridSpec(
        num_scalar_prefetch=1,
        grid=(nb, nq),
        in_specs=[pl.BlockSpec((tq, d), row),
                  pl.BlockSpec((tq, qm.shape[1]), row),
                  pl.BlockSpec((seq, km.shape[1]), bat),
                  pl.BlockSpec((seq, vm.shape[1]), bat),
                  pl.BlockSpec((tq, qs.shape[1]), row),
                  pl.BlockSpec((seq, ks.shape[1]), bat),
                  pl.BlockSpec((seq, vs.shape[1]), bat)]
        + [pl.BlockSpec(c.shape, lambda b, i, *_, n=len(c.shape): (0,) * n) for c in consts],
        out_specs=[pl.BlockSpec((tq, d), row),
                   pl.BlockSpec((tq, d), row),
                   pl.BlockSpec((tq, LANE), row)],
    )
    return pl.pallas_call(
        functools.partial(_attn_kernel, tq=tq, seq=seq),
        grid_spec=grid_spec,
        out_shape=[jax.ShapeDtypeStruct((t, d), F32),
                   jax.ShapeDtypeStruct((t, d), F32),
                   jax.ShapeDtypeStruct((t, LANE), F32)],
        compiler_params=pltpu.CompilerParams(
            dimension_semantics=("arbitrary", "arbitrary"), vmem_limit_bytes=VMEM_LIMIT),
        name="attention",
    )(sink, x, qm, km, vm, qs, ks, vs, *consts)


def _route_kernel(lg_ref, lower_ref, out_ref, tab_ref, cnt_ref, carry_ref):
    i = pl.program_id(0)
    rows = lg_ref.shape[0]
    lg = lg_ref[...]
    lane = lax.broadcasted_iota(I32, (rows, LANE), 1)
    ninf = -jnp.inf

    @pl.when(i == 0)
    def _():
        carry_ref[...] = jnp.zeros_like(carry_ref)

    gl = jnp.where(lane < N_GROUPS, lg, ninf)
    gmax = jnp.max(gl, axis=1, keepdims=True)
    gidx = jnp.min(jnp.where(gl == gmax, lane, LANE), axis=1, keepdims=True)
    gp = 1.0 / jnp.sum(jnp.exp(gl - gmax), axis=1, keepdims=True)

    first = N_GROUPS + gidx * N_EXP
    el = jnp.where((lane >= first) & (lane < first + N_EXP), lg, ninf)
    m1 = jnp.max(el, axis=1, keepdims=True)
    i1 = jnp.min(jnp.where(el == m1, lane, LANE), axis=1, keepdims=True)
    el2 = jnp.where(lane == i1, ninf, el)
    m2 = jnp.max(el2, axis=1, keepdims=True)
    i2 = jnp.min(jnp.where(el2 == m2, lane, LANE), axis=1, keepdims=True)
    e2 = jnp.exp(m2 - m1)
    w1 = gp / (1.0 + e2)
    w2 = gp * e2 / (1.0 + e2)
    id1 = i1 - N_GROUPS
    id2 = i2 - N_GROUPS
    onehot = jnp.where((lane == id1) | (lane == id2), 1.0, 0.0)

    before = _dot(lower_ref[...], onehot.astype(BF16)) + carry_ref[...]
    rank1 = jnp.sum(jnp.where(lane == id1, before, 0.0), axis=1, keepdims=True)
    rank2 = jnp.sum(jnp.where(lane == id2, before, 0.0), axis=1, keepdims=True)
    carry_ref[...] += jnp.sum(onehot, axis=0, keepdims=True)
    out = jnp.where(lane == 0, rank1, 0.0)
    out = jnp.where(lane == 1, rank2, out)
    out = jnp.where(lane == 2, w1, out)
    out = jnp.where(lane == 3, w2, out)
    out = jnp.where(lane == 4, id1.astype(F32), out)
    out = jnp.where(lane == 5, id2.astype(F32), out)
    out_ref[...] = out
    tab_ref[...] = out.T[0:TAB_ROWS, :]
    cnt_ref[...] = carry_ref[...]


def _route(lg):
    t = lg.shape[0]
    rows = 512
    lower = jnp.tri(rows, k=-1, dtype=BF16)
    return pl.pallas_call(
        _route_kernel,
        grid=(t // rows,),
        in_specs=[pl.BlockSpec((rows, LANE), lambda i: (i, 0)),
                  pl.BlockSpec((rows, rows), lambda i: (0, 0))],
        out_specs=[pl.BlockSpec((rows, LANE), lambda i: (i, 0)),
                   pl.BlockSpec((TAB_ROWS, rows), lambda i: (0, i)),
                   pl.BlockSpec((1, LANE), lambda i: (0, 0))],
        out_shape=[jax.ShapeDtypeStruct((t, LANE), F32), jax.ShapeDtypeStruct((TAB_ROWS, t), F32),
                   jax.ShapeDtypeStruct((1, LANE), F32)],
        scratch_shapes=[pltpu.VMEM((1, LANE), F32)],
        compiler_params=pltpu.CompilerParams(dimension_semantics=("arbitrary",)),
        name="moe_route",
    )(lg, lower)


def _row(ref, r):
    return ref.at[pl.ds(r, 1), :]


def _dispatch_kernel(pos_ref, h_ref, xs_ref, sem):
    i = pl.program_id(0)
    rows = h_ref.shape[0]
    n_tok = pos_ref.shape[0] // 2

    for r in range(rows):
        t = i * rows + r
        for k in range(2):
            pltpu.make_async_copy(_row(h_ref, r), _row(xs_ref, pos_ref[k * n_tok + t]), sem).start(priority=k)
    for _ in range(2):
        pltpu.make_async_copy(h_ref, xs_ref.at[pl.ds(0, rows), :], sem).wait()


def _dispatch(pos, h):
    t, d = h.shape
    grid_spec = pltpu.PrefetchScalarGridSpec(
        num_scalar_prefetch=1,
        grid=(t // DISPATCH_TILE,),
        in_specs=[pl.BlockSpec((DISPATCH_TILE, d), lambda i, *_: (i, 0))],
        out_specs=pl.BlockSpec(memory_space=pl.ANY),
        scratch_shapes=[pltpu.SemaphoreType.DMA],
    )
    return pl.pallas_call(
        _dispatch_kernel,
        grid_spec=grid_spec,
        out_shape=jax.ShapeDtypeStruct((2 * t, d), F32),
        compiler_params=pltpu.CompilerParams(dimension_semantics=("arbitrary",)),
        name="moe_dispatch",
    )(pos, h)


def _expert_kernel(tile_ref, exp_ref, lo_ref, hi_ref, xs_ref, wg_ref, wu_ref, wd_ref, ys_ref,
                   xbuf, sems, wg_s, wu_s, wd_s, *, n_tiles):
    i = pl.program_id(0)
    prev = jnp.maximum(i - 1, 0)
    tile = tile_ref[i]
    new_tile = (i == 0) | (tile != tile_ref[prev])
    new_exp = (i == 0) | (exp_ref[i] != exp_ref[prev])
    lo = lo_ref[i]
    hi = hi_ref[i]
    rows = ys_ref.shape[0]
    slot = tile % XBUF_SLOTS

    def fetch(tl):
        start = tl * rows if isinstance(tl, int) else pl.multiple_of(tl * rows, rows)
        return pltpu.make_async_copy(xs_ref.at[pl.ds(start, rows), :], xbuf.at[tl % XBUF_SLOTS],
                                     sems.at[tl % XBUF_SLOTS])

    @pl.when(i == 0)
    def _():
        for tl in range(min(XBUF_SLOTS - 1, n_tiles)):
            fetch(tl).start()

    @pl.when(new_tile)
    def _():
        fetch(tile).wait()

        @pl.when(tile + XBUF_SLOTS - 1 < n_tiles)
        def _():
            fetch(tile + XBUF_SLOTS - 1).start()

    @pl.when(new_exp)
    def _():
        wg_s[...] = wg_ref[0].astype(BF16)
        wu_s[...] = wu_ref[0].astype(BF16)
        wd_s[...] = wd_ref[0].astype(BF16)

    @pl.when(new_tile)
    def _():
        ys_ref[...] = jnp.zeros_like(ys_ref)

    @pl.when(hi > lo)
    def _():
        x = xbuf[slot].astype(BF16)
        g = _dot(x, wg_s[...])
        u = _dot(x, wu_s[...])
        hid = (g * _sigmoid(g) * u).astype(BF16)
        y = _dot(hid, wd_s[...])
        r = tile * rows + lax.broadcasted_iota(I32, (rows, 1), 0)
        ys_ref[...] += jnp.where((r >= lo) & (r < hi), y, 0.0)


def _experts(work, first_expert, xs, wg, wu, wd):
    n, d = xs.shape
    f = wg.shape[2]
    nwork = work[0].shape[0]
    wmap = lambda i, tl, ex, lo, hi: (first_expert + ex[i], 0, 0)
    grid_spec = pltpu.PrefetchScalarGridSpec(
        num_scalar_prefetch=4,
        grid=(nwork,),
        in_specs=[pl.BlockSpec(memory_space=pl.ANY),
                  pl.BlockSpec((1, d, f), wmap),
                  pl.BlockSpec((1, d, f), wmap),
                  pl.BlockSpec((1, f, d), wmap)],
        out_specs=pl.BlockSpec((MOE_TILE, d), lambda i, tl, ex, lo, hi: (tl[i], 0)),
        scratch_shapes=[pltpu.VMEM((XBUF_SLOTS, MOE_TILE, d), F32), pltpu.SemaphoreType.DMA((XBUF_SLOTS,)),
                        pltpu.VMEM((d, f), BF16), pltpu.VMEM((d, f), BF16), pltpu.VMEM((f, d), BF16)],
    )
    return pl.pallas_call(
        functools.partial(_expert_kernel, n_tiles=n // MOE_TILE),
        grid_spec=grid_spec,
        out_shape=jax.ShapeDtypeStruct((n, d), F32),
        compiler_params=pltpu.CompilerParams(
            dimension_semantics=("arbitrary",), vmem_limit_bytes=VMEM_LIMIT),
        name="moe_experts",
    )(*work, xs, wg, wu, wd)


def _combine_kernel(pos_ref, x_ref, rw_ref, gfin_ref, ys_ref, xo_ref, buf, sems, *, final):
    i = pl.program_id(0)
    n = pl.num_programs(0)
    rows = x_ref.shape[0]
    n_tok = pos_ref.shape[0] // 2
    slot = i % 2

    def issue(step, slt):
        for r in range(rows):
            t = step * rows + r
            for k in range(2):
                pltpu.make_async_copy(_row(ys_ref, pos_ref[k * n_tok + t]), _row(buf.at[slt], k * rows + r),
                                      sems.at[slt]).start(priority=k)

    @pl.when(i == 0)
    def _():
        issue(0, 0)

    for slt in range(2):
        @pl.when((i + 1 < n) & (slot == 1 - slt))
        def _():
            issue(i + 1, slt)

    pltpu.make_async_copy(ys_ref.at[pl.ds(0, 2 * rows), :], buf.at[slot], sems.at[slot]).wait()

    rw = rw_ref[...]
    xn = x_ref[...] + rw[:, 2:3] * buf[slot, 0:rows, :] + rw[:, 3:4] * buf[slot, rows:2 * rows, :]
    if final:
        xn = _rms(xn, gfin_ref[...])
    xo_ref[...] = xn


def _combine(pos, x, rw, gfin, ys, final):
    t, d = x.shape
    grid_spec = pltpu.PrefetchScalarGridSpec(
        num_scalar_prefetch=1,
        grid=(t // ROW_TILE,),
        in_specs=[pl.BlockSpec((ROW_TILE, d), lambda i, *_: (i, 0)),
                  pl.BlockSpec((ROW_TILE, LANE), lambda i, *_: (i, 0)),
                  pl.BlockSpec((1, d), lambda i, *_: (0, 0)),
                  pl.BlockSpec(memory_space=pl.ANY)],
        out_specs=pl.BlockSpec((ROW_TILE, d), lambda i, *_: (i, 0)),
        scratch_shapes=[pltpu.VMEM((2, 2 * ROW_TILE, d), F32), pltpu.SemaphoreType.DMA((2,))],
    )
    return pl.pallas_call(
        functools.partial(_combine_kernel, final=final),
        grid_spec=grid_spec,
        out_shape=jax.ShapeDtypeStruct((t, d), F32),
        compiler_params=pltpu.CompilerParams(
            dimension_semantics=("arbitrary",), vmem_limit_bytes=VMEM_LIMIT),
        name="moe_combine",
    )(pos, x, rw, gfin, ys)


def _worklist(cnt, n_rows):
    ends = jnp.cumsum(cnt)
    starts = ends - cnt
    n_tiles = n_rows // MOE_TILE
    tile_lo = jnp.arange(n_tiles, dtype=I32) * MOE_TILE
    exp_lo = starts[1:]
    rank_t = jnp.arange(n_tiles, dtype=I32) + jnp.sum(exp_lo[None, :] < tile_lo[:, None], axis=1)
    rank_e = jnp.arange(N_EXPERTS - 1, dtype=I32) + jnp.sum(tile_lo[None, :] <= exp_lo[:, None], axis=1)
    vals = jnp.concatenate([tile_lo, exp_lo])
    ranks = jnp.concatenate([rank_t, rank_e])
    slot = jnp.arange(vals.shape[0], dtype=I32)
    lo = jnp.sum(jnp.where(ranks[None, :] == slot[:, None], vals[None, :], 0), axis=1)
    hi = jnp.concatenate([lo[1:], jnp.full((1,), n_rows, I32)])
    tile = jnp.minimum(lo // MOE_TILE, n_tiles - 1)
    expert = jnp.minimum(jnp.sum(ends[None, :] <= lo[:, None], axis=1), N_EXPERTS - 1).astype(I32)
    return tile, expert, lo, hi


def _moe(x, h, lg, wg, wu, wd, layer, gfin, final):
    t = x.shape[0]
    rw, tab, cnt = _route(lg)
    cnt = cnt[0, :N_EXPERTS].astype(I32)
    starts = jnp.cumsum(cnt) - cnt
    ids = tab[4:6].astype(I32)
    offs = jnp.zeros_like(ids)
    for e in range(N_EXPERTS):
        offs = jnp.where(ids == e, starts[e], offs)
    pos = (offs + tab[0:2].astype(I32)).reshape(2 * t)
    xs = _dispatch(pos, h)
    work = _worklist(cnt, 2 * t)
    ys = _experts(work, layer * N_EXPERTS, xs, wg, wu, wd)
    return _combine(pos, x, rw, gfin, ys, final)


def _router_weights(w_group, b_group, w_router, b_router):
    d = w_group.shape[0]
    w = jnp.concatenate([w_group] + [w_router[g] for g in range(N_GROUPS)], axis=1)
    w = jnp.pad(w, ((0, 0), (0, LANE - w.shape[1])))
    b = jnp.concatenate([b_group, b_router.reshape(-1)])
    b = jnp.pad(b, (0, LANE - b.shape[0])).reshape(1, LANE)
    hi = w.astype(BF16)
    lo = (w - hi.astype(F32)).astype(BF16)
    return hi, lo, b


def _rope_tables(seq):
    half = MLA_ROPE // 2
    pos = jnp.arange(seq, dtype=F32)
    inv = 1.0 / (ROPE_THETA ** (jnp.arange(0, MLA_ROPE, 2, dtype=F32) / MLA_ROPE))
    ang = pos[:, None] * inv[None, :]
    cos, sin = jnp.cos(ang), jnp.sin(ang)
    z = lambda n: jnp.zeros((seq, n), F32)
    tail = LANE - MLA_NOPE - MLA_ROPE
    ctab = jnp.concatenate([jnp.ones((seq, MLA_NOPE), F32), cos, cos, z(tail)], axis=1)
    atab = jnp.concatenate([z(MLA_NOPE), -sin, z(half), z(tail)], axis=1)
    btab = jnp.concatenate([z(MLA_NOPE), z(half), sin, z(tail)], axis=1)
    return ctab, atab, btab


def _odd_weights(w_in, w_uq, w_ukv):
    d = w_in.shape[0]
    sizes = (MLA_Q_RANK, MLA_KV_RANK, MLA_ROPE, SWA_HEADS * SWA_DIM, SWA_KV_HEADS * SWA_DIM,
             SWA_KV_HEADS * SWA_DIM)
    offs = [0]
    for s in sizes:
        offs.append(offs[-1] + s)
    cq, ckv, kpe, qs, ks, vs = [w_in[:, offs[k]:offs[k + 1]] for k in range(6)]
    kpe = jnp.pad(kpe, ((0, 0), (MLA_NOPE, LANE - MLA_NOPE - MLA_ROPE)))
    win = jnp.concatenate([cq, ckv, kpe, qs, ks, vs], axis=1).astype(BF16)
    hq = MLA_NOPE + MLA_ROPE
    wuq = jnp.pad(w_uq.reshape(MLA_Q_RANK, MLA_HEADS, hq), ((0, 0), (0, 0), (0, LANE - hq)))
    wuq = wuq.reshape(MLA_Q_RANK, MLA_HEADS * LANE).astype(BF16)
    wkv = w_ukv.reshape(MLA_KV_RANK, MLA_HEADS, MLA_NOPE + MLA_V)
    wuk = jnp.pad(wkv[:, :, :MLA_NOPE], ((0, 0), (0, 0), (0, LANE - MLA_NOPE)))
    wuk = wuk.reshape(MLA_KV_RANK, MLA_HEADS * LANE).astype(BF16)
    wuv = jnp.pad(wkv[:, :, MLA_NOPE:], ((0, 0), (0, 0), (0, LANE - MLA_V)))
    wuv = wuv.reshape(MLA_KV_RANK, MLA_HEADS * LANE).astype(BF16)
    vone = jnp.tile(jnp.zeros((LANE,), F32).at[MLA_V].set(1.0), MLA_HEADS).reshape(1, MLA_HEADS * LANE)
    return win, wuq, wuk, wuv, vone


def kernel(x, norm_mix, norm_ffn, norm_final, ev_w_in, ev_conv_w, ev_conv_b, ev_cnorm_g, ev_cnorm_b,
           ev_vnorm_g, ev_vnorm_b, ev_w_sp, ev_b_sp, ev_w_out, od_w_in, od_g_cq, od_w_uq, od_g_ckv,
           od_w_ukv, od_sink, od_w_out, moe_w_group, moe_b_group, moe_w_router, moe_b_router,
           moe_w_gate, moe_w_up, moe_w_down):
    b, seq, d = x.shape
    t = b * seq
    depth = norm_mix.shape[0]
    row = lambda v: v.reshape(1, -1)
    xt = x.reshape(t, d)
    gfin = row(norm_final)
    wg_all = moe_w_gate.reshape(depth * N_EXPERTS, d, -1)
    wu_all = moe_w_up.reshape(depth * N_EXPERTS, d, -1)
    wd_all = moe_w_down.reshape(depth * N_EXPERTS, -1, d)
    for layer in range(depth):
        k = layer // 2
        wrh, wrl, br = _router_weights(moe_w_group[layer], moe_b_group[layer], moe_w_router[layer],
                                       moe_b_router[layer])
        gffn = row(norm_ffn[layer])
        if layer % 2 == 0:
            bsp = jnp.broadcast_to(ev_b_sp[k][:, :, None], ev_w_sp[k].shape)
            xt, h3, lg = _even_mixer(
                xt, seq, row(norm_mix[layer]), ev_w_in[k].astype(BF16), ev_conv_w[k], row(ev_conv_b[k]),
                row(ev_cnorm_g[k]), row(ev_cnorm_b[k]), row(ev_vnorm_g[k]), row(ev_vnorm_b[k]),
                ev_w_sp[k].astype(BF16), bsp, ev_w_out[k].astype(BF16), gffn, wrh, wrl, br)
        else:
            win, wuq, wuk, wuv, vone = _odd_weights(od_w_in[k], od_w_uq[k], od_w_ukv[k])
            ctab, atab, btab = _rope_tables(seq)
            qm, km, vm, qs, ks, vs = _odd_proj(
                xt, seq, row(norm_mix[layer]), win, row(od_g_cq[k]), wuq, row(od_g_ckv[k]), wuk, wuv,
                vone, ctab, atab, btab)
            xt, h3, lg = _attention(xt, seq, od_sink[k], qm, km, vm, qs, ks, vs,
                                    od_w_out[k].astype(BF16), gffn, wrh, wrl, br)
        xt = _moe(xt, h3, lg, wg_all, wu_all, wd_all, layer, gfin, layer == depth - 1)
    return xt.reshape(b, seq, d)
```

```python
import functools

import jax
import jax.numpy as jnp
from jax import lax
from jax.experimental import pallas as pl
from jax.experimental.pallas import tpu as pltpu

F32 = jnp.float32
BF16 = jnp.bfloat16
I32 = jnp.int32

EPS = 1e-6
NEG_INF = -1e30
LOG2E = 1.4426950408889634
LANE = 128
VMEM_LIMIT = 56 * 1024 * 1024

CONV_DIM = 512
CONV_GROUP = 128
CONV_WIDTH = 31
SGU_DIM = 512
SGU_HEAD = 128
CHUNK = 128
MLA_HEADS = 8
MLA_Q_RANK = 256
MLA_KV_RANK = 128
MLA_NOPE = 64
MLA_ROPE = 32
MLA_V = 64
ROPE_THETA = 10000.0
SWA_HEADS = 8
SWA_KV_HEADS = 2
SWA_DIM = 64
WINDOW = 128
N_GROUPS = 4
N_EXP = 8
N_EXPERTS = N_GROUPS * N_EXP

HALO = 16
CONV_ROWS = 64
MOE_TILE = 512
XBUF_SLOTS = 3
ROW_TILE = 256
TAB_ROWS = 8
DISPATCH_TILE = 512


def _dot(a, b):
    return jnp.dot(a, b, preferred_element_type=F32)


def _dot_nt(a, b):
    return lax.dot_general(a, b, (((1,), (1,)), ((), ())), preferred_element_type=F32)


def _rms(x, g):
    return x * lax.rsqrt(jnp.mean(x * x, axis=-1, keepdims=True) + EPS) * g


def _ln(x, g, b):
    mu = jnp.mean(x, axis=-1, keepdims=True)
    d = x - mu
    var = jnp.mean(d * d, axis=-1, keepdims=True)
    return d * lax.rsqrt(var + EPS) * g + b


def _gelu(x):
    return 0.5 * x * (1.0 + jnp.tanh(0.7978845608028654 * (x + 0.044715 * (x * x * x))))


def _sigmoid(x):
    return 1.0 / (1.0 + jnp.exp(-x))


def _router_epilogue(xn, gffn_ref, wrh_ref, wrl_ref, br_ref, h3_ref, lg_ref):
    h = _rms(xn, gffn_ref[...])
    h3_ref[...] = h
    hi = h.astype(BF16)
    lo = (h - hi.astype(F32)).astype(BF16)
    wh = wrh_ref[...]
    lg_ref[...] = _dot(hi, wh) + _dot(lo, wh) + _dot(hi, wrl_ref[...]) + br_ref[...]


def _even_kernel(xc_ref, xp_ref, gmix_ref, win_ref, cw_ref, cb_ref, cng_ref, cnb_ref, vng_ref,
                 vnb_ref, wsp_ref, bsp_ref, wout_ref, gffn_ref, wrh_ref, wrl_ref, br_ref,
                 xo_ref, h3_ref, lg_ref, a_s, mix_s, *, ns, sub, seq):
    j = pl.program_id(1)
    win_rows = CONV_ROWS + 2 * HALO
    n_chunks = sub // CONV_ROWS

    def proj_conv_input():
        h = _rms(xc_ref[...], gmix_ref[...]).astype(BF16)
        pa = _dot(h, win_ref[:, 0:2 * CONV_DIM])
        return h, pa[:, :CONV_DIM] * _sigmoid(pa[:, CONV_DIM:])

    def proj_gating(h, a):
        r0 = pl.multiple_of(j * sub, sub)
        a_s[pl.ds(HALO + r0, sub), :] = a
        u = _gelu(_dot(h, win_ref[:, 2 * CONV_DIM:2 * CONV_DIM + SGU_DIM]))
        v = _gelu(_dot(h, win_ref[:, 2 * CONV_DIM + SGU_DIM:]))
        for hd in range(SGU_DIM // SGU_HEAD):
            cs = slice(hd * SGU_HEAD, (hd + 1) * SGU_HEAD)
            vn = _ln(v[:, cs], vng_ref[:, cs], vnb_ref[:, cs]).astype(BF16)
            for c in range(sub // CHUNK):
                rs = slice(c * CHUNK, (c + 1) * CHUNK)
                sv = _dot(wsp_ref[hd], vn[rs, :]) + bsp_ref[hd]
                mix_s[pl.ds(r0 + c * CHUNK, CHUNK),
                      CONV_DIM + hd * SGU_HEAD:CONV_DIM + (hd + 1) * SGU_HEAD] = (u[rs, cs] * sv).astype(BF16)

    def conv_and_out(next_halo):
        r0 = (j - 1) * sub
        for i in range(n_chunks):
            base = pl.multiple_of(r0 + i * CONV_ROWS, CONV_ROWS)
            if i < n_chunks - 1:
                win = a_s[pl.ds(base, win_rows), :]
            else:
                win = jnp.concatenate([a_s[pl.ds(base, CONV_ROWS + HALO), :], next_halo], axis=0)
            acc = jnp.zeros((CONV_ROWS, CONV_DIM), F32)
            for s in range(8):
                sh = win if s == 0 else pltpu.roll(win, win_rows - s, 0)
                for m in range(4):
                    k = 8 * m + s - 1
                    if 0 <= k < CONV_WIDTH:
                        acc = acc + sh[8 * m:8 * m + CONV_ROWS, :] * cw_ref[k:k + 1, :]
            acc = acc + cb_ref[...]
            for g in range(CONV_DIM // CONV_GROUP):
                cs = slice(g * CONV_GROUP, (g + 1) * CONV_GROUP)
                y = _ln(acc[:, cs], cng_ref[:, cs], cnb_ref[:, cs])
                mix_s[pl.ds(base, CONV_ROWS), cs] = (y * _sigmoid(y)).astype(BF16)
        mix = mix_s[pl.ds(pl.multiple_of(r0, sub), sub), :]
        xn = xp_ref[...] + _dot(mix, wout_ref[...])
        xo_ref[...] = xn
        _router_epilogue(xn, gffn_ref, wrh_ref, wrl_ref, br_ref, h3_ref, lg_ref)

    @pl.when(j == 0)
    def _():
        a_s[0:HALO, :] = jnp.zeros((HALO, CONV_DIM), F32)
        h, a = proj_conv_input()
        proj_gating(h, a)

    @pl.when((j >= 1) & (j < ns))
    def _():
        h, a = proj_conv_input()
        conv_and_out(a[0:HALO, :])
        proj_gating(h, a)

    @pl.when(j == ns)
    def _():
        conv_and_out(jnp.zeros((HALO, CONV_DIM), F32))


def _const_spec(shape):
    nd = len(shape)
    return pl.BlockSpec(shape, lambda *_: (0,) * nd)


def _even_mixer(x, seq, gmix, win, cw, cb, cng, cnb, vng, vnb, wsp, bsp, wout, gffn, wrh, wrl, br):
    t, d = x.shape
    nb = t // seq
    sub = 512
    ns = seq // sub
    row = lambda b, j: (b * ns + jnp.minimum(j, ns - 1), 0)
    prev = lambda b, j: (b * ns + jnp.maximum(j - 1, 0), 0)
    consts = (gmix, win, cw, cb, cng, cnb, vng, vnb, wsp, bsp, wout, gffn, wrh, wrl, br)
    return pl.pallas_call(
        functools.partial(_even_kernel, ns=ns, sub=sub, seq=seq),
        grid=(nb, ns + 1),
        in_specs=[pl.BlockSpec((sub, d), row), pl.BlockSpec((sub, d), prev)]
        + [_const_spec(c.shape) for c in consts],
        out_specs=[pl.BlockSpec((sub, d), prev),
                   pl.BlockSpec((sub, d), prev),
                   pl.BlockSpec((sub, LANE), prev)],
        out_shape=[jax.ShapeDtypeStruct((t, d), F32),
                   jax.ShapeDtypeStruct((t, d), F32),
                   jax.ShapeDtypeStruct((t, LANE), F32)],
        scratch_shapes=[pltpu.VMEM((seq + HALO, CONV_DIM), F32),
                        pltpu.VMEM((seq, CONV_DIM + SGU_DIM), BF16)],
        compiler_params=pltpu.CompilerParams(
            dimension_semantics=("arbitrary", "arbitrary"), vmem_limit_bytes=VMEM_LIMIT),
        name="even_mixer",
    )(x, x, *consts)


def _odd_proj_kernel(x_ref, gmix_ref, win_ref, gcq_ref, wuq_ref, gckv_ref, wuk_ref, wuv_ref, vone_ref,
                     cos_ref, sa_ref, sb_ref, qm_ref, km_ref, vm_ref, qs_ref, ks_ref, vs_ref):
    h = _rms(x_ref[...], gmix_ref[...]).astype(BF16)
    p = _dot(h, win_ref[...])
    o = 0
    cq = p[:, o:o + MLA_Q_RANK]; o += MLA_Q_RANK
    ckv = p[:, o:o + MLA_KV_RANK]; o += MLA_KV_RANK
    kpe = p[:, o:o + LANE]; o += LANE
    qs = p[:, o:o + SWA_HEADS * SWA_DIM]; o += SWA_HEADS * SWA_DIM
    ks = p[:, o:o + LANE]; o += LANE
    vs = p[:, o:o + LANE]

    cos = cos_ref[...]
    sa = sa_ref[...]
    sb = sb_ref[...]

    def rope(z, reps):
        w = z.shape[1]
        c = jnp.concatenate([cos] * reps, axis=1) if reps > 1 else cos
        a = jnp.concatenate([sa] * reps, axis=1) if reps > 1 else sa
        b = jnp.concatenate([sb] * reps, axis=1) if reps > 1 else sb
        return z * c + pltpu.roll(z, w - MLA_ROPE // 2, 1) * a + pltpu.roll(z, MLA_ROPE // 2, 1) * b

    q = _dot(_rms(cq, gcq_ref[...]).astype(BF16), wuq_ref[...])
    qm_ref[...] = (rope(q, MLA_HEADS) * ((MLA_NOPE + MLA_ROPE) ** -0.5 * LOG2E)).astype(BF16)
    ckvn = _rms(ckv, gckv_ref[...]).astype(BF16)
    kr = rope(kpe, 1)
    km_ref[...] = (_dot(ckvn, wuk_ref[...]) + jnp.concatenate([kr] * MLA_HEADS, axis=1)).astype(BF16)
    vm_ref[...] = (_dot(ckvn, wuv_ref[...]) + vone_ref[...]).astype(BF16)
    qs_ref[...] = (qs * (SWA_DIM ** -0.5 * LOG2E)).astype(BF16)
    ks_ref[...] = ks.astype(BF16)
    lane = lax.broadcasted_iota(I32, (vs.shape[0], SWA_DIM), 1)
    one = jnp.where(lane == 0, 1.0, 0.0)
    vs_ref[...] = jnp.concatenate(
        [piece for g in range(SWA_KV_HEADS) for piece in (vs[:, g * SWA_DIM:(g + 1) * SWA_DIM], one)],
        axis=1).astype(BF16)


def _odd_proj(x, seq, gmix, win, gcq, wuq, gckv, wuk, wuv, vone, cos, sa, sb):
    t, d = x.shape
    tm = 512
    nsq = seq // tm
    row = lambda i: (i, 0)
    pos = lambda i: (i % nsq, 0)
    consts = (gmix, win, gcq, wuq, gckv, wuk, wuv, vone)
    widths = (MLA_HEADS * LANE, MLA_HEADS * LANE, MLA_HEADS * LANE, SWA_HEADS * SWA_DIM, LANE,
              SWA_KV_HEADS * LANE)
    return pl.pallas_call(
        _odd_proj_kernel,
        grid=(t // tm,),
        in_specs=[pl.BlockSpec((tm, d), row)] + [_const_spec(c.shape) for c in consts]
        + [pl.BlockSpec((tm, LANE), pos)] * 3,
        out_specs=[pl.BlockSpec((tm, w), row) for w in widths],
        out_shape=[jax.ShapeDtypeStruct((t, w), BF16) for w in widths],
        compiler_params=pltpu.CompilerParams(
            dimension_semantics=("arbitrary",), vmem_limit_bytes=VMEM_LIMIT),
        name="odd_proj",
    )(x, *consts, cos, sa, sb)


def _attn_kernel(sink_ref, x_ref, qm_ref, km_ref, vm_ref, qs_ref, ks_ref, vs_ref, wout_ref,
                 gffn_ref, wrh_ref, wrl_ref, br_ref, xo_ref, h3_ref, lg_ref, *, tq, seq):
    i = pl.program_id(1)
    qm = qm_ref[...]
    outs = []
    for hd in range(MLA_HEADS):
        cs = slice(hd * LANE, (hd + 1) * LANE)
        s = _dot_nt(qm[:, cs], km_ref[:, cs])
        e = jnp.exp2((s - jnp.max(s, axis=-1, keepdims=True)).astype(BF16))
        pv = _dot(e, vm_ref[:, cs])
        outs.append(pv[:, :MLA_V] / pv[:, MLA_V:MLA_V + 1])

    span = LANE + 2 * WINDOW
    rep = SWA_HEADS // SWA_KV_HEADS
    swa_rows = []
    for blk in range(tq // LANE):
        q0 = i * tq + blk * LANE
        start = pl.multiple_of(jnp.clip(q0 - WINDOW, 0, seq - span), LANE)
        kw = ks_ref[pl.ds(start, span), :]
        vw = vs_ref[pl.ds(start, span), :]
        qpos = q0 + lax.broadcasted_iota(I32, (LANE, span), 0)
        kpos = start + lax.broadcasted_iota(I32, (LANE, span), 1)
        absd = jnp.abs(qpos - kpos).astype(F32)
        in_win = absd <= float(WINDOW)
        heads = []
        for g in range(SWA_KV_HEADS):
            hds = range(g * rep, (g + 1) * rep)
            rows = slice(blk * LANE, (blk + 1) * LANE)
            q = jnp.concatenate([qs_ref[rows, hd * SWA_DIM:(hd + 1) * SWA_DIM] for hd in hds], axis=0)
            bias = jnp.concatenate(
                [jnp.where(in_win, (-LOG2E * 2.0 ** -(hd + 1)) * absd, NEG_INF) for hd in hds], axis=0)
            sk = jnp.concatenate([jnp.full((LANE, 1), sink_ref[hd] * LOG2E, F32) for hd in hds], axis=0)
            s = _dot_nt(q, kw[:, g * SWA_DIM:(g + 1) * SWA_DIM]) + bias
            m = jnp.maximum(jnp.max(s, axis=-1, keepdims=True), sk)
            pv = _dot(jnp.exp2((s - m).astype(BF16)), vw[:, g * LANE:(g + 1) * LANE])
            o = pv[:, :SWA_DIM] / (pv[:, SWA_DIM:SWA_DIM + 1] + jnp.exp2(sk - m))
            heads += [o[r * LANE:(r + 1) * LANE] for r in range(rep)]
        swa_rows.append(jnp.concatenate(heads, axis=1))
    swa = jnp.concatenate(swa_rows, axis=0) if len(swa_rows) > 1 else swa_rows[0]

    mix = jnp.concatenate(outs + [swa], axis=1).astype(BF16)
    xn = x_ref[...] + _dot(mix, wout_ref[...])
    xo_ref[...] = xn
    _router_epilogue(xn, gffn_ref, wrh_ref, wrl_ref, br_ref, h3_ref, lg_ref)


def _attention(x, seq, sink, qm, km, vm, qs, ks, vs, wout, gffn, wrh, wrl, br):
    t, d = x.shape
    nb = t // seq
    tq = 512
    nq = seq // tq
    row = lambda b, i, *_: (b * nq + i, 0)
    bat = lambda b, i, *_: (b, 0)
    consts = (wout, gffn, wrh, wrl, br)
    grid_spec = pltpu.PrefetchScalarGridSpec(
        num_scalar_prefetch=1,
        grid=(nb, nq),
        in_specs=[pl.BlockSpec((tq, d), row),
                  pl.BlockSpec((tq, qm.shape[1]), row),
                  pl.BlockSpec((seq, km.shape[1]), bat),
                  pl.BlockSpec((seq, vm.shape[1]), bat),
                  pl.BlockSpec((tq, qs.shape[1]), row),
                  pl.BlockSpec((seq, ks.shape[1]), bat),
                  pl.BlockSpec((seq, vs.shape[1]), bat)]
        + [pl.BlockSpec(c.shape, lambda b, i, *_, n=len(c.shape): (0,) * n) for c in consts],
        out_specs=[pl.BlockSpec((tq, d), row),
                   pl.BlockSpec((tq, d), row),
                   pl.BlockSpec((tq, LANE), row)],
    )
    return pl.pallas_call(
        functools.partial(_attn_kernel, tq=tq, seq=seq),
        grid_spec=grid_spec,
        out_shape=[jax.ShapeDtypeStruct((t, d), F32),
                   jax.ShapeDtypeStruct((t, d), F32),
                   jax.ShapeDtypeStruct((t, LANE), F32)],
        compiler_params=pltpu.CompilerParams(
            dimension_semantics=("arbitrary", "arbitrary"), vmem_limit_bytes=VMEM_LIMIT),
        name="attention",
    )(sink, x, qm, km, vm, qs, ks, vs, *consts)


def _route_kernel(lg_ref, lower_ref, out_ref, tab_ref, cnt_ref, carry_ref):
    i = pl.program_id(0)
    rows = lg_ref.shape[0]
    lg = lg_ref[...]
    lane = lax.broadcasted_iota(I32, (rows, LANE), 1)
    ninf = -jnp.inf

    @pl.when(i == 0)
    def _():
        carry_ref[...] = jnp.zeros_like(carry_ref)

    gl = jnp.where(lane < N_GROUPS, lg, ninf)
    gmax = jnp.max(gl, axis=1, keepdims=True)
    gidx = jnp.min(jnp.where(gl == gmax, lane, LANE), axis=1, keepdims=True)
    gp = 1.0 / jnp.sum(jnp.exp(gl - gmax), axis=1, keepdims=True)

    first = N_GROUPS + gidx * N_EXP
    el = jnp.where((lane >= first) & (lane < first + N_EXP), lg, ninf)
    m1 = jnp.max(el, axis=1, keepdims=True)
    i1 = jnp.min(jnp.where(el == m1, lane, LANE), axis=1, keepdims=True)
    el2 = jnp.where(lane == i1, ninf, el)
    m2 = jnp.max(el2, axis=1, keepdims=True)
    i2 = jnp.min(jnp.where(el2 == m2, lane, LANE), axis=1, keepdims=True)
    e2 = jnp.exp(m2 - m1)
    w1 = gp / (1.0 + e2)
    w2 = gp * e2 / (1.0 + e2)
    id1 = i1 - N_GROUPS
    id2 = i2 - N_GROUPS
    onehot = jnp.where((lane == id1) | (lane == id2), 1.0, 0.0)

    before = _dot(lower_ref[...], onehot.astype(BF16)) + carry_ref[...]
    rank1 = jnp.sum(jnp.where(lane == id1, before, 0.0), axis=1, keepdims=True)
    rank2 = jnp.sum(jnp.where(lane == id2, before, 0.0), axis=1, keepdims=True)
    carry_ref[...] += jnp.sum(onehot, axis=0, keepdims=True)
    out = jnp.where(lane == 0, rank1, 0.0)
    out = jnp.where(lane == 1, rank2, out)
    out = jnp.where(lane == 2, w1, out)
    out = jnp.where(lane == 3, w2, out)
    out = jnp.where(lane == 4, id1.astype(F32), out)
    out = jnp.where(lane == 5, id2.astype(F32), out)
    out_ref[...] = out
    fields = out.T[0:TAB_ROWS, :]
    for c in range(rows // LANE):
        tab_ref[c] = fields[:, c * LANE:(c + 1) * LANE]
    cnt_ref[...] = carry_ref[...]


def _route(lg):
    t = lg.shape[0]
    rows = 512
    lower = jnp.tri(rows, k=-1, dtype=BF16)
    return pl.pallas_call(
        _route_kernel,
        grid=(t // rows,),
        in_specs=[pl.BlockSpec((rows, LANE), lambda i: (i, 0)),
                  pl.BlockSpec((rows, rows), lambda i: (0, 0))],
        out_specs=[pl.BlockSpec((rows, LANE), lambda i: (i, 0)),
                   pl.BlockSpec((rows // LANE, TAB_ROWS, LANE), lambda i: (i, 0, 0)),
                   pl.BlockSpec((1, LANE), lambda i: (0, 0))],
        out_shape=[jax.ShapeDtypeStruct((t, LANE), F32), jax.ShapeDtypeStruct((t // LANE, TAB_ROWS, LANE), F32),
                   jax.ShapeDtypeStruct((1, LANE), F32)],
        scratch_shapes=[pltpu.VMEM((1, LANE), F32)],
        compiler_params=pltpu.CompilerParams(dimension_semantics=("arbitrary",)),
        name="moe_route",
    )(lg, lower)


def _sorted_rows_kernel(start_ref, tab_ref, pos_ref):
    for k in range(2):
        ids = tab_ref[:, 4 + k, :]
        offs = jnp.zeros(ids.shape, I32)
        for e in range(N_EXPERTS):
            offs = jnp.where(ids == float(e), start_ref[e], offs)
        pos_ref[k] = offs + tab_ref[:, k, :].astype(I32)


def _sorted_rows(starts, tab):
    nblk = tab.shape[0]
    grid_spec = pltpu.PrefetchScalarGridSpec(
        num_scalar_prefetch=1,
        grid=(1,),
        in_specs=[pl.BlockSpec(tab.shape, lambda i, *_: (0, 0, 0))],
        out_specs=pl.BlockSpec((2, nblk, LANE), lambda i, *_: (0, 0, 0)),
    )
    return pl.pallas_call(
        _sorted_rows_kernel,
        grid_spec=grid_spec,
        out_shape=jax.ShapeDtypeStruct((2, nblk, LANE), I32),
        name="moe_sorted_rows",
    )(starts, tab)


def _row(ref, r):
    return ref.at[pl.ds(r, 1), :]


def _dispatch_kernel(pos_ref, h_ref, xs_ref, sem):
    i = pl.program_id(0)
    rows = h_ref.shape[0]
    n_tok = pos_ref.shape[0] // 2

    for r in range(rows):
        t = i * rows + r
        for k in range(2):
            pltpu.make_async_copy(_row(h_ref, r), _row(xs_ref, pos_ref[k * n_tok + t]), sem).start(priority=k)
    for _ in range(2):
        pltpu.make_async_copy(h_ref, xs_ref.at[pl.ds(0, rows), :], sem).wait()


def _dispatch(pos, h):
    t, d = h.shape
    grid_spec = pltpu.PrefetchScalarGridSpec(
        num_scalar_prefetch=1,
        grid=(t // DISPATCH_TILE,),
        in_specs=[pl.BlockSpec((DISPATCH_TILE, d), lambda i, *_: (i, 0))],
        out_specs=pl.BlockSpec(memory_space=pl.ANY),
        scratch_shapes=[pltpu.SemaphoreType.DMA],
    )
    return pl.pallas_call(
        _dispatch_kernel,
        grid_spec=grid_spec,
        out_shape=jax.ShapeDtypeStruct((2 * t, d), F32),
        compiler_params=pltpu.CompilerParams(dimension_semantics=("arbitrary",)),
        name="moe_dispatch",
    )(pos, h)


def _expert_kernel(tile_ref, exp_ref, lo_ref, hi_ref, xs_ref, wg_ref, wu_ref, wd_ref, ys_ref,
                   xbuf, sems, wg_s, wu_s, wd_s, *, n_tiles):
    i = pl.program_id(0)
    prev = jnp.maximum(i - 1, 0)
    tile = tile_ref[i]
    new_tile = (i == 0) | (tile != tile_ref[prev])
    new_exp = (i == 0) | (exp_ref[i] != exp_ref[prev])
    lo = lo_ref[i]
    hi = hi_ref[i]
    rows = ys_ref.shape[0]
    slot = tile % XBUF_SLOTS

    def fetch(tl):
        start = tl * rows if isinstance(tl, int) else pl.multiple_of(tl * rows, rows)
        return pltpu.make_async_copy(xs_ref.at[pl.ds(start, rows), :], xbuf.at[tl % XBUF_SLOTS],
                                     sems.at[tl % XBUF_SLOTS])

    @pl.when(i == 0)
    def _():
        for tl in range(min(XBUF_SLOTS - 1, n_tiles)):
            fetch(tl).start()

    @pl.when(new_tile)
    def _():
        fetch(tile).wait()

        @pl.when(tile + XBUF_SLOTS - 1 < n_tiles)
        def _():
            fetch(tile + XBUF_SLOTS - 1).start()

    @pl.when(new_exp)
    def _():
        wg_s[...] = wg_ref[0].astype(BF16)
        wu_s[...] = wu_ref[0].astype(BF16)
        wd_s[...] = wd_ref[0].astype(BF16)

    @pl.when(new_tile)
    def _():
        ys_ref[...] = jnp.zeros_like(ys_ref)

    @pl.when(hi > lo)
    def _():
        x = xbuf[slot].astype(BF16)
        g = _dot(x, wg_s[...])
        u = _dot(x, wu_s[...])
        hid = (g * _sigmoid(g) * u).astype(BF16)
        y = _dot(hid, wd_s[...])
        r = tile * rows + lax.broadcasted_iota(I32, (rows, 1), 0)
        ys_ref[...] += jnp.where((r >= lo) & (r < hi), y, 0.0)


def _experts(work, first_expert, xs, wg, wu, wd):
    n, d = xs.shape
    f = wg.shape[2]
    nwork = work[0].shape[0]
    wmap = lambda i, tl, ex, lo, hi: (first_expert + ex[i], 0, 0)
    grid_spec = pltpu.PrefetchScalarGridSpec(
        num_scalar_prefetch=4,
        grid=(nwork,),
        in_specs=[pl.BlockSpec(memory_space=pl.ANY),
                  pl.BlockSpec((1, d, f), wmap),
                  pl.BlockSpec((1, d, f), wmap),
                  pl.BlockSpec((1, f, d), wmap)],
        out_specs=pl.BlockSpec((MOE_TILE, d), lambda i, tl, ex, lo, hi: (tl[i], 0)),
        scratch_shapes=[pltpu.VMEM((XBUF_SLOTS, MOE_TILE, d), F32), pltpu.SemaphoreType.DMA((XBUF_SLOTS,)),
                        pltpu.VMEM((d, f), BF16), pltpu.VMEM((d, f), BF16), pltpu.VMEM((f, d), BF16)],
    )
    return pl.pallas_call(
        functools.partial(_expert_kernel, n_tiles=n // MOE_TILE),
        grid_spec=grid_spec,
        out_shape=jax.ShapeDtypeStruct((n, d), F32),
        compiler_params=pltpu.CompilerParams(
            dimension_semantics=("arbitrary",), vmem_limit_bytes=VMEM_LIMIT),
        name="moe_experts",
    )(*work, xs, wg, wu, wd)


def _combine_kernel(pos_ref, x_ref, rw_ref, gfin_ref, ys_ref, xo_ref, buf, sems, *, final):
    i = pl.program_id(0)
    n = pl.num_programs(0)
    rows = x_ref.shape[0]
    n_tok = pos_ref.shape[0] // 2
    slot = i % 2

    def issue(step, slt):
        for r in range(rows):
            t = step * rows + r
            for k in range(2):
                pltpu.make_async_copy(_row(ys_ref, pos_ref[k * n_tok + t]), _row(buf.at[slt], k * rows + r),
                                      sems.at[slt]).start(priority=k)

    @pl.when(i == 0)
    def _():
        issue(0, 0)

    for slt in range(2):
        @pl.when((i + 1 < n) & (slot == 1 - slt))
        def _():
            issue(i + 1, slt)

    pltpu.make_async_copy(ys_ref.at[pl.ds(0, 2 * rows), :], buf.at[slot], sems.at[slot]).wait()

    rw = rw_ref[...]
    xn = x_ref[...] + rw[:, 2:3] * buf[slot, 0:rows, :] + rw[:, 3:4] * buf[slot, rows:2 * rows, :]
    if final:
        xn = _rms(xn, gfin_ref[...])
    xo_ref[...] = xn


def _combine(pos, x, rw, gfin, ys, final):
    t, d = x.shape
    grid_spec = pltpu.PrefetchScalarGridSpec(
        num_scalar_prefetch=1,
        grid=(t // ROW_TILE,),
        in_specs=[pl.BlockSpec((ROW_TILE, d), lambda i, *_: (i, 0)),
                  pl.BlockSpec((ROW_TILE, LANE), lambda i, *_: (i, 0)),
                  pl.BlockSpec((1, d), lambda i, *_: (0, 0)),
                  pl.BlockSpec(memory_space=pl.ANY)],
        out_specs=pl.BlockSpec((ROW_TILE, d), lambda i, *_: (i, 0)),
        scratch_shapes=[pltpu.VMEM((2, 2 * ROW_TILE, d), F32), pltpu.SemaphoreType.DMA((2,))],
    )
    return pl.pallas_call(
        functools.partial(_combine_kernel, final=final),
        grid_spec=grid_spec,
        out_shape=jax.ShapeDtypeStruct((t, d), F32),
        compiler_params=pltpu.CompilerParams(
            dimension_semantics=("arbitrary",), vmem_limit_bytes=VMEM_LIMIT),
        name="moe_combine",
    )(pos, x, rw, gfin, ys)


def _worklist(cnt, n_rows):
    ends = jnp.cumsum(cnt)
    starts = ends - cnt
    n_tiles = n_rows // MOE_TILE
    tile_lo = jnp.arange(n_tiles, dtype=I32) * MOE_TILE
    exp_lo = starts[1:]
    rank_t = jnp.arange(n_tiles, dtype=I32) + jnp.sum(exp_lo[None, :] < tile_lo[:, None], axis=1)
    rank_e = jnp.arange(N_EXPERTS - 1, dtype=I32) + jnp.sum(tile_lo[None, :] <= exp_lo[:, None], axis=1)
    vals = jnp.concatenate([tile_lo, exp_lo])
    ranks = jnp.concatenate([rank_t, rank_e])
    slot = jnp.arange(vals.shape[0], dtype=I32)
    lo = jnp.sum(jnp.where(ranks[None, :] == slot[:, None], vals[None, :], 0), axis=1)
    hi = jnp.concatenate([lo[1:], jnp.full((1,), n_rows, I32)])
    tile = jnp.minimum(lo // MOE_TILE, n_tiles - 1)
    expert = jnp.minimum(jnp.sum(ends[None, :] <= lo[:, None], axis=1), N_EXPERTS - 1).astype(I32)
    return tile, expert, lo, hi


def _moe(x, h, lg, wg, wu, wd, layer, gfin, final):
    t = x.shape[0]
    rw, tab, cnt = _route(lg)
    cnt = cnt[0, :N_EXPERTS].astype(I32)
    starts = jnp.cumsum(cnt) - cnt
    pos = _sorted_rows(starts, tab).reshape(2 * t)
    xs = _dispatch(pos, h)
    work = _worklist(cnt, 2 * t)
    ys = _experts(work, layer * N_EXPERTS, xs, wg, wu, wd)
    return _combine(pos, x, rw, gfin, ys, final)


def _router_weights(w_group, b_group, w_router, b_router):
    d = w_group.shape[0]
    w = jnp.concatenate([w_group] + [w_router[g] for g in range(N_GROUPS)], axis=1)
    w = jnp.pad(w, ((0, 0), (0, LANE - w.shape[1])))
    b = jnp.concatenate([b_group, b_router.reshape(-1)])
    b = jnp.pad(b, (0, LANE - b.shape[0])).reshape(1, LANE)
    hi = w.astype(BF16)
    lo = (w - hi.astype(F32)).astype(BF16)
    return hi, lo, b


def _rope_tables(seq):
    half = MLA_ROPE // 2
    pos = jnp.arange(seq, dtype=F32)
    inv = 1.0 / (ROPE_THETA ** (jnp.arange(0, MLA_ROPE, 2, dtype=F32) / MLA_ROPE))
    ang = pos[:, None] * inv[None, :]
    cos, sin = jnp.cos(ang), jnp.sin(ang)
    z = lambda n: jnp.zeros((seq, n), F32)
    tail = LANE - MLA_NOPE - MLA_ROPE
    ctab = jnp.concatenate([jnp.ones((seq, MLA_NOPE), F32), cos, cos, z(tail)], axis=1)
    atab = jnp.concatenate([z(MLA_NOPE), -sin, z(half), z(tail)], axis=1)
    btab = jnp.concatenate([z(MLA_NOPE), z(half), sin, z(tail)], axis=1)
    return ctab, atab, btab


def _odd_weights(w_in, w_uq, w_ukv):
    d = w_in.shape[0]
    sizes = (MLA_Q_RANK, MLA_KV_RANK, MLA_ROPE, SWA_HEADS * SWA_DIM, SWA_KV_HEADS * SWA_DIM,
             SWA_KV_HEADS * SWA_DIM)
    offs = [0]
    for s in sizes:
        offs.append(offs[-1] + s)
    cq, ckv, kpe, qs, ks, vs = [w_in[:, offs[k]:offs[k + 1]] for k in range(6)]
    kpe = jnp.pad(kpe, ((0, 0), (MLA_NOPE, LANE - MLA_NOPE - MLA_ROPE)))
    win = jnp.concatenate([cq, ckv, kpe, qs, ks, vs], axis=1).astype(BF16)
    hq = MLA_NOPE + MLA_ROPE
    wuq = jnp.pad(w_uq.reshape(MLA_Q_RANK, MLA_HEADS, hq), ((0, 0), (0, 0), (0, LANE - hq)))
    wuq = wuq.reshape(MLA_Q_RANK, MLA_HEADS * LANE).astype(BF16)
    wkv = w_ukv.reshape(MLA_KV_RANK, MLA_HEADS, MLA_NOPE + MLA_V)
    wuk = jnp.pad(wkv[:, :, :MLA_NOPE], ((0, 0), (0, 0), (0, LANE - MLA_NOPE)))
    wuk = wuk.reshape(MLA_KV_RANK, MLA_HEADS * LANE).astype(BF16)
    wuv = jnp.pad(wkv[:, :, MLA_NOPE:], ((0, 0), (0, 0), (0, LANE - MLA_V)))
    wuv = wuv.reshape(MLA_KV_RANK, MLA_HEADS * LANE).astype(BF16)
    vone = jnp.tile(jnp.zeros((LANE,), F32).at[MLA_V].set(1.0), MLA_HEADS).reshape(1, MLA_HEADS * LANE)
    return win, wuq, wuk, wuv, vone


def kernel(x, norm_mix, norm_ffn, norm_final, ev_w_in, ev_conv_w, ev_conv_b, ev_cnorm_g, ev_cnorm_b,
           ev_vnorm_g, ev_vnorm_b, ev_w_sp, ev_b_sp, ev_w_out, od_w_in, od_g_cq, od_w_uq, od_g_ckv,
           od_w_ukv, od_sink, od_w_out, moe_w_group, moe_b_group, moe_w_router, moe_b_router,
           moe_w_gate, moe_w_up, moe_w_down):
    b, seq, d = x.shape
    t = b * seq
    depth = norm_mix.shape[0]
    row = lambda v: v.reshape(1, -1)
    xt = x.reshape(t, d)
    gfin = row(norm_final)
    wg_all = moe_w_gate.reshape(depth * N_EXPERTS, d, -1)
    wu_all = moe_w_up.reshape(depth * N_EXPERTS, d, -1)
    wd_all = moe_w_down.reshape(depth * N_EXPERTS, -1, d)
    for layer in range(depth):
        k = layer // 2
        wrh, wrl, br = _router_weights(moe_w_group[layer], moe_b_group[layer], moe_w_router[layer],
                                       moe_b_router[layer])
        gffn = row(norm_ffn[layer])
        if layer % 2 == 0:
            bsp = jnp.broadcast_to(ev_b_sp[k][:, :, None], ev_w_sp[k].shape)
            xt, h3, lg = _even_mixer(
                xt, seq, row(norm_mix[layer]), ev_w_in[k].astype(BF16), ev_conv_w[k], row(ev_conv_b[k]),
                row(ev_cnorm_g[k]), row(ev_cnorm_b[k]), row(ev_vnorm_g[k]), row(ev_vnorm_b[k]),
                ev_w_sp[k].astype(BF16), bsp, ev_w_out[k].astype(BF16), gffn, wrh, wrl, br)
        else:
            win, wuq, wuk, wuv, vone = _odd_weights(od_w_in[k], od_w_uq[k], od_w_ukv[k])
            ctab, atab, btab = _rope_tables(seq)
            qm, km, vm, qs, ks, vs = _odd_proj(
                xt, seq, row(norm_mix[layer]), win, row(od_g_cq[k]), wuq, row(od_g_ckv[k]), wuk, wuv,
                vone, ctab, atab, btab)
            xt, h3, lg = _attention(xt, seq, od_sink[k], qm, km, vm, qs, ks, vs,
                                    od_w_out[k].astype(BF16), gffn, wrh, wrl, br)
        xt = _moe(xt, h3, lg, wg_all, wu_all, wd_all, layer, gfin, layer == depth - 1)
    return xt.reshape(b, seq, d)
```

```python
import functools

import jax
import jax.numpy as jnp
from jax import lax
from jax.experimental import pallas as pl
from jax.experimental.pallas import tpu as pltpu

F32 = jnp.float32
BF16 = jnp.bfloat16
I32 = jnp.int32

EPS = 1e-6
NEG_INF = -1e30
LOG2E = 1.4426950408889634
LANE = 128
VMEM_LIMIT = 56 * 1024 * 1024

CONV_DIM = 512
CONV_GROUP = 128
CONV_WIDTH = 31
SGU_DIM = 512
SGU_HEAD = 128
CHUNK = 128
MLA_HEADS = 8
MLA_Q_RANK = 256
MLA_KV_RANK = 128
MLA_NOPE = 64
MLA_ROPE = 32
MLA_V = 64
ROPE_THETA = 10000.0
SWA_HEADS = 8
SWA_KV_HEADS = 2
SWA_DIM = 64
WINDOW = 128
N_GROUPS = 4
N_EXP = 8
N_EXPERTS = N_GROUPS * N_EXP

HALO = 16
CONV_ROWS = 64
MOE_TILE = 512
XBUF_SLOTS = 3
ROW_TILE = 512
TAB_ROWS = 8
DISPATCH_TILE = 1024


def _dot(a, b):
    return jnp.dot(a, b, preferred_element_type=F32)


def _dot_nt(a, b):
    return lax.dot_general(a, b, (((1,), (1,)), ((), ())), preferred_element_type=F32)


def _rms(x, g):
    return x * lax.rsqrt(jnp.mean(x * x, axis=-1, keepdims=True) + EPS) * g


def _ln(x, g, b):
    mu = jnp.mean(x, axis=-1, keepdims=True)
    d = x - mu
    var = jnp.mean(d * d, axis=-1, keepdims=True)
    return d * lax.rsqrt(var + EPS) * g + b


def _gelu(x):
    return 0.5 * x * (1.0 + jnp.tanh(0.7978845608028654 * (x + 0.044715 * (x * x * x))))


def _sigmoid(x):
    return 1.0 / (1.0 + jnp.exp(-x))


def _router_epilogue(xn, gffn_ref, wrh_ref, wrl_ref, br_ref, h3_ref, lg_ref):
    h = _rms(xn, gffn_ref[...])
    h3_ref[...] = h
    hi = h.astype(BF16)
    lo = (h - hi.astype(F32)).astype(BF16)
    wh = wrh_ref[...]
    lg_ref[...] = _dot(hi, wh) + _dot(lo, wh) + _dot(hi, wrl_ref[...]) + br_ref[...]


def _even_kernel(xc_ref, xp_ref, gmix_ref, win_ref, cw_ref, cb_ref, cng_ref, cnb_ref, vng_ref,
                 vnb_ref, wsp_ref, bsp_ref, wout_ref, gffn_ref, wrh_ref, wrl_ref, br_ref,
                 xo_ref, h3_ref, lg_ref, a_s, mix_s, *, ns, sub, seq):
    j = pl.program_id(1)
    win_rows = CONV_ROWS + 2 * HALO
    n_chunks = sub // CONV_ROWS

    def proj_conv_input():
        h = _rms(xc_ref[...], gmix_ref[...]).astype(BF16)
        pa = _dot(h, win_ref[:, 0:2 * CONV_DIM])
        return h, pa[:, :CONV_DIM] * _sigmoid(pa[:, CONV_DIM:])

    def proj_gating(h, a):
        r0 = pl.multiple_of(j * sub, sub)
        a_s[pl.ds(HALO + r0, sub), :] = a
        u = _gelu(_dot(h, win_ref[:, 2 * CONV_DIM:2 * CONV_DIM + SGU_DIM]))
        v = _gelu(_dot(h, win_ref[:, 2 * CONV_DIM + SGU_DIM:]))
        for hd in range(SGU_DIM // SGU_HEAD):
            cs = slice(hd * SGU_HEAD, (hd + 1) * SGU_HEAD)
            vn = _ln(v[:, cs], vng_ref[:, cs], vnb_ref[:, cs]).astype(BF16)
            for c in range(sub // CHUNK):
                rs = slice(c * CHUNK, (c + 1) * CHUNK)
                sv = _dot(wsp_ref[hd], vn[rs, :]) + bsp_ref[hd]
                mix_s[pl.ds(r0 + c * CHUNK, CHUNK),
                      CONV_DIM + hd * SGU_HEAD:CONV_DIM + (hd + 1) * SGU_HEAD] = (u[rs, cs] * sv).astype(BF16)

    def conv_and_out(next_halo):
        r0 = (j - 1) * sub
        for i in range(n_chunks):
            base = pl.multiple_of(r0 + i * CONV_ROWS, CONV_ROWS)
            if i < n_chunks - 1:
                win = a_s[pl.ds(base, win_rows), :]
            else:
                win = jnp.concatenate([a_s[pl.ds(base, CONV_ROWS + HALO), :], next_halo], axis=0)
            acc = jnp.zeros((CONV_ROWS, CONV_DIM), F32)
            for s in range(8):
                sh = win if s == 0 else pltpu.roll(win, win_rows - s, 0)
                for m in range(4):
                    k = 8 * m + s - 1
                    if 0 <= k < CONV_WIDTH:
                        acc = acc + sh[8 * m:8 * m + CONV_ROWS, :] * cw_ref[k:k + 1, :]
            acc = acc + cb_ref[...]
            for g in range(CONV_DIM // CONV_GROUP):
                cs = slice(g * CONV_GROUP, (g + 1) * CONV_GROUP)
                y = _ln(acc[:, cs], cng_ref[:, cs], cnb_ref[:, cs])
                mix_s[pl.ds(base, CONV_ROWS), cs] = (y * _sigmoid(y)).astype(BF16)
        mix = mix_s[pl.ds(pl.multiple_of(r0, sub), sub), :]
        xn = xp_ref[...] + _dot(mix, wout_ref[...])
        xo_ref[...] = xn
        _router_epilogue(xn, gffn_ref, wrh_ref, wrl_ref, br_ref, h3_ref, lg_ref)

    @pl.when(j == 0)
    def _():
        a_s[0:HALO, :] = jnp.zeros((HALO, CONV_DIM), F32)
        h, a = proj_conv_input()
        proj_gating(h, a)

    @pl.when((j >= 1) & (j < ns))
    def _():
        h, a = proj_conv_input()
        conv_and_out(a[0:HALO, :])
        proj_gating(h, a)

    @pl.when(j == ns)
    def _():
        conv_and_out(jnp.zeros((HALO, CONV_DIM), F32))


def _const_spec(shape):
    nd = len(shape)
    return pl.BlockSpec(shape, lambda *_: (0,) * nd)


def _even_mixer(x, seq, gmix, win, cw, cb, cng, cnb, vng, vnb, wsp, bsp, wout, gffn, wrh, wrl, br):
    t, d = x.shape
    nb = t // seq
    sub = 512
    ns = seq // sub
    row = lambda b, j: (b * ns + jnp.minimum(j, ns - 1), 0)
    prev = lambda b, j: (b * ns + jnp.maximum(j - 1, 0), 0)
    consts = (gmix, win, cw, cb, cng, cnb, vng, vnb, wsp, bsp, wout, gffn, wrh, wrl, br)
    return pl.pallas_call(
        functools.partial(_even_kernel, ns=ns, sub=sub, seq=seq),
        grid=(nb, ns + 1),
        in_specs=[pl.BlockSpec((sub, d), row), pl.BlockSpec((sub, d), prev)]
        + [_const_spec(c.shape) for c in consts],
        out_specs=[pl.BlockSpec((sub, d), prev),
                   pl.BlockSpec((sub, d), prev),
                   pl.BlockSpec((sub, LANE), prev)],
        out_shape=[jax.ShapeDtypeStruct((t, d), F32),
                   jax.ShapeDtypeStruct((t, d), F32),
                   jax.ShapeDtypeStruct((t, LANE), F32)],
        scratch_shapes=[pltpu.VMEM((seq + HALO, CONV_DIM), F32),
                        pltpu.VMEM((seq, CONV_DIM + SGU_DIM), BF16)],
        compiler_params=pltpu.CompilerParams(
            dimension_semantics=("arbitrary", "arbitrary"), vmem_limit_bytes=VMEM_LIMIT),
        name="even_mixer",
    )(x, x, *consts)


def _odd_proj_kernel(x_ref, gmix_ref, win_ref, gcq_ref, wuq_ref, gckv_ref, wuk_ref, wuv_ref, vone_ref,
                     cos_ref, sa_ref, sb_ref, qm_ref, km_ref, vm_ref, qs_ref, ks_ref, vs_ref):
    h = _rms(x_ref[...], gmix_ref[...]).astype(BF16)
    p = _dot(h, win_ref[...])
    o = 0
    cq = p[:, o:o + MLA_Q_RANK]; o += MLA_Q_RANK
    ckv = p[:, o:o + MLA_KV_RANK]; o += MLA_KV_RANK
    kpe = p[:, o:o + LANE]; o += LANE
    qs = p[:, o:o + SWA_HEADS * SWA_DIM]; o += SWA_HEADS * SWA_DIM
    ks = p[:, o:o + LANE]; o += LANE
    vs = p[:, o:o + LANE]

    cos = cos_ref[...]
    sa = sa_ref[...]
    sb = sb_ref[...]

    def rope(z, reps):
        w = z.shape[1]
        c = jnp.concatenate([cos] * reps, axis=1) if reps > 1 else cos
        a = jnp.concatenate([sa] * reps, axis=1) if reps > 1 else sa
        b = jnp.concatenate([sb] * reps, axis=1) if reps > 1 else sb
        return z * c + pltpu.roll(z, w - MLA_ROPE // 2, 1) * a + pltpu.roll(z, MLA_ROPE // 2, 1) * b

    q = _dot(_rms(cq, gcq_ref[...]).astype(BF16), wuq_ref[...])
    qm_ref[...] = (rope(q, MLA_HEADS) * ((MLA_NOPE + MLA_ROPE) ** -0.5 * LOG2E)).astype(BF16)
    ckvn = _rms(ckv, gckv_ref[...]).astype(BF16)
    kr = rope(kpe, 1)
    km_ref[...] = (_dot(ckvn, wuk_ref[...]) + jnp.concatenate([kr] * MLA_HEADS, axis=1)).astype(BF16)
    vm_ref[...] = (_dot(ckvn, wuv_ref[...]) + vone_ref[...]).astype(BF16)
    qs_ref[...] = (qs * (SWA_DIM ** -0.5 * LOG2E)).astype(BF16)
    ks_ref[...] = ks.astype(BF16)
    lane = lax.broadcasted_iota(I32, (vs.shape[0], SWA_DIM), 1)
    one = jnp.where(lane == 0, 1.0, 0.0)
    vs_ref[...] = jnp.concatenate(
        [piece for g in range(SWA_KV_HEADS) for piece in (vs[:, g * SWA_DIM:(g + 1) * SWA_DIM], one)],
        axis=1).astype(BF16)


def _odd_proj(x, seq, gmix, win, gcq, wuq, gckv, wuk, wuv, vone, cos, sa, sb):
    t, d = x.shape
    tm = 512
    nsq = seq // tm
    row = lambda i: (i, 0)
    pos = lambda i: (i % nsq, 0)
    consts = (gmix, win, gcq, wuq, gckv, wuk, wuv, vone)
    widths = (MLA_HEADS * LANE, MLA_HEADS * LANE, MLA_HEADS * LANE, SWA_HEADS * SWA_DIM, LANE,
              SWA_KV_HEADS * LANE)
    return pl.pallas_call(
        _odd_proj_kernel,
        grid=(t // tm,),
        in_specs=[pl.BlockSpec((tm, d), row)] + [_const_spec(c.shape) for c in consts]
        + [pl.BlockSpec((tm, LANE), pos)] * 3,
        out_specs=[pl.BlockSpec((tm, w), row) for w in widths],
        out_shape=[jax.ShapeDtypeStruct((t, w), BF16) for w in widths],
        compiler_params=pltpu.CompilerParams(
            dimension_semantics=("arbitrary",), vmem_limit_bytes=VMEM_LIMIT),
        name="odd_proj",
    )(x, *consts, cos, sa, sb)


def _attn_kernel(sink_ref, x_ref, qm_ref, km_ref, vm_ref, qs_ref, ks_ref, vs_ref, wout_ref,
                 gffn_ref, wrh_ref, wrl_ref, br_ref, xo_ref, h3_ref, lg_ref, *, tq, seq):
    i = pl.program_id(1)
    qm = qm_ref[...]
    outs = []
    for hd in range(MLA_HEADS):
        cs = slice(hd * LANE, (hd + 1) * LANE)
        s = _dot_nt(qm[:, cs], km_ref[:, cs])
        e = jnp.exp2((s - jnp.max(s, axis=-1, keepdims=True)).astype(BF16))
        pv = _dot(e, vm_ref[:, cs])
        outs.append(pv[:, :MLA_V] / pv[:, MLA_V:MLA_V + 1])

    span = LANE + 2 * WINDOW
    rep = SWA_HEADS // SWA_KV_HEADS
    swa_rows = []
    for blk in range(tq // LANE):
        q0 = i * tq + blk * LANE
        start = pl.multiple_of(jnp.clip(q0 - WINDOW, 0, seq - span), LANE)
        kw = ks_ref[pl.ds(start, span), :]
        vw = vs_ref[pl.ds(start, span), :]
        qpos = q0 + lax.broadcasted_iota(I32, (LANE, span), 0)
        kpos = start + lax.broadcasted_iota(I32, (LANE, span), 1)
        absd = jnp.abs(qpos - kpos).astype(F32)
        in_win = absd <= float(WINDOW)
        heads = []
        for g in range(SWA_KV_HEADS):
            hds = range(g * rep, (g + 1) * rep)
            rows = slice(blk * LANE, (blk + 1) * LANE)
            q = jnp.concatenate([qs_ref[rows, hd * SWA_DIM:(hd + 1) * SWA_DIM] for hd in hds], axis=0)
            bias = jnp.concatenate(
                [jnp.where(in_win, (-LOG2E * 2.0 ** -(hd + 1)) * absd, NEG_INF) for hd in hds], axis=0)
            sk = jnp.concatenate([jnp.full((LANE, 1), sink_ref[hd] * LOG2E, F32) for hd in hds], axis=0)
            s = _dot_nt(q, kw[:, g * SWA_DIM:(g + 1) * SWA_DIM]) + bias
            m = jnp.maximum(jnp.max(s, axis=-1, keepdims=True), sk)
            pv = _dot(jnp.exp2((s - m).astype(BF16)), vw[:, g * LANE:(g + 1) * LANE])
            o = pv[:, :SWA_DIM] / (pv[:, SWA_DIM:SWA_DIM + 1] + jnp.exp2(sk - m))
            heads += [o[r * LANE:(r + 1) * LANE] for r in range(rep)]
        swa_rows.append(jnp.concatenate(heads, axis=1))
    swa = jnp.concatenate(swa_rows, axis=0) if len(swa_rows) > 1 else swa_rows[0]

    mix = jnp.concatenate(outs + [swa], axis=1).astype(BF16)
    xn = x_ref[...] + _dot(mix, wout_ref[...])
    xo_ref[...] = xn
    _router_epilogue(xn, gffn_ref, wrh_ref, wrl_ref, br_ref, h3_ref, lg_ref)


def _attention(x, seq, sink, qm, km, vm, qs, ks, vs, wout, gffn, wrh, wrl, br):
    t, d = x.shape
    nb = t // seq
    tq = 512
    nq = seq // tq
    row = lambda b, i, *_: (b * nq + i, 0)
    bat = lambda b, i, *_: (b, 0)
    consts = (wout, gffn, wrh, wrl, br)
    grid_spec = pltpu.PrefetchScalarGridSpec(
        num_scalar_prefetch=1,
        grid=(nb, nq),
        in_specs=[pl.BlockSpec((tq, d), row),
                  pl.BlockSpec((tq, qm.shape[1]), row),
                  pl.BlockSpec((seq, km.shape[1]), bat),
                  pl.BlockSpec((seq, vm.shape[1]), bat),
                  pl.BlockSpec((tq, qs.shape[1]), row),
                  pl.BlockSpec((seq, ks.shape[1]), bat),
                  pl.BlockSpec((seq, vs.shape[1]), bat)]
        + [pl.BlockSpec(c.shape, lambda b, i, *_, n=len(c.shape): (0,) * n) for c in consts],
        out_specs=[pl.BlockSpec((tq, d), row),
                   pl.BlockSpec((tq, d), row),
                   pl.BlockSpec((tq, LANE), row)],
    )
    return pl.pallas_call(
        functools.partial(_attn_kernel, tq=tq, seq=seq),
        grid_spec=grid_spec,
        out_shape=[jax.ShapeDtypeStruct((t, d), F32),
                   jax.ShapeDtypeStruct((t, d), F32),
                   jax.ShapeDtypeStruct((t, LANE), F32)],
        compiler_params=pltpu.CompilerParams(
            dimension_semantics=("arbitrary", "arbitrary"), vmem_limit_bytes=VMEM_LIMIT),
        name="attention",
    )(sink, x, qm, km, vm, qs, ks, vs, *consts)


def _route_kernel(lg_ref, lower_ref, out_ref, tab_ref, cnt_ref, carry_ref):
    i = pl.program_id(0)
    rows = lg_ref.shape[0]
    lg = lg_ref[...]
    lane = lax.broadcasted_iota(I32, (rows, LANE), 1)
    ninf = -jnp.inf

    @pl.when(i == 0)
    def _():
        carry_ref[...] = jnp.zeros_like(carry_ref)

    gl = jnp.where(lane < N_GROUPS, lg, ninf)
    gmax = jnp.max(gl, axis=1, keepdims=True)
    gidx = jnp.min(jnp.where(gl == gmax, lane, LANE), axis=1, keepdims=True)
    gp = 1.0 / jnp.sum(jnp.exp(gl - gmax), axis=1, keepdims=True)

    first = N_GROUPS + gidx * N_EXP
    el = jnp.where((lane >= first) & (lane < first + N_EXP), lg, ninf)
    m1 = jnp.max(el, axis=1, keepdims=True)
    i1 = jnp.min(jnp.where(el == m1, lane, LANE), axis=1, keepdims=True)
    el2 = jnp.where(lane == i1, ninf, el)
    m2 = jnp.max(el2, axis=1, keepdims=True)
    i2 = jnp.min(jnp.where(el2 == m2, lane, LANE), axis=1, keepdims=True)
    e2 = jnp.exp(m2 - m1)
    w1 = gp / (1.0 + e2)
    w2 = gp * e2 / (1.0 + e2)
    id1 = i1 - N_GROUPS
    id2 = i2 - N_GROUPS
    onehot = jnp.where((lane == id1) | (lane == id2), 1.0, 0.0)

    before = _dot(lower_ref[...], onehot.astype(BF16)) + carry_ref[...]
    rank1 = jnp.sum(jnp.where(lane == id1, before, 0.0), axis=1, keepdims=True)
    rank2 = jnp.sum(jnp.where(lane == id2, before, 0.0), axis=1, keepdims=True)
    carry_ref[...] += jnp.sum(onehot, axis=0, keepdims=True)
    out = jnp.where(lane == 0, rank1, 0.0)
    out = jnp.where(lane == 1, rank2, out)
    out = jnp.where(lane == 2, w1, out)
    out = jnp.where(lane == 3, w2, out)
    out = jnp.where(lane == 4, id1.astype(F32), out)
    out = jnp.where(lane == 5, id2.astype(F32), out)
    out_ref[...] = out
    fields = out.T[0:TAB_ROWS, :]
    for c in range(rows // LANE):
        tab_ref[c] = fields[:, c * LANE:(c + 1) * LANE]
    cnt_ref[...] = carry_ref[...]


def _route(lg):
    t = lg.shape[0]
    rows = 512
    lower = jnp.tri(rows, k=-1, dtype=BF16)
    return pl.pallas_call(
        _route_kernel,
        grid=(t // rows,),
        in_specs=[pl.BlockSpec((rows, LANE), lambda i: (i, 0)),
                  pl.BlockSpec((rows, rows), lambda i: (0, 0))],
        out_specs=[pl.BlockSpec((rows, LANE), lambda i: (i, 0)),
                   pl.BlockSpec((rows // LANE, TAB_ROWS, LANE), lambda i: (i, 0, 0)),
                   pl.BlockSpec((1, LANE), lambda i: (0, 0))],
        out_shape=[jax.ShapeDtypeStruct((t, LANE), F32), jax.ShapeDtypeStruct((t // LANE, TAB_ROWS, LANE), F32),
                   jax.ShapeDtypeStruct((1, LANE), F32)],
        scratch_shapes=[pltpu.VMEM((1, LANE), F32)],
        compiler_params=pltpu.CompilerParams(dimension_semantics=("arbitrary",)),
        name="moe_route",
    )(lg, lower)


def _sorted_rows_kernel(start_ref, tab_ref, pos_ref):
    for k in range(2):
        ids = tab_ref[:, 4 + k, :]
        offs = jnp.zeros(ids.shape, I32)
        for e in range(N_EXPERTS):
            offs = jnp.where(ids == float(e), start_ref[e], offs)
        pos_ref[k] = offs + tab_ref[:, k, :].astype(I32)


def _sorted_rows(starts, tab):
    nblk = tab.shape[0]
    grid_spec = pltpu.PrefetchScalarGridSpec(
        num_scalar_prefetch=1,
        grid=(1,),
        in_specs=[pl.BlockSpec(tab.shape, lambda i, *_: (0, 0, 0))],
        out_specs=pl.BlockSpec((2, nblk, LANE), lambda i, *_: (0, 0, 0)),
    )
    return pl.pallas_call(
        _sorted_rows_kernel,
        grid_spec=grid_spec,
        out_shape=jax.ShapeDtypeStruct((2, nblk, LANE), I32),
        name="moe_sorted_rows",
    )(starts, tab)


def _row(ref, r):
    return ref.at[pl.ds(r, 1), :]


def _dispatch_kernel(pos_ref, h_ref, xs_ref, sem):
    i = pl.program_id(0)
    rows = h_ref.shape[0]
    n_tok = pos_ref.shape[0] // 2

    for r in range(rows):
        t = i * rows + r
        for k in range(2):
            pltpu.make_async_copy(_row(h_ref, r), _row(xs_ref, pos_ref[k * n_tok + t]), sem).start(priority=k)
    for _ in range(2):
        pltpu.make_async_copy(h_ref, xs_ref.at[pl.ds(0, rows), :], sem).wait()


def _dispatch(pos, h):
    t, d = h.shape
    grid_spec = pltpu.PrefetchScalarGridSpec(
        num_scalar_prefetch=1,
        grid=(t // DISPATCH_TILE,),
        in_specs=[pl.BlockSpec((DISPATCH_TILE, d), lambda i, *_: (i, 0))],
        out_specs=pl.BlockSpec(memory_space=pl.ANY),
        scratch_shapes=[pltpu.SemaphoreType.DMA],
    )
    return pl.pallas_call(
        _dispatch_kernel,
        grid_spec=grid_spec,
        out_shape=jax.ShapeDtypeStruct((2 * t, d), F32),
        compiler_params=pltpu.CompilerParams(dimension_semantics=("arbitrary",)),
        name="moe_dispatch",
    )(pos, h)


def _expert_kernel(tile_ref, exp_ref, lo_ref, hi_ref, xs_ref, wg_ref, wu_ref, wd_ref, ys_ref,
                   xbuf, sems, wg_s, wu_s, wd_s, *, n_tiles):
    i = pl.program_id(0)
    prev = jnp.maximum(i - 1, 0)
    tile = tile_ref[i]
    new_tile = (i == 0) | (tile != tile_ref[prev])
    new_exp = (i == 0) | (exp_ref[i] != exp_ref[prev])
    lo = lo_ref[i]
    hi = hi_ref[i]
    rows = ys_ref.shape[0]
    slot = tile % XBUF_SLOTS

    def fetch(tl):
        start = tl * rows if isinstance(tl, int) else pl.multiple_of(tl * rows, rows)
        return pltpu.make_async_copy(xs_ref.at[pl.ds(start, rows), :], xbuf.at[tl % XBUF_SLOTS],
                                     sems.at[tl % XBUF_SLOTS])

    @pl.when(i == 0)
    def _():
        for tl in range(min(XBUF_SLOTS - 1, n_tiles)):
            fetch(tl).start()

    @pl.when(new_tile)
    def _():
        fetch(tile).wait()

        @pl.when(tile + XBUF_SLOTS - 1 < n_tiles)
        def _():
            fetch(tile + XBUF_SLOTS - 1).start()

    @pl.when(new_exp)
    def _():
        wg_s[...] = wg_ref[0].astype(BF16)
        wu_s[...] = wu_ref[0].astype(BF16)
        wd_s[...] = wd_ref[0].astype(BF16)

    @pl.when(new_tile)
    def _():
        ys_ref[...] = jnp.zeros_like(ys_ref)

    @pl.when(hi > lo)
    def _():
        x = xbuf[slot].astype(BF16)
        g = _dot(x, wg_s[...])
        u = _dot(x, wu_s[...])
        hid = (g * _sigmoid(g) * u).astype(BF16)
        y = _dot(hid, wd_s[...])
        r = tile * rows + lax.broadcasted_iota(I32, (rows, 1), 0)
        ys_ref[...] += jnp.where((r >= lo) & (r < hi), y, 0.0)


def _experts(work, first_expert, xs, wg, wu, wd):
    n, d = xs.shape
    f = wg.shape[2]
    nwork = work[0].shape[0]
    wmap = lambda i, tl, ex, lo, hi: (first_expert + ex[i], 0, 0)
    grid_spec = pltpu.PrefetchScalarGridSpec(
        num_scalar_prefetch=4,
        grid=(nwork,),
        in_specs=[pl.BlockSpec(memory_space=pl.ANY),
                  pl.BlockSpec((1, d, f), wmap),
                  pl.BlockSpec((1, d, f), wmap),
                  pl.BlockSpec((1, f, d), wmap)],
        out_specs=pl.BlockSpec((MOE_TILE, d), lambda i, tl, ex, lo, hi: (tl[i], 0)),
        scratch_shapes=[pltpu.VMEM((XBUF_SLOTS, MOE_TILE, d), F32), pltpu.SemaphoreType.DMA((XBUF_SLOTS,)),
                        pltpu.VMEM((d, f), BF16), pltpu.VMEM((d, f), BF16), pltpu.VMEM((f, d), BF16)],
    )
    return pl.pallas_call(
        functools.partial(_expert_kernel, n_tiles=n // MOE_TILE),
        grid_spec=grid_spec,
        out_shape=jax.ShapeDtypeStruct((n, d), F32),
        compiler_params=pltpu.CompilerParams(
            dimension_semantics=("arbitrary",), vmem_limit_bytes=VMEM_LIMIT),
        name="moe_experts",
    )(*work, xs, wg, wu, wd)


def _combine_kernel(pos_ref, x_ref, rw_ref, gfin_ref, ys_ref, xo_ref, buf, sems, *, final):
    i = pl.program_id(0)
    n = pl.num_programs(0)
    rows = x_ref.shape[0]
    n_tok = pos_ref.shape[0] // 2
    slot = i % 2

    def issue(step, slt):
        for r in range(rows):
            t = step * rows + r
            for k in range(2):
                pltpu.make_async_copy(_row(ys_ref, pos_ref[k * n_tok + t]), _row(buf.at[slt], k * rows + r),
                                      sems.at[slt]).start(priority=k)

    @pl.when(i == 0)
    def _():
        issue(0, 0)

    for slt in range(2):
        @pl.when((i + 1 < n) & (slot == 1 - slt))
        def _():
            issue(i + 1, slt)

    pltpu.make_async_copy(ys_ref.at[pl.ds(0, 2 * rows), :], buf.at[slot], sems.at[slot]).wait()

    rw = rw_ref[...]
    xn = x_ref[...] + rw[:, 2:3] * buf[slot, 0:rows, :] + rw[:, 3:4] * buf[slot, rows:2 * rows, :]
    if final:
        xn = _rms(xn, gfin_ref[...])
    xo_ref[...] = xn


def _combine(pos, x, rw, gfin, ys, final):
    t, d = x.shape
    grid_spec = pltpu.PrefetchScalarGridSpec(
        num_scalar_prefetch=1,
        grid=(t // ROW_TILE,),
        in_specs=[pl.BlockSpec((ROW_TILE, d), lambda i, *_: (i, 0)),
                  pl.BlockSpec((ROW_TILE, LANE), lambda i, *_: (i, 0)),
                  pl.BlockSpec((1, d), lambda i, *_: (0, 0)),
                  pl.BlockSpec(memory_space=pl.ANY)],
        out_specs=pl.BlockSpec((ROW_TILE, d), lambda i, *_: (i, 0)),
        scratch_shapes=[pltpu.VMEM((2, 2 * ROW_TILE, d), F32), pltpu.SemaphoreType.DMA((2,))],
    )
    return pl.pallas_call(
        functools.partial(_combine_kernel, final=final),
        grid_spec=grid_spec,
        out_shape=jax.ShapeDtypeStruct((t, d), F32),
        compiler_params=pltpu.CompilerParams(
            dimension_semantics=("arbitrary",), vmem_limit_bytes=VMEM_LIMIT),
        name="moe_combine",
    )(pos, x, rw, gfin, ys)


def _worklist(cnt, n_rows):
    ends = jnp.cumsum(cnt)
    starts = ends - cnt
    n_tiles = n_rows // MOE_TILE
    tile_lo = jnp.arange(n_tiles, dtype=I32) * MOE_TILE
    exp_lo = starts[1:]
    rank_t = jnp.arange(n_tiles, dtype=I32) + jnp.sum(exp_lo[None, :] < tile_lo[:, None], axis=1)
    rank_e = jnp.arange(N_EXPERTS - 1, dtype=I32) + jnp.sum(tile_lo[None, :] <= exp_lo[:, None], axis=1)
    vals = jnp.concatenate([tile_lo, exp_lo])
    ranks = jnp.concatenate([rank_t, rank_e])
    slot = jnp.arange(vals.shape[0], dtype=I32)
    lo = jnp.sum(jnp.where(ranks[None, :] == slot[:, None], vals[None, :], 0), axis=1)
    hi = jnp.concatenate([lo[1:], jnp.full((1,), n_rows, I32)])
    tile = jnp.minimum(lo // MOE_TILE, n_tiles - 1)
    expert = jnp.minimum(jnp.sum(ends[None, :] <= lo[:, None], axis=1), N_EXPERTS - 1).astype(I32)
    return tile, expert, lo, hi


def _moe(x, h, lg, wg, wu, wd, layer, gfin, final):
    t = x.shape[0]
    rw, tab, cnt = _route(lg)
    cnt = cnt[0, :N_EXPERTS].astype(I32)
    starts = jnp.cumsum(cnt) - cnt
    pos = _sorted_rows(starts, tab).reshape(2 * t)
    xs = _dispatch(pos, h)
    work = _worklist(cnt, 2 * t)
    ys = _experts(work, layer * N_EXPERTS, xs, wg, wu, wd)
    return _combine(pos, x, rw, gfin, ys, final)


def _router_weights(w_group, b_group, w_router, b_router):
    d = w_group.shape[0]
    w = jnp.concatenate([w_group] + [w_router[g] for g in range(N_GROUPS)], axis=1)
    w = jnp.pad(w, ((0, 0), (0, LANE - w.shape[1])))
    b = jnp.concatenate([b_group, b_router.reshape(-1)])
    b = jnp.pad(b, (0, LANE - b.shape[0])).reshape(1, LANE)
    hi = w.astype(BF16)
    lo = (w - hi.astype(F32)).astype(BF16)
    return hi, lo, b


def _rope_tables(seq):
    half = MLA_ROPE // 2
    pos = jnp.arange(seq, dtype=F32)
    inv = 1.0 / (ROPE_THETA ** (jnp.arange(0, MLA_ROPE, 2, dtype=F32) / MLA_ROPE))
    ang = pos[:, None] * inv[None, :]
    cos, sin = jnp.cos(ang), jnp.sin(ang)
    z = lambda n: jnp.zeros((seq, n), F32)
    tail = LANE - MLA_NOPE - MLA_ROPE
    ctab = jnp.concatenate([jnp.ones((seq, MLA_NOPE), F32), cos, cos, z(tail)], axis=1)
    atab = jnp.concatenate([z(MLA_NOPE), -sin, z(half), z(tail)], axis=1)
    btab = jnp.concatenate([z(MLA_NOPE), z(half), sin, z(tail)], axis=1)
    return ctab, atab, btab


def _odd_weights(w_in, w_uq, w_ukv):
    d = w_in.shape[0]
    sizes = (MLA_Q_RANK, MLA_KV_RANK, MLA_ROPE, SWA_HEADS * SWA_DIM, SWA_KV_HEADS * SWA_DIM,
             SWA_KV_HEADS * SWA_DIM)
    offs = [0]
    for s in sizes:
        offs.append(offs[-1] + s)
    cq, ckv, kpe, qs, ks, vs = [w_in[:, offs[k]:offs[k + 1]] for k in range(6)]
    kpe = jnp.pad(kpe, ((0, 0), (MLA_NOPE, LANE - MLA_NOPE - MLA_ROPE)))
    win = jnp.concatenate([cq, ckv, kpe, qs, ks, vs], axis=1).astype(BF16)
    hq = MLA_NOPE + MLA_ROPE
    wuq = jnp.pad(w_uq.reshape(MLA_Q_RANK, MLA_HEADS, hq), ((0, 0), (0, 0), (0, LANE - hq)))
    wuq = wuq.reshape(MLA_Q_RANK, MLA_HEADS * LANE).astype(BF16)
    wkv = w_ukv.reshape(MLA_KV_RANK, MLA_HEADS, MLA_NOPE + MLA_V)
    wuk = jnp.pad(wkv[:, :, :MLA_NOPE], ((0, 0), (0, 0), (0, LANE - MLA_NOPE)))
    wuk = wuk.reshape(MLA_KV_RANK, MLA_HEADS * LANE).astype(BF16)
    wuv = jnp.pad(wkv[:, :, MLA_NOPE:], ((0, 0), (0, 0), (0, LANE - MLA_V)))
    wuv = wuv.reshape(MLA_KV_RANK, MLA_HEADS * LANE).astype(BF16)
    vone = jnp.tile(jnp.zeros((LANE,), F32).at[MLA_V].set(1.0), MLA_HEADS).reshape(1, MLA_HEADS * LANE)
    return win, wuq, wuk, wuv, vone


def kernel(x, norm_mix, norm_ffn, norm_final, ev_w_in, ev_conv_w, ev_conv_b, ev_cnorm_g, ev_cnorm_b,
           ev_vnorm_g, ev_vnorm_b, ev_w_sp, ev_b_sp, ev_w_out, od_w_in, od_g_cq, od_w_uq, od_g_ckv,
           od_w_ukv, od_sink, od_w_out, moe_w_group, moe_b_group, moe_w_router, moe_b_router,
           moe_w_gate, moe_w_up, moe_w_down):
    b, seq, d = x.shape
    t = b * seq
    depth = norm_mix.shape[0]
    row = lambda v: v.reshape(1, -1)
    xt = x.reshape(t, d)
    gfin = row(norm_final)
    wg_all = moe_w_gate.reshape(depth * N_EXPERTS, d, -1)
    wu_all = moe_w_up.reshape(depth * N_EXPERTS, d, -1)
    wd_all = moe_w_down.reshape(depth * N_EXPERTS, -1, d)
    for layer in range(depth):
        k = layer // 2
        wrh, wrl, br = _router_weights(moe_w_group[layer], moe_b_group[layer], moe_w_router[layer],
                                       moe_b_router[layer])
        gffn = row(norm_ffn[layer])
        if layer % 2 == 0:
            bsp = jnp.broadcast_to(ev_b_sp[k][:, :, None], ev_w_sp[k].shape)
            xt, h3, lg = _even_mixer(
                xt, seq, row(norm_mix[layer]), ev_w_in[k].astype(BF16), ev_conv_w[k], row(ev_conv_b[k]),
                row(ev_cnorm_g[k]), row(ev_cnorm_b[k]), row(ev_vnorm_g[k]), row(ev_vnorm_b[k]),
                ev_w_sp[k].astype(BF16), bsp, ev_w_out[k].astype(BF16), gffn, wrh, wrl, br)
        else:
            win, wuq, wuk, wuv, vone = _odd_weights(od_w_in[k], od_w_uq[k], od_w_ukv[k])
            ctab, atab, btab = _rope_tables(seq)
            qm, km, vm, qs, ks, vs = _odd_proj(
                xt, seq, row(norm_mix[layer]), win, row(od_g_cq[k]), wuq, row(od_g_ckv[k]), wuk, wuv,
                vone, ctab, atab, btab)
            xt, h3, lg = _attention(xt, seq, od_sink[k], qm, km, vm, qs, ks, vs,
                                    od_w_out[k].astype(BF16), gffn, wrh, wrl, br)
        xt = _moe(xt, h3, lg, wg_all, wu_all, wd_all, layer, gfin, layer == depth - 1)
    return xt.reshape(b, seq, d)
```

```python
import functools

import jax
import jax.numpy as jnp
from jax import lax
from jax.experimental import pallas as pl
from jax.experimental.pallas import tpu as pltpu

F32 = jnp.float32
BF16 = jnp.bfloat16
I32 = jnp.int32

EPS = 1e-6
NEG_INF = -1e30
LOG2E = 1.4426950408889634
LANE = 128
VMEM_LIMIT = 56 * 1024 * 1024

CONV_DIM = 512
CONV_GROUP = 128
CONV_WIDTH = 31
SGU_DIM = 512
SGU_HEAD = 128
CHUNK = 128
MLA_HEADS = 8
MLA_Q_RANK = 256
MLA_KV_RANK = 128
MLA_NOPE = 64
MLA_ROPE = 32
MLA_V = 64
ROPE_THETA = 10000.0
SWA_HEADS = 8
SWA_KV_HEADS = 2
SWA_DIM = 64
WINDOW = 128
N_GROUPS = 4
N_EXP = 8
N_EXPERTS = N_GROUPS * N_EXP

HALO = 16
CONV_ROWS = 64
MOE_TILE = 512
XBUF_SLOTS = 3
ROW_TILE = 512
TAB_ROWS = 8
DISPATCH_TILE = 2048


def _dot(a, b):
    return jnp.dot(a, b, preferred_element_type=F32)


def _dot_nt(a, b):
    return lax.dot_general(a, b, (((1,), (1,)), ((), ())), preferred_element_type=F32)


def _rms(x, g):
    return x * lax.rsqrt(jnp.mean(x * x, axis=-1, keepdims=True) + EPS) * g


def _ln(x, g, b):
    mu = jnp.mean(x, axis=-1, keepdims=True)
    d = x - mu
    var = jnp.mean(d * d, axis=-1, keepdims=True)
    return d * lax.rsqrt(var + EPS) * g + b


def _gelu(x):
    return 0.5 * x * (1.0 + jnp.tanh(0.7978845608028654 * (x + 0.044715 * (x * x * x))))


def _sigmoid(x):
    return 1.0 / (1.0 + jnp.exp(-x))


def _router_epilogue(xn, gffn_ref, wrh_ref, wrl_ref, br_ref, h3_ref, lg_ref):
    h = _rms(xn, gffn_ref[...])
    h3_ref[...] = h
    hi = h.astype(BF16)
    lo = (h - hi.astype(F32)).astype(BF16)
    wh = wrh_ref[...]
    lg_ref[...] = _dot(hi, wh) + _dot(lo, wh) + _dot(hi, wrl_ref[...]) + br_ref[...]


def _even_kernel(xc_ref, xp_ref, gmix_ref, win_ref, cw_ref, cb_ref, cng_ref, cnb_ref, vng_ref,
                 vnb_ref, wsp_ref, bsp_ref, wout_ref, gffn_ref, wrh_ref, wrl_ref, br_ref,
                 xo_ref, h3_ref, lg_ref, a_s, mix_s, *, ns, sub, seq):
    j = pl.program_id(1)
    win_rows = CONV_ROWS + 2 * HALO
    n_chunks = sub // CONV_ROWS

    def proj_conv_input():
        h = _rms(xc_ref[...], gmix_ref[...]).astype(BF16)
        pa = _dot(h, win_ref[:, 0:2 * CONV_DIM])
        return h, pa[:, :CONV_DIM] * _sigmoid(pa[:, CONV_DIM:])

    def proj_gating(h, a):
        r0 = pl.multiple_of(j * sub, sub)
        a_s[pl.ds(HALO + r0, sub), :] = a
        u = _gelu(_dot(h, win_ref[:, 2 * CONV_DIM:2 * CONV_DIM + SGU_DIM]))
        v = _gelu(_dot(h, win_ref[:, 2 * CONV_DIM + SGU_DIM:]))
        for hd in range(SGU_DIM // SGU_HEAD):
            cs = slice(hd * SGU_HEAD, (hd + 1) * SGU_HEAD)
            vn = _ln(v[:, cs], vng_ref[:, cs], vnb_ref[:, cs]).astype(BF16)
            for c in range(sub // CHUNK):
                rs = slice(c * CHUNK, (c + 1) * CHUNK)
                sv = _dot(wsp_ref[hd], vn[rs, :]) + bsp_ref[hd]
                mix_s[pl.ds(r0 + c * CHUNK, CHUNK),
                      CONV_DIM + hd * SGU_HEAD:CONV_DIM + (hd + 1) * SGU_HEAD] = (u[rs, cs] * sv).astype(BF16)

    def conv_and_out(next_halo):
        r0 = (j - 1) * sub
        for i in range(n_chunks):
            base = pl.multiple_of(r0 + i * CONV_ROWS, CONV_ROWS)
            if i < n_chunks - 1:
                win = a_s[pl.ds(base, win_rows), :]
            else:
                win = jnp.concatenate([a_s[pl.ds(base, CONV_ROWS + HALO), :], next_halo], axis=0)
            acc = jnp.zeros((CONV_ROWS, CONV_DIM), F32)
            for s in range(8):
                sh = win if s == 0 else pltpu.roll(win, win_rows - s, 0)
                for m in range(4):
                    k = 8 * m + s - 1
                    if 0 <= k < CONV_WIDTH:
                        acc = acc + sh[8 * m:8 * m + CONV_ROWS, :] * cw_ref[k:k + 1, :]
            acc = acc + cb_ref[...]
            for g in range(CONV_DIM // CONV_GROUP):
                cs = slice(g * CONV_GROUP, (g + 1) * CONV_GROUP)
                y = _ln(acc[:, cs], cng_ref[:, cs], cnb_ref[:, cs])
                mix_s[pl.ds(base, CONV_ROWS), cs] = (y * _sigmoid(y)).astype(BF16)
        mix = mix_s[pl.ds(pl.multiple_of(r0, sub), sub), :]
        xn = xp_ref[...] + _dot(mix, wout_ref[...])
        xo_ref[...] = xn
        _router_epilogue(xn, gffn_ref, wrh_ref, wrl_ref, br_ref, h3_ref, lg_ref)

    @pl.when(j == 0)
    def _():
        a_s[0:HALO, :] = jnp.zeros((HALO, CONV_DIM), F32)
        h, a = proj_conv_input()
        proj_gating(h, a)

    @pl.when((j >= 1) & (j < ns))
    def _():
        h, a = proj_conv_input()
        conv_and_out(a[0:HALO, :])
        proj_gating(h, a)

    @pl.when(j == ns)
    def _():
        conv_and_out(jnp.zeros((HALO, CONV_DIM), F32))


def _const_spec(shape):
    nd = len(shape)
    return pl.BlockSpec(shape, lambda *_: (0,) * nd)


def _even_mixer(x, seq, gmix, win, cw, cb, cng, cnb, vng, vnb, wsp, bsp, wout, gffn, wrh, wrl, br):
    t, d = x.shape
    nb = t // seq
    sub = 512
    ns = seq // sub
    row = lambda b, j: (b * ns + jnp.minimum(j, ns - 1), 0)
    prev = lambda b, j: (b * ns + jnp.maximum(j - 1, 0), 0)
    consts = (gmix, win, cw, cb, cng, cnb, vng, vnb, wsp, bsp, wout, gffn, wrh, wrl, br)
    return pl.pallas_call(
        functools.partial(_even_kernel, ns=ns, sub=sub, seq=seq),
        grid=(nb, ns + 1),
        in_specs=[pl.BlockSpec((sub, d), row), pl.BlockSpec((sub, d), prev)]
        + [_const_spec(c.shape) for c in consts],
        out_specs=[pl.BlockSpec((sub, d), prev),
                   pl.BlockSpec((sub, d), prev),
                   pl.BlockSpec((sub, LANE), prev)],
        out_shape=[jax.ShapeDtypeStruct((t, d), F32),
                   jax.ShapeDtypeStruct((t, d), F32),
                   jax.ShapeDtypeStruct((t, LANE), F32)],
        scratch_shapes=[pltpu.VMEM((seq + HALO, CONV_DIM), F32),
                        pltpu.VMEM((seq, CONV_DIM + SGU_DIM), BF16)],
        compiler_params=pltpu.CompilerParams(
            dimension_semantics=("arbitrary", "arbitrary"), vmem_limit_bytes=VMEM_LIMIT),
        name="even_mixer",
    )(x, x, *consts)


def _odd_proj_kernel(x_ref, gmix_ref, win_ref, gcq_ref, wuq_ref, gckv_ref, wuk_ref, wuv_ref, vone_ref,
                     cos_ref, sa_ref, sb_ref, qm_ref, km_ref, vm_ref, qs_ref, ks_ref, vs_ref):
    h = _rms(x_ref[...], gmix_ref[...]).astype(BF16)
    p = _dot(h, win_ref[...])
    o = 0
    cq = p[:, o:o + MLA_Q_RANK]; o += MLA_Q_RANK
    ckv = p[:, o:o + MLA_KV_RANK]; o += MLA_KV_RANK
    kpe = p[:, o:o + LANE]; o += LANE
    qs = p[:, o:o + SWA_HEADS * SWA_DIM]; o += SWA_HEADS * SWA_DIM
    ks = p[:, o:o + LANE]; o += LANE
    vs = p[:, o:o + LANE]

    cos = cos_ref[...]
    sa = sa_ref[...]
    sb = sb_ref[...]

    def rope(z, reps):
        w = z.shape[1]
        c = jnp.concatenate([cos] * reps, axis=1) if reps > 1 else cos
        a = jnp.concatenate([sa] * reps, axis=1) if reps > 1 else sa
        b = jnp.concatenate([sb] * reps, axis=1) if reps > 1 else sb
        return z * c + pltpu.roll(z, w - MLA_ROPE // 2, 1) * a + pltpu.roll(z, MLA_ROPE // 2, 1) * b

    q = _dot(_rms(cq, gcq_ref[...]).astype(BF16), wuq_ref[...])
    qm_ref[...] = (rope(q, MLA_HEADS) * ((MLA_NOPE + MLA_ROPE) ** -0.5 * LOG2E)).astype(BF16)
    ckvn = _rms(ckv, gckv_ref[...]).astype(BF16)
    kr = rope(kpe, 1)
    km_ref[...] = (_dot(ckvn, wuk_ref[...]) + jnp.concatenate([kr] * MLA_HEADS, axis=1)).astype(BF16)
    vm_ref[...] = (_dot(ckvn, wuv_ref[...]) + vone_ref[...]).astype(BF16)
    qs_ref[...] = (qs * (SWA_DIM ** -0.5 * LOG2E)).astype(BF16)
    ks_ref[...] = ks.astype(BF16)
    lane = lax.broadcasted_iota(I32, (vs.shape[0], SWA_DIM), 1)
    one = jnp.where(lane == 0, 1.0, 0.0)
    vs_ref[...] = jnp.concatenate(
        [piece for g in range(SWA_KV_HEADS) for piece in (vs[:, g * SWA_DIM:(g + 1) * SWA_DIM], one)],
        axis=1).astype(BF16)


def _odd_proj(x, seq, gmix, win, gcq, wuq, gckv, wuk, wuv, vone, cos, sa, sb):
    t, d = x.shape
    tm = 512
    nsq = seq // tm
    row = lambda i: (i, 0)
    pos = lambda i: (i % nsq, 0)
    consts = (gmix, win, gcq, wuq, gckv, wuk, wuv, vone)
    widths = (MLA_HEADS * LANE, MLA_HEADS * LANE, MLA_HEADS * LANE, SWA_HEADS * SWA_DIM, LANE,
              SWA_KV_HEADS * LANE)
    return pl.pallas_call(
        _odd_proj_kernel,
        grid=(t // tm,),
        in_specs=[pl.BlockSpec((tm, d), row)] + [_const_spec(c.shape) for c in consts]
        + [pl.BlockSpec((tm, LANE), pos)] * 3,
        out_specs=[pl.BlockSpec((tm, w), row) for w in widths],
        out_shape=[jax.ShapeDtypeStruct((t, w), BF16) for w in widths],
        compiler_params=pltpu.CompilerParams(
            dimension_semantics=("arbitrary",), vmem_limit_bytes=VMEM_LIMIT),
        name="odd_proj",
    )(x, *consts, cos, sa, sb)


def _attn_kernel(sink_ref, x_ref, qm_ref, km_ref, vm_ref, qs_ref, ks_ref, vs_ref, wout_ref,
                 gffn_ref, wrh_ref, wrl_ref, br_ref, xo_ref, h3_ref, lg_ref, *, tq, seq):
    i = pl.program_id(1)
    qm = qm_ref[...]
    outs = []
    for hd in range(MLA_HEADS):
        cs = slice(hd * LANE, (hd + 1) * LANE)
        s = _dot_nt(qm[:, cs], km_ref[:, cs])
        e = jnp.exp2((s - jnp.max(s, axis=-1, keepdims=True)).astype(BF16))
        pv = _dot(e, vm_ref[:, cs])
        outs.append(pv[:, :MLA_V] / pv[:, MLA_V:MLA_V + 1])

    span = LANE + 2 * WINDOW
    rep = SWA_HEADS // SWA_KV_HEADS
    swa_rows = []
    for blk in range(tq // LANE):
        q0 = i * tq + blk * LANE
        start = pl.multiple_of(jnp.clip(q0 - WINDOW, 0, seq - span), LANE)
        kw = ks_ref[pl.ds(start, span), :]
        vw = vs_ref[pl.ds(start, span), :]
        qpos = q0 + lax.broadcasted_iota(I32, (LANE, span), 0)
        kpos = start + lax.broadcasted_iota(I32, (LANE, span), 1)
        absd = jnp.abs(qpos - kpos).astype(F32)
        in_win = absd <= float(WINDOW)
        heads = []
        for g in range(SWA_KV_HEADS):
            hds = range(g * rep, (g + 1) * rep)
            rows = slice(blk * LANE, (blk + 1) * LANE)
            q = jnp.concatenate([qs_ref[rows, hd * SWA_DIM:(hd + 1) * SWA_DIM] for hd in hds], axis=0)
            bias = jnp.concatenate(
                [jnp.where(in_win, (-LOG2E * 2.0 ** -(hd + 1)) * absd, NEG_INF) for hd in hds], axis=0)
            sk = jnp.concatenate([jnp.full((LANE, 1), sink_ref[hd] * LOG2E, F32) for hd in hds], axis=0)
            s = _dot_nt(q, kw[:, g * SWA_DIM:(g + 1) * SWA_DIM]) + bias
            m = jnp.maximum(jnp.max(s, axis=-1, keepdims=True), sk)
            pv = _dot(jnp.exp2((s - m).astype(BF16)), vw[:, g * LANE:(g + 1) * LANE])
            o = pv[:, :SWA_DIM] / (pv[:, SWA_DIM:SWA_DIM + 1] + jnp.exp2(sk - m))
            heads += [o[r * LANE:(r + 1) * LANE] for r in range(rep)]
        swa_rows.append(jnp.concatenate(heads, axis=1))
    swa = jnp.concatenate(swa_rows, axis=0) if len(swa_rows) > 1 else swa_rows[0]

    mix = jnp.concatenate(outs + [swa], axis=1).astype(BF16)
    xn = x_ref[...] + _dot(mix, wout_ref[...])
    xo_ref[...] = xn
    _router_epilogue(xn, gffn_ref, wrh_ref, wrl_ref, br_ref, h3_ref, lg_ref)


def _attention(x, seq, sink, qm, km, vm, qs, ks, vs, wout, gffn, wrh, wrl, br):
    t, d = x.shape
    nb = t // seq
    tq = 512
    nq = seq // tq
    row = lambda b, i, *_: (b * nq + i, 0)
    bat = lambda b, i, *_: (b, 0)
    consts = (wout, gffn, wrh, wrl, br)
    grid_spec = pltpu.PrefetchScalarGridSpec(
        num_scalar_prefetch=1,
        grid=(nb, nq),
        in_specs=[pl.BlockSpec((tq, d), row),
                  pl.BlockSpec((tq, qm.shape[1]), row),
                  pl.BlockSpec((seq, km.shape[1]), bat),
                  pl.BlockSpec((seq, vm.shape[1]), bat),
                  pl.BlockSpec((tq, qs.shape[1]), row),
                  pl.BlockSpec((seq, ks.shape[1]), bat),
                  pl.BlockSpec((seq, vs.shape[1]), bat)]
        + [pl.BlockSpec(c.shape, lambda b, i, *_, n=len(c.shape): (0,) * n) for c in consts],
        out_specs=[pl.BlockSpec((tq, d), row),
                   pl.BlockSpec((tq, d), row),
                   pl.BlockSpec((tq, LANE), row)],
    )
    return pl.pallas_call(
        functools.partial(_attn_kernel, tq=tq, seq=seq),
        grid_spec=grid_spec,
        out_shape=[jax.ShapeDtypeStruct((t, d), F32),
                   jax.ShapeDtypeStruct((t, d), F32),
                   jax.ShapeDtypeStruct((t, LANE), F32)],
        compiler_params=pltpu.CompilerParams(
            dimension_semantics=("arbitrary", "arbitrary"), vmem_limit_bytes=VMEM_LIMIT),
        name="attention",
    )(sink, x, qm, km, vm, qs, ks, vs, *consts)


def _route_kernel(lg_ref, lower_ref, out_ref, tab_ref, cnt_ref, carry_ref):
    i = pl.program_id(0)
    rows = lg_ref.shape[0]
    lg = lg_ref[...]
    lane = lax.broadcasted_iota(I32, (rows, LANE), 1)
    ninf = -jnp.inf

    @pl.when(i == 0)
    def _():
        carry_ref[...] = jnp.zeros_like(carry_ref)

    gl = jnp.where(lane < N_GROUPS, lg, ninf)
    gmax = jnp.max(gl, axis=1, keepdims=True)
    gidx = jnp.min(jnp.where(gl == gmax, lane, LANE), axis=1, keepdims=True)
    gp = 1.0 / jnp.sum(jnp.exp(gl - gmax), axis=1, keepdims=True)

    first = N_GROUPS + gidx * N_EXP
    el = jnp.where((lane >= first) & (lane < first + N_EXP), lg, ninf)
    m1 = jnp.max(el, axis=1, keepdims=True)
    i1 = jnp.min(jnp.where(el == m1, lane, LANE), axis=1, keepdims=True)
    el2 = jnp.where(lane == i1, ninf, el)
    m2 = jnp.max(el2, axis=1, keepdims=True)
    i2 = jnp.min(jnp.where(el2 == m2, lane, LANE), axis=1, keepdims=True)
    e2 = jnp.exp(m2 - m1)
    w1 = gp / (1.0 + e2)
    w2 = gp * e2 / (1.0 + e2)
    id1 = i1 - N_GROUPS
    id2 = i2 - N_GROUPS
    onehot = jnp.where((lane == id1) | (lane == id2), 1.0, 0.0)

    before = _dot(lower_ref[...], onehot.astype(BF16)) + carry_ref[...]
    rank1 = jnp.sum(jnp.where(lane == id1, before, 0.0), axis=1, keepdims=True)
    rank2 = jnp.sum(jnp.where(lane == id2, before, 0.0), axis=1, keepdims=True)
    carry_ref[...] += jnp.sum(onehot, axis=0, keepdims=True)
    out = jnp.where(lane == 0, rank1, 0.0)
    out = jnp.where(lane == 1, rank2, out)
    out = jnp.where(lane == 2, w1, out)
    out = jnp.where(lane == 3, w2, out)
    out = jnp.where(lane == 4, id1.astype(F32), out)
    out = jnp.where(lane == 5, id2.astype(F32), out)
    out_ref[...] = out
    fields = out.T[0:TAB_ROWS, :]
    for c in range(rows // LANE):
        tab_ref[c] = fields[:, c * LANE:(c + 1) * LANE]
    cnt_ref[...] = carry_ref[...]


def _route(lg):
    t = lg.shape[0]
    rows = 512
    lower = jnp.tri(rows, k=-1, dtype=BF16)
    return pl.pallas_call(
        _route_kernel,
        grid=(t // rows,),
        in_specs=[pl.BlockSpec((rows, LANE), lambda i: (i, 0)),
                  pl.BlockSpec((rows, rows), lambda i: (0, 0))],
        out_specs=[pl.BlockSpec((rows, LANE), lambda i: (i, 0)),
                   pl.BlockSpec((rows // LANE, TAB_ROWS, LANE), lambda i: (i, 0, 0)),
                   pl.BlockSpec((1, LANE), lambda i: (0, 0))],
        out_shape=[jax.ShapeDtypeStruct((t, LANE), F32), jax.ShapeDtypeStruct((t // LANE, TAB_ROWS, LANE), F32),
                   jax.ShapeDtypeStruct((1, LANE), F32)],
        scratch_shapes=[pltpu.VMEM((1, LANE), F32)],
        compiler_params=pltpu.CompilerParams(dimension_semantics=("arbitrary",)),
        name="moe_route",
    )(lg, lower)


def _sorted_rows_kernel(start_ref, tab_ref, pos_ref):
    for k in range(2):
        ids = tab_ref[:, 4 + k, :]
        offs = jnp.zeros(ids.shape, I32)
        for e in range(N_EXPERTS):
            offs = jnp.where(ids == float(e), start_ref[e], offs)
        pos_ref[k] = offs + tab_ref[:, k, :].astype(I32)


def _sorted_rows(starts, tab):
    nblk = tab.shape[0]
    grid_spec = pltpu.PrefetchScalarGridSpec(
        num_scalar_prefetch=1,
        grid=(1,),
        in_specs=[pl.BlockSpec(tab.shape, lambda i, *_: (0, 0, 0))],
        out_specs=pl.BlockSpec((2, nblk, LANE), lambda i, *_: (0, 0, 0)),
    )
    return pl.pallas_call(
        _sorted_rows_kernel,
        grid_spec=grid_spec,
        out_shape=jax.ShapeDtypeStruct((2, nblk, LANE), I32),
        name="moe_sorted_rows",
    )(starts, tab)


def _row(ref, r):
    return ref.at[pl.ds(r, 1), :]


def _dispatch_kernel(pos_ref, h_ref, xs_ref, sem):
    i = pl.program_id(0)
    rows = h_ref.shape[0]
    n_tok = pos_ref.shape[0] // 2

    for r in range(rows):
        t = i * rows + r
        for k in range(2):
            pltpu.make_async_copy(_row(h_ref, r), _row(xs_ref, pos_ref[k * n_tok + t]), sem).start(priority=k)
    for _ in range(2):
        pltpu.make_async_copy(h_ref, xs_ref.at[pl.ds(0, rows), :], sem).wait()


def _dispatch(pos, h):
    t, d = h.shape
    grid_spec = pltpu.PrefetchScalarGridSpec(
        num_scalar_prefetch=1,
        grid=(t // DISPATCH_TILE,),
        in_specs=[pl.BlockSpec((DISPATCH_TILE, d), lambda i, *_: (i, 0))],
        out_specs=pl.BlockSpec(memory_space=pl.ANY),
        scratch_shapes=[pltpu.SemaphoreType.DMA],
    )
    return pl.pallas_call(
        _dispatch_kernel,
        grid_spec=grid_spec,
        out_shape=jax.ShapeDtypeStruct((2 * t, d), F32),
        compiler_params=pltpu.CompilerParams(dimension_semantics=("arbitrary",)),
        name="moe_dispatch",
    )(pos, h)


def _expert_kernel(tile_ref, exp_ref, lo_ref, hi_ref, xs_ref, wg_ref, wu_ref, wd_ref, ys_ref,
                   xbuf, sems, wg_s, wu_s, wd_s, *, n_tiles):
    i = pl.program_id(0)
    prev = jnp.maximum(i - 1, 0)
    tile = tile_ref[i]
    new_tile = (i == 0) | (tile != tile_ref[prev])
    new_exp = (i == 0) | (exp_ref[i] != exp_ref[prev])
    lo = lo_ref[i]
    hi = hi_ref[i]
    rows = ys_ref.shape[0]
    slot = tile % XBUF_SLOTS

    def fetch(tl):
        start = tl * rows if isinstance(tl, int) else pl.multiple_of(tl * rows, rows)
        return pltpu.make_async_copy(xs_ref.at[pl.ds(start, rows), :], xbuf.at[tl % XBUF_SLOTS],
                                     sems.at[tl % XBUF_SLOTS])

    @pl.when(i == 0)
    def _():
        for tl in range(min(XBUF_SLOTS - 1, n_tiles)):
            fetch(tl).start()

    @pl.when(new_tile)
    def _():
        fetch(tile).wait()

        @pl.when(tile + XBUF_SLOTS - 1 < n_tiles)
        def _():
            fetch(tile + XBUF_SLOTS - 1).start()

    @pl.when(new_exp)
    def _():
        wg_s[...] = wg_ref[0].astype(BF16)
        wu_s[...] = wu_ref[0].astype(BF16)
        wd_s[...] = wd_ref[0].astype(BF16)

    @pl.when(new_tile)
    def _():
        ys_ref[...] = jnp.zeros_like(ys_ref)

    @pl.when(hi > lo)
    def _():
        x = xbuf[slot].astype(BF16)
        g = _dot(x, wg_s[...])
        u = _dot(x, wu_s[...])
        hid = (g * _sigmoid(g) * u).astype(BF16)
        y = _dot(hid, wd_s[...])
        r = tile * rows + lax.broadcasted_iota(I32, (rows, 1), 0)
        ys_ref[...] += jnp.where((r >= lo) & (r < hi), y, 0.0)


def _experts(work, first_expert, xs, wg, wu, wd):
    n, d = xs.shape
    f = wg.shape[2]
    nwork = work[0].shape[0]
    wmap = lambda i, tl, ex, lo, hi: (first_expert + ex[i], 0, 0)
    grid_spec = pltpu.PrefetchScalarGridSpec(
        num_scalar_prefetch=4,
        grid=(nwork,),
        in_specs=[pl.BlockSpec(memory_space=pl.ANY),
                  pl.BlockSpec((1, d, f), wmap),
                  pl.BlockSpec((1, d, f), wmap),
                  pl.BlockSpec((1, f, d), wmap)],
        out_specs=pl.BlockSpec((MOE_TILE, d), lambda i, tl, ex, lo, hi: (tl[i], 0)),
        scratch_shapes=[pltpu.VMEM((XBUF_SLOTS, MOE_TILE, d), F32), pltpu.SemaphoreType.DMA((XBUF_SLOTS,)),
                        pltpu.VMEM((d, f), BF16), pltpu.VMEM((d, f), BF16), pltpu.VMEM((f, d), BF16)],
    )
    return pl.pallas_call(
        functools.partial(_expert_kernel, n_tiles=n // MOE_TILE),
        grid_spec=grid_spec,
        out_shape=jax.ShapeDtypeStruct((n, d), F32),
        compiler_params=pltpu.CompilerParams(
            dimension_semantics=("arbitrary",), vmem_limit_bytes=VMEM_LIMIT),
        name="moe_experts",
    )(*work, xs, wg, wu, wd)


def _combine_kernel(pos_ref, x_ref, rw_ref, gfin_ref, ys_ref, xo_ref, buf, sems, *, final):
    i = pl.program_id(0)
    n = pl.num_programs(0)
    rows = x_ref.shape[0]
    n_tok = pos_ref.shape[0] // 2
    slot = i % 2

    def issue(step, slt):
        for r in range(rows):
            t = step * rows + r
            for k in range(2):
                pltpu.make_async_copy(_row(ys_ref, pos_ref[k * n_tok + t]), _row(buf.at[slt], k * rows + r),
                                      sems.at[slt]).start(priority=k)

    @pl.when(i == 0)
    def _():
        issue(0, 0)

    for slt in range(2):
        @pl.when((i + 1 < n) & (slot == 1 - slt))
        def _():
            issue(i + 1, slt)

    pltpu.make_async_copy(ys_ref.at[pl.ds(0, 2 * rows), :], buf.at[slot], sems.at[slot]).wait()

    rw = rw_ref[...]
    xn = x_ref[...] + rw[:, 2:3] * buf[slot, 0:rows, :] + rw[:, 3:4] * buf[slot, rows:2 * rows, :]
    if final:
        xn = _rms(xn, gfin_ref[...])
    xo_ref[...] = xn


def _combine(pos, x, rw, gfin, ys, final):
    t, d = x.shape
    grid_spec = pltpu.PrefetchScalarGridSpec(
        num_scalar_prefetch=1,
        grid=(t // ROW_TILE,),
        in_specs=[pl.BlockSpec((ROW_TILE, d), lambda i, *_: (i, 0)),
                  pl.BlockSpec((ROW_TILE, LANE), lambda i, *_: (i, 0)),
                  pl.BlockSpec((1, d), lambda i, *_: (0, 0)),
                  pl.BlockSpec(memory_space=pl.ANY)],
        out_specs=pl.BlockSpec((ROW_TILE, d), lambda i, *_: (i, 0)),
        scratch_shapes=[pltpu.VMEM((2, 2 * ROW_TILE, d), F32), pltpu.SemaphoreType.DMA((2,))],
    )
    return pl.pallas_call(
        functools.partial(_combine_kernel, final=final),
        grid_spec=grid_spec,
        out_shape=jax.ShapeDtypeStruct((t, d), F32),
        compiler_params=pltpu.CompilerParams(
            dimension_semantics=("arbitrary",), vmem_limit_bytes=VMEM_LIMIT),
        name="moe_combine",
    )(pos, x, rw, gfin, ys)


def _worklist(cnt, n_rows):
    ends = jnp.cumsum(cnt)
    starts = ends - cnt
    n_tiles = n_rows // MOE_TILE
    tile_lo = jnp.arange(n_tiles, dtype=I32) * MOE_TILE
    exp_lo = starts[1:]
    rank_t = jnp.arange(n_tiles, dtype=I32) + jnp.sum(exp_lo[None, :] < tile_lo[:, None], axis=1)
    rank_e = jnp.arange(N_EXPERTS - 1, dtype=I32) + jnp.sum(tile_lo[None, :] <= exp_lo[:, None], axis=1)
    vals = jnp.concatenate([tile_lo, exp_lo])
    ranks = jnp.concatenate([rank_t, rank_e])
    slot = jnp.arange(vals.shape[0], dtype=I32)
    lo = jnp.sum(jnp.where(ranks[None, :] == slot[:, None], vals[None, :], 0), axis=1)
    hi = jnp.concatenate([lo[1:], jnp.full((1,), n_rows, I32)])
    tile = jnp.minimum(lo // MOE_TILE, n_tiles - 1)
    expert = jnp.minimum(jnp.sum(ends[None, :] <= lo[:, None], axis=1), N_EXPERTS - 1).astype(I32)
    return tile, expert, lo, hi


def _moe(x, h, lg, wg, wu, wd, layer, gfin, final):
    t = x.shape[0]
    rw, tab, cnt = _route(lg)
    cnt = cnt[0, :N_EXPERTS].astype(I32)
    starts = jnp.cumsum(cnt) - cnt
    pos = _sorted_rows(starts, tab).reshape(2 * t)
    xs = _dispatch(pos, h)
    work = _worklist(cnt, 2 * t)
    ys = _experts(work, layer * N_EXPERTS, xs, wg, wu, wd)
    return _combine(pos, x, rw, gfin, ys, final)


def _router_weights(w_group, b_group, w_router, b_router):
    d = w_group.shape[0]
    w = jnp.concatenate([w_group] + [w_router[g] for g in range(N_GROUPS)], axis=1)
    w = jnp.pad(w, ((0, 0), (0, LANE - w.shape[1])))
    b = jnp.concatenate([b_group, b_router.reshape(-1)])
    b = jnp.pad(b, (0, LANE - b.shape[0])).reshape(1, LANE)
    hi = w.astype(BF16)
    lo = (w - hi.astype(F32)).astype(BF16)
    return hi, lo, b


def _rope_tables(seq):
    half = MLA_ROPE // 2
    pos = jnp.arange(seq, dtype=F32)
    inv = 1.0 / (ROPE_THETA ** (jnp.arange(0, MLA_ROPE, 2, dtype=F32) / MLA_ROPE))
    ang = pos[:, None] * inv[None, :]
    cos, sin = jnp.cos(ang), jnp.sin(ang)
    z = lambda n: jnp.zeros((seq, n), F32)
    tail = LANE - MLA_NOPE - MLA_ROPE
    ctab = jnp.concatenate([jnp.ones((seq, MLA_NOPE), F32), cos, cos, z(tail)], axis=1)
    atab = jnp.concatenate([z(MLA_NOPE), -sin, z(half), z(tail)], axis=1)
    btab = jnp.concatenate([z(MLA_NOPE), z(half), sin, z(tail)], axis=1)
    return ctab, atab, btab


def _odd_weights(w_in, w_uq, w_ukv):
    d = w_in.shape[0]
    sizes = (MLA_Q_RANK, MLA_KV_RANK, MLA_ROPE, SWA_HEADS * SWA_DIM, SWA_KV_HEADS * SWA_DIM,
             SWA_KV_HEADS * SWA_DIM)
    offs = [0]
    for s in sizes:
        offs.append(offs[-1] + s)
    cq, ckv, kpe, qs, ks, vs = [w_in[:, offs[k]:offs[k + 1]] for k in range(6)]
    kpe = jnp.pad(kpe, ((0, 0), (MLA_NOPE, LANE - MLA_NOPE - MLA_ROPE)))
    win = jnp.concatenate([cq, ckv, kpe, qs, ks, vs], axis=1).astype(BF16)
    hq = MLA_NOPE + MLA_ROPE
    wuq = jnp.pad(w_uq.reshape(MLA_Q_RANK, MLA_HEADS, hq), ((0, 0), (0, 0), (0, LANE - hq)))
    wuq = wuq.reshape(MLA_Q_RANK, MLA_HEADS * LANE).astype(BF16)
    wkv = w_ukv.reshape(MLA_KV_RANK, MLA_HEADS, MLA_NOPE + MLA_V)
    wuk = jnp.pad(wkv[:, :, :MLA_NOPE], ((0, 0), (0, 0), (0, LANE - MLA_NOPE)))
    wuk = wuk.reshape(MLA_KV_RANK, MLA_HEADS * LANE).astype(BF16)
    wuv = jnp.pad(wkv[:, :, MLA_NOPE:], ((0, 0), (0, 0), (0, LANE - MLA_V)))
    wuv = wuv.reshape(MLA_KV_RANK, MLA_HEADS * LANE).astype(BF16)
    vone = jnp.tile(jnp.zeros((LANE,), F32).at[MLA_V].set(1.0), MLA_HEADS).reshape(1, MLA_HEADS * LANE)
    return win, wuq, wuk, wuv, vone


def kernel(x, norm_mix, norm_ffn, norm_final, ev_w_in, ev_conv_w, ev_conv_b, ev_cnorm_g, ev_cnorm_b,
           ev_vnorm_g, ev_vnorm_b, ev_w_sp, ev_b_sp, ev_w_out, od_w_in, od_g_cq, od_w_uq, od_g_ckv,
           od_w_ukv, od_sink, od_w_out, moe_w_group, moe_b_group, moe_w_router, moe_b_router,
           moe_w_gate, moe_w_up, moe_w_down):
    b, seq, d = x.shape
    t = b * seq
    depth = norm_mix.shape[0]
    row = lambda v: v.reshape(1, -1)
    xt = x.reshape(t, d)
    gfin = row(norm_final)
    wg_all = moe_w_gate.reshape(depth * N_EXPERTS, d, -1)
    wu_all = moe_w_up.reshape(depth * N_EXPERTS, d, -1)
    wd_all = moe_w_down.reshape(depth * N_EXPERTS, -1, d)
    for layer in range(depth):
        k = layer // 2
        wrh, wrl, br = _router_weights(moe_w_group[layer], moe_b_group[layer], moe_w_router[layer],
                                       moe_b_router[layer])
        gffn = row(norm_ffn[layer])
        if layer % 2 == 0:
            bsp = jnp.broadcast_to(ev_b_sp[k][:, :, None], ev_w_sp[k].shape)
            xt, h3, lg = _even_mixer(
                xt, seq, row(norm_mix[layer]), ev_w_in[k].astype(BF16), ev_conv_w[k], row(ev_conv_b[k]),
                row(ev_cnorm_g[k]), row(ev_cnorm_b[k]), row(ev_vnorm_g[k]), row(ev_vnorm_b[k]),
                ev_w_sp[k].astype(BF16), bsp, ev_w_out[k].astype(BF16), gffn, wrh, wrl, br)
        else:
            win, wuq, wuk, wuv, vone = _odd_weights(od_w_in[k], od_w_uq[k], od_w_ukv[k])
            ctab, atab, btab = _rope_tables(seq)
            qm, km, vm, qs, ks, vs = _odd_proj(
                xt, seq, row(norm_mix[layer]), win, row(od_g_cq[k]), wuq, row(od_g_ckv[k]), wuk, wuv,
                vone, ctab, atab, btab)
            xt, h3, lg = _attention(xt, seq, od_sink[k], qm, km, vm, qs, ks, vs,
                                    od_w_out[k].astype(BF16), gffn, wrh, wrl, br)
        xt = _moe(xt, h3, lg, wg_all, wu_all, wd_all, layer, gfin, layer == depth - 1)
    return xt.reshape(b, seq, d)
```
